```python
import math
import jax, jax.numpy as jnp
from jax import lax
import numpy as np

D_MODEL = 1024
BATCH = 16
SEQ = 2048
DEPTH = 1
DEC_BATCH = 16
DEC_SEQ = 32
PAST_LEN = 4096

CHUNK = 64
N_META = 16
Q_BLOCK = 128
EPS = 1e-6
RET_HEADS = 4
RET_DK = 128
RET_DV = 256
RET_THETA = 10000.0
DIFF_HEADS = 8
DIFF_DH = 64
ROT_DIM = DIFF_DH // 4
ROPE_THETA = 500000.0
N_EXPERTS = 32
TOP_K = 4
D_FF = 1024
SWIGLU_LIMIT = 7.0
SWIGLU_ALPHA = 1.702
MOE_BLOCK = 256

PROJ_WIDTHS = (RET_HEADS * RET_DK, RET_HEADS * RET_DK, RET_HEADS * RET_DV, RET_HEADS * RET_DV,
               DIFF_HEADS * 2 * DIFF_DH, DIFF_HEADS * 2 * DIFF_DH, DIFF_HEADS * 2 * DIFF_DH,
               D_MODEL, D_MODEL)
D_IN = sum(PROJ_WIDTHS)

kernel_name = "hybrid_retention_diffattn_moe_stream_step"

F32 = jnp.float32


def rms_norm(x, g):
    xf = x.astype(F32)
    y = xf * lax.rsqrt(jnp.mean(xf * xf, axis=-1, keepdims=True) + EPS)
    return (y * g.astype(F32)).astype(x.dtype)


def ret_log_gamma():
    return jnp.log(1.0 - 2.0 ** (-5.0 - jnp.arange(RET_HEADS, dtype=F32)))


def ret_rotate(x, pos):
    angle = RET_THETA ** (-jnp.linspace(0.0, 1.0, RET_DK // 2, dtype=F32))
    angle = jnp.repeat(angle, 2)
    ang = pos.astype(F32)[:, None] * angle[None, :]
    cos = jnp.cos(ang)[None, :, None, :]
    sin = jnp.sin(ang)[None, :, None, :]
    xf = x.astype(F32)
    rot = jnp.stack([-xf[..., 1::2], xf[..., ::2]], axis=-1).reshape(xf.shape)
    return (xf * cos + rot * sin).astype(x.dtype)


def partial_rope(x, pos):
    half = ROT_DIM // 2
    inv = ROPE_THETA ** (-jnp.arange(half, dtype=F32) * (2.0 / ROT_DIM))
    ang = pos.astype(F32)[:, None] * inv[None, :]
    cos = jnp.cos(ang)[None, :, None, None, :]
    sin = jnp.sin(ang)[None, :, None, None, :]
    xr = x[..., :ROT_DIM].astype(F32)
    x1, x2 = xr[..., :half], xr[..., half:]
    rot = jnp.concatenate([x1 * cos - x2 * sin, x2 * cos + x1 * sin], axis=-1).astype(x.dtype)
    return jnp.concatenate([rot, x[..., ROT_DIM:]], axis=-1)


def project(h, pos, w_in):
    B, S, _ = h.shape
    p = h @ w_in
    idx = [int(i) for i in np.cumsum(PROJ_WIDTHS)[:-1]]
    rq, rk, rv, rg, dq, dk, dv, ga, gb = jnp.split(p, idx, axis=-1)
    rq = ret_rotate(rq.reshape(B, S, RET_HEADS, RET_DK), pos)
    rk = ret_rotate(rk.reshape(B, S, RET_HEADS, RET_DK), pos) * (RET_DK ** -0.5)
    rv = rv.reshape(B, S, RET_HEADS, RET_DV)
    dq = partial_rope(dq.reshape(B, S, DIFF_HEADS, 2, DIFF_DH), pos)
    dk = partial_rope(dk.reshape(B, S, DIFF_HEADS, 2, DIFF_DH), pos)
    dv = dv.reshape(B, S, DIFF_HEADS, 2 * DIFF_DH)
    return rq, rk, rv, rg, dq, dk, dv, ga, gb


def retention_chunk(q, k, v, state, log_gamma):
    C = q.shape[1]
    q, k, v = q.astype(F32), k.astype(F32), v.astype(F32)
    state = state.astype(F32)
    i = jnp.arange(C, dtype=F32)
    dist = i[:, None] - i[None, :]
    decay = jnp.where(dist >= 0, jnp.exp(log_gamma[:, None, None] * jnp.maximum(dist, 0.0)), 0.0)
    scores = jnp.einsum('bihd,bjhd->bhij', q, k) * decay[None]
    o = jnp.einsum('bhij,bjhv->bihv', scores, v)
    cross_decay = jnp.exp(log_gamma[None, :] * (i[:, None] + 1.0))
    o = o + jnp.einsum('bihd,bhdv->bihv', q, state) * cross_decay[None, :, :, None]
    w = jnp.exp(log_gamma[None, :] * (C - 1.0 - i[:, None]))
    new_state = (jnp.exp(log_gamma * C)[None, :, None, None] * state
                 + jnp.einsum('bjhd,jh,bjhv->bhdv', k, w, v))
    return o, new_state


def retention_scan(q, k, v, state0, log_gamma):
    B, S = q.shape[:2]
    n = S // CHUNK

    def chunks(t):
        return t.reshape(B, n, CHUNK, t.shape[2], t.shape[3]).swapaxes(0, 1)

    def step(state, xs):
        o, state = retention_chunk(xs[0], xs[1], xs[2], state, log_gamma)
        return state, o

    state, o = lax.scan(step, state0.astype(F32), (chunks(q), chunks(k), chunks(v)))
    return o.swapaxes(0, 1).reshape(B, S, RET_HEADS, RET_DV), state


def retention_out(o, rg, w_ret_out):
    B, S = o.shape[:2]
    on = o * lax.rsqrt(jnp.mean(o * o, axis=-1, keepdims=True) + EPS)
    y = jax.nn.silu(rg.astype(F32)) * on.reshape(B, S, RET_HEADS * RET_DV)
    return y.astype(w_ret_out.dtype) @ w_ret_out


def diff_lambda(lq1, lk1, lq2, lk2, lam_init):
    e1 = jnp.exp(jnp.sum(lq1.astype(F32) * lk1.astype(F32)))
    e2 = jnp.exp(jnp.sum(lq2.astype(F32) * lk2.astype(F32)))
    return e1 - e2 + lam_init


def diff_core(q, k, v, mask, lam):
    s = jnp.einsum('bqhcd,bkhcd->bhcqk', q, k, preferred_element_type=F32) * (DIFF_DH ** -0.5)
    if mask is not None:
        s = jnp.where(mask, s, -jnp.inf)
    a = jax.nn.softmax(s, axis=-1)
    a = a[:, :, 0] - lam * a[:, :, 1]
    return jnp.einsum('bhqk,bkhv->bqhv', a, v.astype(F32))


def diff_attn_prompt(q, k_all, v_all, lam):
    B, S = q.shape[:2]
    nblk = S // Q_BLOCK
    qb = q.reshape(B, nblk, Q_BLOCK, DIFF_HEADS, 2, DIFF_DH).swapaxes(0, 1)
    key_chunk = jnp.concatenate([jnp.full((N_META,), -1, jnp.int32),
                                 jnp.arange(S, dtype=jnp.int32) // CHUNK])

    def one_block(args):
        j, qj = args
        q_chunk = (j * Q_BLOCK + jnp.arange(Q_BLOCK, dtype=jnp.int32)) // CHUNK
        mask = key_chunk[None, :] <= q_chunk[:, None]
        return diff_core(qj, k_all, v_all, mask, lam)

    o = lax.map(one_block, (jnp.arange(nblk, dtype=jnp.int32), qb))
    return o.swapaxes(0, 1).reshape(B, S, DIFF_HEADS, 2 * DIFF_DH)


def diff_out(o, g_subln, lam_init, w_diff_out):
    B, S = o.shape[:2]
    on = o * lax.rsqrt(jnp.mean(o * o, axis=-1, keepdims=True) + EPS) * g_subln.astype(F32) * (1.0 - lam_init)
    return on.reshape(B, S, DIFF_HEADS * 2 * DIFF_DH).astype(w_diff_out.dtype) @ w_diff_out


def mix_out(o_ret, rg, o_diff, ga, gb, w_ret_out, g_subln, lam_init, w_diff_out, w_out):
    y_a = retention_out(o_ret, rg, w_ret_out)
    y_b = diff_out(o_diff, g_subln, lam_init, w_diff_out)
    return (jax.nn.sigmoid(ga) * y_a + jax.nn.sigmoid(gb) * y_b) @ w_out


def moe(h, w_router, b_router, w_gu, b_gu, w_down, b_down):
    T, D = h.shape
    n_assign = T * TOP_K
    logits = (h @ w_router).astype(F32) + b_router.astype(F32)
    top_v, top_i = lax.top_k(logits, TOP_K)
    gates = jax.nn.softmax(top_v, axis=-1)
    flat_e = top_i.reshape(-1)
    order = jnp.argsort(flat_e)
    sorted_e = flat_e[order]
    counts = jnp.bincount(flat_e, length=N_EXPERTS)
    padded = (counts + MOE_BLOCK - 1) // MOE_BLOCK * MOE_BLOCK
    pend = jnp.cumsum(padded)
    start = jnp.cumsum(counts) - counts
    dest = (pend - padded)[sorted_e] + jnp.arange(n_assign, dtype=jnp.int32) - start[sorted_e]
    n_blocks = n_assign // MOE_BLOCK + N_EXPERTS
    rows = jnp.zeros((n_blocks * MOE_BLOCK, D), h.dtype).at[dest].set(h[order // TOP_K])
    block_e = jnp.minimum(jnp.searchsorted(pend, jnp.arange(n_blocks, dtype=jnp.int32) * MOE_BLOCK, side='right'),
                          N_EXPERTS - 1)

    def expert_block(args):
        xb, e = args
        gu = xb @ w_gu[e] + b_gu[e]
        gate, up = gu[:, :D_FF], gu[:, D_FF:]
        gate = jnp.minimum(gate, SWIGLU_LIMIT)
        up = jnp.clip(up, -SWIGLU_LIMIT, SWIGLU_LIMIT)
        act = (up + 1.0) * gate * jax.nn.sigmoid(SWIGLU_ALPHA * gate)
        return act @ w_down[e] + b_down[e]

    out_rows = lax.map(expert_block, (rows.reshape(n_blocks, MOE_BLOCK, D), block_e)).reshape(-1, D)
    y_assign = jnp.zeros((n_assign, D), out_rows.dtype).at[order].set(out_rows[dest])
    return jnp.einsum('tkd,tk->td', y_assign.reshape(T, TOP_K, D), gates.astype(y_assign.dtype))


def channel_mix(x, g_ffn, w_router, b_router, w_gu, b_gu, w_down, b_down):
    B, S, D = x.shape
    h = rms_norm(x, g_ffn).reshape(B * S, D)
    return x + moe(h, w_router, b_router, w_gu, b_gu, w_down, b_down).reshape(B, S, D).astype(x.dtype)


def setup_inputs(seed: int = 0) -> dict:
    key = jax.random.key(seed)
    ks = jax.random.split(key, 26)

    def nrm(k, shape, scale):
        return jax.random.normal(k, shape, F32) * scale

    def gain(k, shape):
        return 1.0 + 0.1 * jax.random.normal(k, shape, F32)

    kv_shape = (DEPTH, DEC_BATCH, PAST_LEN, DIFF_HEADS, 2 * DIFF_DH)
    return {
        "x_prompt": nrm(ks[0], (BATCH, SEQ, D_MODEL), 1.0),
        "x_sample": nrm(ks[1], (DEC_BATCH, DEC_SEQ, D_MODEL), 1.0),
        "cache_k": nrm(ks[2], kv_shape, 1.0),
        "cache_v": nrm(ks[3], kv_shape, 1.0),
        "state_ret": nrm(ks[4], (DEPTH, DEC_BATCH, RET_HEADS, RET_DK, RET_DV), 0.1),
        "meta_tokens": nrm(ks[5], (N_META, D_MODEL), 1.0),
        "g_mix": gain(ks[6], (DEPTH, D_MODEL)),
        "w_in": nrm(ks[7], (DEPTH, D_MODEL, D_IN), D_MODEL ** -0.5),
        "lam_q1": nrm(ks[8], (DEPTH, DIFF_DH), 0.1),
        "lam_k1": nrm(ks[9], (DEPTH, DIFF_DH), 0.1),
        "lam_q2": nrm(ks[10], (DEPTH, DIFF_DH), 0.1),
        "lam_k2": nrm(ks[11], (DEPTH, DIFF_DH), 0.1),
        "g_subln": gain(ks[12], (DEPTH, 2 * DIFF_DH)),
        "w_ret_out": nrm(ks[13], (DEPTH, RET_HEADS * RET_DV, D_MODEL), (RET_HEADS * RET_DV) ** -0.5),
        "w_diff_out": nrm(ks[14], (DEPTH, DIFF_HEADS * 2 * DIFF_DH, D_MODEL), (DIFF_HEADS * 2 * DIFF_DH) ** -0.5),
        "w_out": nrm(ks[15], (DEPTH, D_MODEL, D_MODEL), D_MODEL ** -0.5),
        "g_ffn": gain(ks[16], (DEPTH, D_MODEL)),
        "w_router": nrm(ks[17], (DEPTH, D_MODEL, N_EXPERTS), D_MODEL ** -0.5),
        "b_router": nrm(ks[18], (DEPTH, N_EXPERTS), 0.01),
        "w_gu": nrm(ks[19], (DEPTH, N_EXPERTS, D_MODEL, 2 * D_FF), D_MODEL ** -0.5),
        "b_gu": nrm(ks[20], (DEPTH, N_EXPERTS, 2 * D_FF), 0.01),
        "w_down": nrm(ks[21], (DEPTH, N_EXPERTS, D_FF, D_MODEL), D_FF ** -0.5),
        "b_down": nrm(ks[22], (DEPTH, N_EXPERTS, D_MODEL), 0.01),
        "g_final": gain(ks[23], (D_MODEL,)),
    }


def reference(x_prompt, x_sample, cache_k, cache_v, state_ret, meta_tokens, g_mix, w_in,
              lam_q1, lam_k1, lam_q2, lam_k2, g_subln, w_ret_out, w_diff_out, w_out,
              g_ffn, w_router, b_router, w_gu, b_gu, w_down, b_down, g_final):
    B, S, _ = x_prompt.shape
    DB, DS, _ = x_sample.shape
    PL = cache_k.shape[2]
    log_gamma = ret_log_gamma()
    pos_meta = jnp.arange(N_META, dtype=jnp.int32)
    pos_prompt = N_META + jnp.arange(S, dtype=jnp.int32)
    pos_sample = N_META + PL + jnp.arange(DS, dtype=jnp.int32)

    m = meta_tokens[None].astype(x_prompt.dtype)
    xp, xs = x_prompt, x_sample
    ret_p, k_p, v_p, ret_s, k_s, v_s = [], [], [], [], [], []

    for l in range(DEPTH):
        lam_init = 0.8 - 0.6 * math.exp(-0.3 * l)
        lam = diff_lambda(lam_q1[l], lam_k1[l], lam_q2[l], lam_k2[l], lam_init)

        mq, mk, mv, mg, mdq, mdk, mdv, mga, mgb = project(rms_norm(m, g_mix[l]), pos_meta, w_in[l])
        zero_state = jnp.zeros((1, RET_HEADS, RET_DK, RET_DV), F32)
        mo_r, s_meta = retention_chunk(mq, mk, mv, zero_state, log_gamma)
        if l + 1 < DEPTH:
            mo_d = diff_core(mdq, mdk, mdv, None, lam)
            m = m + mix_out(mo_r, mg, mo_d, mga, mgb, w_ret_out[l], g_subln[l], lam_init,
                            w_diff_out[l], w_out[l]).astype(m.dtype)
            m = channel_mix(m, g_ffn[l], w_router[l], b_router[l], w_gu[l], b_gu[l], w_down[l], b_down[l])

        rq, rk, rv, rg, dq, dk, dv, ga, gb = project(rms_norm(xp, g_mix[l]), pos_prompt, w_in[l])
        o_r, s_p = retention_scan(rq, rk, rv,
                                  jnp.broadcast_to(s_meta, (B, RET_HEADS, RET_DK, RET_DV)), log_gamma)
        k_all = jnp.concatenate([jnp.broadcast_to(mdk, (B,) + mdk.shape[1:]).astype(dk.dtype), dk], axis=1)
        v_all = jnp.concatenate([jnp.broadcast_to(mdv, (B,) + mdv.shape[1:]).astype(dv.dtype), dv], axis=1)
        o_d = diff_attn_prompt(dq, k_all, v_all, lam)
        xp = xp + mix_out(o_r, rg, o_d, ga, gb, w_ret_out[l], g_subln[l], lam_init,
                          w_diff_out[l], w_out[l]).astype(xp.dtype)
        xp = channel_mix(xp, g_ffn[l], w_router[l], b_router[l], w_gu[l], b_gu[l], w_down[l], b_down[l])
        ret_p.append(s_p)
        k_p.append(k_all.reshape(B, N_META + S, DIFF_HEADS, 2 * DIFF_DH))
        v_p.append(v_all)

        sq, sk, sv, sg, sdq, sdk, sdv, sga, sgb = project(rms_norm(xs, g_mix[l]), pos_sample, w_in[l])
        so_r, s_s = retention_chunk(sq, sk, sv, state_ret[l], log_gamma)
        ks_all = jnp.concatenate([
            jnp.broadcast_to(mdk, (DB,) + mdk.shape[1:]).astype(sdk.dtype),
            cache_k[l].reshape(DB, PL, DIFF_HEADS, 2, DIFF_DH).astype(sdk.dtype),
            sdk], axis=1)
        vs_all = jnp.concatenate([
            jnp.broadcast_to(mdv, (DB,) + mdv.shape[1:]).astype(sdv.dtype),
            cache_v[l].astype(sdv.dtype),
            sdv], axis=1)
        so_d = diff_core(sdq, ks_all, vs_all, None, lam)
        xs = xs + mix_out(so_r, sg, so_d, sga, sgb, w_ret_out[l], g_subln[l], lam_init,
                          w_diff_out[l], w_out[l]).astype(xs.dtype)
        xs = channel_mix(xs, g_ffn[l], w_router[l], b_router[l], w_gu[l], b_gu[l], w_down[l], b_down[l])
        ret_s.append(s_s)
        k_s.append(sdk.reshape(DB, DS, DIFF_HEADS, 2 * DIFF_DH))
        v_s.append(sdv)

    y_prompt = rms_norm(xp, g_final)
    y_sample = rms_norm(xs, g_final)
    return (y_prompt, y_sample, jnp.stack(ret_p), jnp.stack(k_p), jnp.stack(v_p),
            jnp.stack(ret_s), jnp.stack(k_s), jnp.stack(v_s))
```

```python
import functools
import math

import jax
import jax.numpy as jnp
from jax import lax
from jax.experimental import pallas as pl
from jax.experimental.pallas import tpu as pltpu

F32 = jnp.float32
BF16 = jnp.bfloat16

EPS = 1e-6
N_META = 16
CHUNK = 64
RET_HEADS = 4
RET_DK = 128
RET_DV = 256
RET_THETA = 10000.0
DIFF_HEADS = 8
DIFF_DH = 64
ROT_DIM = DIFF_DH // 4
ROPE_THETA = 500000.0
N_EXPERTS = 32
TOP_K = 4
SWIGLU_LIMIT = 7.0
SWIGLU_ALPHA = 1.702

LANES = 128
HEAD_W = 2 * DIFF_DH
MOE_ROWS = 256
VMEM_LIMIT = 56 * 1024 * 1024
NEG = -1e30


def _cparams(sem):
    return pltpu.CompilerParams(dimension_semantics=sem, vmem_limit_bytes=VMEM_LIMIT)


def _ret_tables(pos):
    angle = RET_THETA ** (-jnp.linspace(0.0, 1.0, RET_DK // 2, dtype=F32))
    angle = jnp.repeat(angle, 2)
    ang = pos.astype(F32)[:, None] * angle[None, :]
    cos, sin = jnp.cos(ang), jnp.sin(ang)
    even = (jnp.arange(RET_DK) % 2 == 0)[None, :]
    return cos, jnp.where(even, -sin, 0.0), jnp.where(even, 0.0, sin)


def _rope_tables(pos):
    half = ROT_DIM // 2
    inv = ROPE_THETA ** (-jnp.arange(half, dtype=F32) * (2.0 / ROT_DIM))
    ang = pos.astype(F32)[:, None] * inv[None, :]
    cos, sin = jnp.cos(ang), jnp.sin(ang)
    jj = jnp.arange(HEAD_W) % DIFF_DH
    first = (jj < half)[None, :]
    second = ((jj >= half) & (jj < ROT_DIM))[None, :]
    cos_l = jnp.take(cos, jj % half, axis=1)
    sin_l = jnp.take(sin, jj % half, axis=1)
    c = jnp.where(first | second, cos_l, 1.0)
    sa = jnp.where(first, -sin_l, 0.0)
    sb = jnp.where(second, sin_l, 0.0)
    return c, sa, sb


def _tables(pos):
    return jnp.concatenate(_ret_tables(pos) + _rope_tables(pos), axis=1)


def _proj_kernel(x_ref, g_ref, w_ref, tab_ref, rq_ref, rk_ref, rv_ref, rg_ref,
                 dq_ref, dk_ref, dv_ref, ga_ref, gb_ref):
    x = x_ref[...]
    ms = jnp.mean(x * x, axis=-1, keepdims=True)
    h = (x * lax.rsqrt(ms + EPS) * g_ref[...]).astype(BF16)

    def mm(c0, width):
        return jnp.dot(h, w_ref[:, c0:c0 + width], preferred_element_type=F32)

    def rot(p, t0, near, far):
        c = tab_ref[:, t0:t0 + LANES]
        sa = tab_ref[:, t0 + LANES:t0 + 2 * LANES]
        sb = tab_ref[:, t0 + 2 * LANES:t0 + 3 * LANES]
        return p * c + pltpu.roll(p, LANES - near, 1) * sa + pltpu.roll(p, far, 1) * sb

    kw = RET_HEADS * RET_DK
    vw = RET_HEADS * RET_DV
    dw = DIFF_HEADS * HEAD_W
    c0 = 0
    p = mm(c0, kw)
    for hh in range(RET_HEADS):
        sl = slice(hh * LANES, (hh + 1) * LANES)
        rq_ref[:, sl] = rot(p[:, sl], 0, 1, 1).astype(BF16)
    c0 += kw
    p = mm(c0, kw)
    for hh in range(RET_HEADS):
        sl = slice(hh * LANES, (hh + 1) * LANES)
        rk_ref[:, sl] = (rot(p[:, sl], 0, 1, 1) * (RET_DK ** -0.5)).astype(BF16)
    c0 += kw
    rv_ref[...] = mm(c0, vw).astype(BF16)
    c0 += vw
    rg_ref[...] = mm(c0, vw).astype(BF16)
    c0 += vw
    half = ROT_DIM // 2
    p = mm(c0, dw)
    for hh in range(DIFF_HEADS):
        sl = slice(hh * LANES, (hh + 1) * LANES)
        dq_ref[:, sl] = (rot(p[:, sl], 3 * LANES, half, half) * (DIFF_DH ** -0.5)).astype(BF16)
    c0 += dw
    p = mm(c0, dw)
    for hh in range(DIFF_HEADS):
        sl = slice(hh * LANES, (hh + 1) * LANES)
        dk_ref[:, sl] = rot(p[:, sl], 3 * LANES, half, half)
    c0 += dw
    dv_ref[...] = mm(c0, dw)
    c0 += dw
    d = x.shape[1]
    ga_ref[...] = mm(c0, d).astype(BF16)
    c0 += d
    gb_ref[...] = mm(c0, d).astype(BF16)


def _proj(x, g, w_bf, tab, tm, tab_blocks):
    t, d = x.shape
    kw, vw, dw = RET_HEADS * RET_DK, RET_HEADS * RET_DV, DIFF_HEADS * HEAD_W
    widths = (kw, kw, vw, vw, dw, dw, dw, d, d)
    dtypes = (BF16, BF16, BF16, BF16, BF16, F32, F32, BF16, BF16)
    row = lambda i: (i, 0)
    const = lambda i: (0, 0)
    return pl.pallas_call(
        _proj_kernel,
        grid=(t // tm,),
        in_specs=[pl.BlockSpec((tm, d), row),
                  pl.BlockSpec((1, d), const),
                  pl.BlockSpec(w_bf.shape, const),
                  pl.BlockSpec((tm, tab.shape[1]), lambda i: (i % tab_blocks, 0))],
        out_specs=[pl.BlockSpec((tm, w), row) for w in widths],
        out_shape=[jax.ShapeDtypeStruct((t, w), dt) for w, dt in zip(widths, dtypes)],
        compiler_params=_cparams(("arbitrary",)),
        name="proj",
    )(x, g, w_bf, tab)


def _ret_log_gammas():
    return tuple(math.log(1.0 - 2.0 ** (-5.0 - hh)) for hh in range(RET_HEADS))


def _ret_kernel(q_ref, k_ref, v_ref, g_ref, s0_ref, y_ref, st_ref, *, cb, n_valid):
    @pl.when(pl.program_id(1) == 0)
    def _():
        st_ref[...] = s0_ref[...]

    row = lax.broadcasted_iota(jnp.int32, (cb, cb), 0)
    col = lax.broadcasted_iota(jnp.int32, (cb, cb), 1)
    dist = (row - col).astype(F32)
    idx = lax.broadcasted_iota(jnp.int32, (cb, 1), 0).astype(F32)
    for hh, lg in enumerate(_ret_log_gammas()):
        q = q_ref[0, :, hh * RET_DK:(hh + 1) * RET_DK]
        k = k_ref[0, :, hh * RET_DK:(hh + 1) * RET_DK]
        v = v_ref[0, :, hh * RET_DV:(hh + 1) * RET_DV]
        g = g_ref[0, :, hh * RET_DV:(hh + 1) * RET_DV].astype(F32)
        st = st_ref[0, hh]
        decay = jnp.where(dist >= 0, jnp.exp(lg * jnp.maximum(dist, 0.0)), 0.0)
        s = lax.dot_general(q, k, (((1,), (1,)), ((), ())), preferred_element_type=F32) * decay
        o = jnp.dot(s.astype(BF16), v, preferred_element_type=F32)
        o = o + jnp.dot(q, st.astype(BF16), preferred_element_type=F32) * jnp.exp(lg * (idx + 1.0))
        kw = (k.astype(F32) * jnp.exp(lg * (n_valid - 1.0 - idx))).astype(BF16)
        st_ref[0, hh] = math.exp(lg * n_valid) * st + lax.dot_general(
            kw, v, (((0,), (0,)), ((), ())), preferred_element_type=F32)
        on = o * lax.rsqrt(jnp.mean(o * o, axis=-1, keepdims=True) + EPS)
        y_ref[0, :, hh * RET_DV:(hh + 1) * RET_DV] = (g * jax.nn.sigmoid(g) * on).astype(BF16)


def _retention(q, k, v, g, s0, cb, n_valid):
    b, s, _ = q.shape
    s0_map = (lambda i, c: (i, 0, 0, 0)) if s0.shape[0] == b else (lambda i, c: (0, 0, 0, 0))
    blk = lambda w: pl.BlockSpec((1, cb, w), lambda i, c: (i, c, 0))
    st_block = (1, RET_HEADS, RET_DK, RET_DV)
    return pl.pallas_call(
        functools.partial(_ret_kernel, cb=cb, n_valid=n_valid),
        grid=(b, s // cb),
        in_specs=[blk(q.shape[2]), blk(k.shape[2]), blk(v.shape[2]), blk(g.shape[2]),
                  pl.BlockSpec(st_block, s0_map)],
        out_specs=[blk(v.shape[2]), pl.BlockSpec(st_block, lambda i, c: (i, 0, 0, 0))],
        out_shape=[jax.ShapeDtypeStruct(v.shape, BF16),
                   jax.ShapeDtypeStruct((b,) + st_block[1:], F32)],
        compiler_params=_cparams(("arbitrary", "arbitrary")),
        name="retention",
    )(q, k, v, g, s0)


def _attn_kernel(lam_ref, q_ref, km_ref, vm_ref, ke_ref, ve_ref, gs_ref, o_ref, kb_ref, vb_ref,
                 *, sq, sk, tq, tk, n_ext, causal, coef):
    kb_ref[...] = km_ref[0].astype(BF16)
    vb_ref[...] = vm_ref[0].astype(BF16)
    ke = ke_ref[0].astype(BF16)
    ve = ve_ref[0].astype(BF16)
    lam = lam_ref[0]
    lo = lax.broadcasted_iota(jnp.int32, (1, HEAD_W), 1) < DIFF_DH
    ext_ok = lax.broadcasted_iota(jnp.int32, (1, ke.shape[0]), 1) < n_ext
    qrow = lax.broadcasted_iota(jnp.int32, (tq, tk), 0) // CHUNK
    kcol = lax.broadcasted_iota(jnp.int32, (tq, tk), 1) // CHUNK

    def scores(qq, kk):
        return lax.dot_general(qq, kk, (((1,), (1,)), ((), ())), preferred_element_type=F32)

    for qi in range(sq // tq):
        q = q_ref[0, qi * tq:(qi + 1) * tq, :]
        zero = jnp.zeros_like(q)
        qs = (jnp.where(lo, q, zero), jnp.where(lo, zero, q))

        carry = []
        for qq in qs:
            s = jnp.where(ext_ok, scores(qq, ke), NEG)
            m = jnp.max(s, axis=-1, keepdims=True)
            p = jnp.exp(s - m)
            carry += [m, jnp.sum(p, axis=-1, keepdims=True),
                      jnp.dot(p.astype(BF16), ve, preferred_element_type=F32)]

        def step(j, carry, mask=None):
            start = j * tk
            if not isinstance(start, int):
                start = pl.multiple_of(start, tk)
            kk = kb_ref[pl.ds(start, tk), :]
            vv = vb_ref[pl.ds(start, tk), :]
            new = []
            for c, qq in enumerate(qs):
                m, l, acc = carry[3 * c:3 * c + 3]
                s = scores(qq, kk)
                if mask is not None:
                    s = jnp.where(mask, s, NEG)
                m2 = jnp.maximum(m, jnp.max(s, axis=-1, keepdims=True))
                a = jnp.exp(m - m2)
                p = jnp.exp(s - m2)
                new += [m2, a * l + jnp.sum(p, axis=-1, keepdims=True),
                        a * acc + jnp.dot(p.astype(BF16), vv, preferred_element_type=F32)]
            return tuple(new)

        carry = tuple(carry)
        if causal:
            n_full = (qi * tq) // tk
            if n_full:
                carry = lax.fori_loop(0, n_full, step, carry)
            for jd in range(tq // tk):
                mask = kcol + jd * (tk // CHUNK) <= qrow
                carry = step(n_full + jd, carry, mask)
        else:
            carry = lax.fori_loop(0, sk // tk, step, carry)

        m1, l1, a1, m2, l2, a2 = carry
        o = a1 / l1 - lam * (a2 / l2)
        on = o * lax.rsqrt(jnp.mean(o * o, axis=-1, keepdims=True) + EPS) * gs_ref[...] * coef
        o_ref[0, qi * tq:(qi + 1) * tq, :] = on.astype(BF16)


def _diff_attention(q, k_main, v_main, k_ext, v_ext, n_ext, lam, g_subln, coef, causal, tq, tk):
    b, sq, hw = q.shape
    sk = k_main.shape[1]
    nh = hw // HEAD_W
    ext_map = (lambda i, h: (i, 0, h)) if k_ext.shape[0] == b else (lambda i, h: (0, 0, h))
    bh = lambda rows: pl.BlockSpec((1, rows, HEAD_W), lambda i, h: (i, 0, h))
    ext = pl.BlockSpec((1, k_ext.shape[1], HEAD_W), ext_map)
    return pl.pallas_call(
        functools.partial(_attn_kernel, sq=sq, sk=sk, tq=tq, tk=tk, n_ext=n_ext, causal=causal, coef=coef),
        grid=(b, nh),
        in_specs=[pl.BlockSpec(memory_space=pltpu.SMEM),
                  bh(sq), bh(sk), bh(sk), ext, ext,
                  pl.BlockSpec((1, HEAD_W), lambda i, h: (0, 0))],
        out_specs=bh(sq),
        out_shape=jax.ShapeDtypeStruct(q.shape, BF16),
        scratch_shapes=[pltpu.VMEM((sk, HEAD_W), BF16), pltpu.VMEM((sk, HEAD_W), BF16)],
        compiler_params=_cparams(("arbitrary", "arbitrary")),
        name="diff_attention",
    )(lam, q, k_main, v_main, k_ext, v_ext, g_subln)


def _mix_kernel(x_ref, yr_ref, od_ref, ga_ref, gb_ref, wr_ref, wd_ref, wo_ref, gf_ref, wrt_ref, br_ref,
                x1_ref, h_ref, lg_ref):
    ya = jnp.dot(yr_ref[...], wr_ref[...], preferred_element_type=F32)
    yb = jnp.dot(od_ref[...], wd_ref[...], preferred_element_type=F32)
    z = jax.nn.sigmoid(ga_ref[...].astype(F32)) * ya + jax.nn.sigmoid(gb_ref[...].astype(F32)) * yb
    x1 = x_ref[...] + jnp.dot(z.astype(BF16), wo_ref[...], preferred_element_type=F32)
    x1_ref[...] = x1
    h = x1 * lax.rsqrt(jnp.mean(x1 * x1, axis=-1, keepdims=True) + EPS) * gf_ref[...]
    h_ref[...] = h
    lg_ref[...] = jnp.dot(h.astype(BF16), wrt_ref[...], preferred_element_type=F32) + br_ref[...]


def _mix(x, yr, od, ga, gb, wr, wd, wo, gf, wrt, br, tm):
    t, d = x.shape
    row = lambda w: pl.BlockSpec((tm, w), lambda i: (i, 0))
    full = lambda a: pl.BlockSpec(a.shape, lambda i: (0, 0))
    return pl.pallas_call(
        _mix_kernel,
        grid=(t // tm,),
        in_specs=[row(d), row(yr.shape[1]), row(od.shape[1]), row(d), row(d),
                  full(wr), full(wd), full(wo), full(gf), full(wrt), full(br)],
        out_specs=[row(d), row(d), row(LANES)],
        out_shape=[jax.ShapeDtypeStruct((t, d), F32), jax.ShapeDtypeStruct((t, d), F32),
                   jax.ShapeDtypeStruct((t, LANES), F32)],
        compiler_params=_cparams(("arbitrary",)),
        name="mix",
    )(x, yr, od, ga, gb, wr, wd, wo, gf, wrt, br)


def _row_copy(src_hbm, row, buf_ref, slot, r, sem):
    return pltpu.make_async_copy(src_hbm.at[pl.ds(row, 1)], buf_ref.at[slot, pl.ds(r, 1)], sem.at[slot])


def _moe_kernel(be_ref, nu_ref, tok_ref, tokn_ref, h_hbm, wgu_ref, bgu_ref, wdn_ref, bdn_ref, out_ref,
                buf_ref, wgu_bf, wdn_bf, sem):
    i = pl.program_id(0)
    n_used = nu_ref[0]
    slot = i % 2
    d_ff = wdn_bf.shape[0]

    def issue(idx_ref, dst_slot):
        def body(r, _):
            _row_copy(h_hbm, idx_ref[0, 0, r], buf_ref, dst_slot, r, sem).start()
            return 0
        lax.fori_loop(0, MOE_ROWS, body, 0)

    @pl.when(i == 0)
    def _():
        issue(tok_ref, 0)

    @pl.when(i + 1 < n_used)
    def _():
        issue(tokn_ref, 1 - slot)

    @pl.when(i < n_used)
    def _():
        @pl.when((i == 0) | (be_ref[i] != be_ref[jnp.maximum(i - 1, 0)]))
        def _():
            wgu_bf[...] = wgu_ref[0].astype(BF16)
            wdn_bf[...] = wdn_ref[0].astype(BF16)

        def wait(r, _):
            _row_copy(h_hbm, 0, buf_ref, slot, r, sem).wait()
            return 0
        lax.fori_loop(0, MOE_ROWS, wait, 0)

        xb = buf_ref[slot].astype(BF16)
        gu = jnp.dot(xb, wgu_bf[...], preferred_element_type=F32) + bgu_ref[0]
        gate = jnp.minimum(gu[:, :d_ff], SWIGLU_LIMIT)
        up = jnp.clip(gu[:, d_ff:], -SWIGLU_LIMIT, SWIGLU_LIMIT)
        act = (up + 1.0) * gate * jax.nn.sigmoid(SWIGLU_ALPHA * gate)
        out_ref[...] = jnp.dot(act.astype(BF16), wdn_bf[...], preferred_element_type=F32) + bdn_ref[0]

    @pl.when(i >= n_used)
    def _():
        out_ref[...] = jnp.zeros_like(out_ref)


def _moe(h, tok, block_e, n_used, w_gu, b_gu, w_down, b_down):
    t, d = h.shape
    nb = tok.shape[0]
    ne, _, two_ff = w_gu.shape
    d_ff = two_ff // 2
    e_map = lambda i, be, nu: (be[i], 0, 0)
    grid_spec = pltpu.PrefetchScalarGridSpec(
        num_scalar_prefetch=2,
        grid=(nb,),
        in_specs=[
            pl.BlockSpec((1, 1, MOE_ROWS), lambda i, be, nu: (i, 0, 0), memory_space=pltpu.SMEM),
            pl.BlockSpec((1, 1, MOE_ROWS), lambda i, be, nu: (jnp.minimum(i + 1, nb - 1), 0, 0),
                         memory_space=pltpu.SMEM),
            pl.BlockSpec(memory_space=pl.ANY),
            pl.BlockSpec((1, d, two_ff), e_map),
            pl.BlockSpec((1, 1, two_ff), e_map),
            pl.BlockSpec((1, d_ff, d), e_map),
            pl.BlockSpec((1, 1, d), e_map),
        ],
        out_specs=pl.BlockSpec((MOE_ROWS, d), lambda i, be, nu: (i, 0)),
        scratch_shapes=[pltpu.VMEM((2, MOE_ROWS, d), F32),
                        pltpu.VMEM((d, two_ff), BF16),
                        pltpu.VMEM((d_ff, d), BF16),
                        pltpu.SemaphoreType.DMA((2,))],
    )
    return pl.pallas_call(
        _moe_kernel,
        grid_spec=grid_spec,
        out_shape=jax.ShapeDtypeStruct((nb * MOE_ROWS, d), F32),
        compiler_params=_cparams(("arbitrary",)),
        name="moe",
    )(block_e, n_used, tok, tok, h, w_gu, b_gu.reshape(ne, 1, two_ff), w_down, b_down.reshape(ne, 1, d))


def _combine_kernel(dst_ref, dstn_ref, x_ref, gate_ref, gf_ref, rows_hbm, y_ref, buf_ref, sem, *, tm, nsteps):
    i = pl.program_id(0)
    slot = i % 2
    n = tm * TOP_K

    def issue(idx_ref, dst_slot):
        def body(r, _):
            _row_copy(rows_hbm, idx_ref[0, 0, r], buf_ref, dst_slot, r, sem).start()
            return 0
        lax.fori_loop(0, n, body, 0)

    @pl.when(i == 0)
    def _():
        issue(dst_ref, 0)

    @pl.when(i + 1 < nsteps)
    def _():
        issue(dstn_ref, 1 - slot)

    def wait(r, _):
        _row_copy(rows_hbm, 0, buf_ref, slot, r, sem).wait()
        return 0
    lax.fori_loop(0, n, wait, 0)

    y = x_ref[...]
    gate = gate_ref[...]
    for k in range(TOP_K):
        y = y + gate[:, k:k + 1] * buf_ref[slot, k * tm:(k + 1) * tm, :]
    y_ref[...] = y * lax.rsqrt(jnp.mean(y * y, axis=-1, keepdims=True) + EPS) * gf_ref[...]


def _combine(x1, gates, dest, rows, g_final, tm):
    t, d = x1.shape
    nsteps = t // tm
    n = tm * TOP_K
    return pl.pallas_call(
        functools.partial(_combine_kernel, tm=tm, nsteps=nsteps),
        grid=(nsteps,),
        in_specs=[pl.BlockSpec((1, 1, n), lambda i: (i, 0, 0), memory_space=pltpu.SMEM),
                  pl.BlockSpec((1, 1, n), lambda i: (jnp.minimum(i + 1, nsteps - 1), 0, 0),
                               memory_space=pltpu.SMEM),
                  pl.BlockSpec((tm, d), lambda i: (i, 0)),
                  pl.BlockSpec((tm, TOP_K), lambda i: (i, 0)),
                  pl.BlockSpec((1, d), lambda i: (0, 0)),
                  pl.BlockSpec(memory_space=pl.ANY)],
        out_specs=pl.BlockSpec((tm, d), lambda i: (i, 0)),
        out_shape=jax.ShapeDtypeStruct((t, d), F32),
        scratch_shapes=[pltpu.VMEM((2, n, d), F32), pltpu.SemaphoreType.DMA((2,))],
        compiler_params=_cparams(("arbitrary",)),
        name="combine",
    )(dest, dest, x1, gates, g_final, rows)


def _route(logits):
    t = logits.shape[0]
    n_assign = t * TOP_K
    top_v, top_i = lax.top_k(logits, TOP_K)
    gates = jax.nn.softmax(top_v, axis=-1)
    flat_e = top_i.reshape(-1)
    onehot = (flat_e[:, None] == jnp.arange(N_EXPERTS, dtype=flat_e.dtype)[None, :]).astype(jnp.int32)
    csum = jnp.cumsum(onehot, axis=0)
    counts = csum[-1]
    rank = jnp.take_along_axis(csum, flat_e[:, None], axis=1)[:, 0] - 1
    padded = (counts + MOE_ROWS - 1) // MOE_ROWS * MOE_ROWS
    pend = jnp.cumsum(padded)
    pstart = pend - padded
    start = jnp.cumsum(counts) - counts
    dest = (pstart[flat_e] + rank).astype(jnp.int32)
    nb = n_assign // MOE_ROWS + N_EXPERTS
    n_used = (pend[-1] // MOE_ROWS).astype(jnp.int32).reshape(1)
    block_e = jnp.minimum(jnp.searchsorted(pend, jnp.arange(nb, dtype=jnp.int32) * MOE_ROWS, side='right'),
                          N_EXPERTS - 1).astype(jnp.int32)
    order = jnp.argsort(flat_e)
    r = jnp.arange(nb * MOE_ROWS, dtype=jnp.int32)
    e_r = jnp.repeat(block_e, MOE_ROWS)
    within = r - pstart[e_r]
    src = jnp.clip(start[e_r] + jnp.minimum(within, counts[e_r] - 1), 0, n_assign - 1)
    tok = (order[src] // TOP_K).astype(jnp.int32)
    return gates, dest, tok.reshape(nb, 1, MOE_ROWS), block_e, n_used


def _channel_mix(x1, h, logits, w_gu, b_gu, w_down, b_down, g_final, tm):
    t = x1.shape[0]
    gates, dest, tok, block_e, n_used = _route(logits[:, :N_EXPERTS])
    rows = _moe(h, tok, block_e, n_used, w_gu, b_gu, w_down, b_down)
    dest = dest.reshape(t // tm, tm, TOP_K).swapaxes(1, 2).reshape(t // tm, 1, TOP_K * tm)
    return _combine(x1, gates, dest, rows, g_final, tm)


def kernel(x_prompt, x_sample, cache_k, cache_v, state_ret, meta_tokens, g_mix, w_in,
           lam_q1, lam_k1, lam_q2, lam_k2, g_subln, w_ret_out, w_diff_out, w_out,
           g_ffn, w_router, b_router, w_gu, b_gu, w_down, b_down, g_final):
    b, s, d = x_prompt.shape
    db, ds, _ = x_sample.shape
    depth, _, pl_len = cache_k.shape[:3]
    assert depth == 1, "single-layer step only"
    nm = meta_tokens.shape[0]
    hw = DIFF_HEADS * HEAD_W

    lam_init = 0.8 - 0.6 * math.exp(-0.3 * 0)
    lam = (jnp.exp(jnp.sum(lam_q1[0] * lam_k1[0])) - jnp.exp(jnp.sum(lam_q2[0] * lam_k2[0]))
           + lam_init).reshape(1).astype(F32)
    coef = 1.0 - lam_init

    w_in_bf = w_in[0].astype(BF16)
    g_mix2 = g_mix[0].reshape(1, d)

    tm = 512 if (b * s) % 512 == 0 else s
    tab_p = _tables(N_META + jnp.arange(s, dtype=jnp.int32))
    rq, rk, rv, rg, dq, dk, dv, ga, gb = _proj(x_prompt.reshape(b * s, d), g_mix2, w_in_bf, tab_p, tm, s // tm)

    pos_small = jnp.concatenate([jnp.tile(N_META + pl_len + jnp.arange(ds, dtype=jnp.int32), db),
                                 jnp.arange(nm, dtype=jnp.int32)])
    x_small = jnp.concatenate([x_sample.reshape(db * ds, d), meta_tokens.astype(F32)], axis=0)
    small = _proj(x_small, g_mix2, w_in_bf, _tables(pos_small), x_small.shape[0], 1)
    ns = db * ds
    srq, srk, srv, srg, sdq, sdk, sdv, sga, sgb = [a[:ns] for a in small]
    mrq, mrk, mrv, mrg, _, mdk, mdv, _, _ = [a[ns:] for a in small]

    cb_small = 128

    def pad_rows(a, nb_, n_):
        a = a.reshape(nb_, n_, a.shape[-1])
        return jnp.pad(a, ((0, 0), (0, cb_small - n_), (0, 0)))

    zero_state = jnp.zeros((1, RET_HEADS, RET_DK, RET_DV), F32)
    _, s_meta = _retention(pad_rows(mrq, 1, nm), pad_rows(mrk, 1, nm), pad_rows(mrv, 1, nm),
                           pad_rows(mrg, 1, nm), zero_state, cb_small, nm)
    cb = 256
    r3 = lambda a: a.reshape(b, s, a.shape[-1])
    yr, ret_p = _retention(r3(rq), r3(rk), r3(rv), r3(rg), s_meta, cb, cb)
    syr, ret_s = _retention(pad_rows(srq, db, ds), pad_rows(srk, db, ds), pad_rows(srv, db, ds),
                            pad_rows(srg, db, ds), state_ret[0], cb_small, ds)
    syr = syr[:, :ds].reshape(ns, -1)

    g_sub = g_subln[0].reshape(1, HEAD_W)
    ext_rows = 128
    pad_ext = lambda a: jnp.pad(a, ((0, 0), (0, ext_rows - a.shape[1]), (0, 0)))
    od = _diff_attention(r3(dq), r3(dk), r3(dv), pad_ext(mdk[None]), pad_ext(mdv[None]), nm,
                         lam, g_sub, coef, True, 256, 256)
    sdk3, sdv3 = sdk.reshape(db, ds, hw), sdv.reshape(db, ds, hw)
    ke_s = jnp.concatenate([jnp.broadcast_to(mdk[None], (db, nm, hw)), sdk3], axis=1)
    ve_s = jnp.concatenate([jnp.broadcast_to(mdv[None], (db, nm, hw)), sdv3], axis=1)
    sod = _diff_attention(sdq.reshape(db, ds, hw), cache_k[0].reshape(db, pl_len, hw),
                          cache_v[0].reshape(db, pl_len, hw), pad_ext(ke_s), pad_ext(ve_s), nm + ds,
                          lam, g_sub, coef, False, ds, 256)

    wr, wd, wo = w_ret_out[0].astype(BF16), w_diff_out[0].astype(BF16), w_out[0].astype(BF16)
    gf = g_ffn[0].reshape(1, d)
    wrt = jnp.pad(w_router[0], ((0, 0), (0, LANES - N_EXPERTS))).astype(BF16)
    br = jnp.pad(b_router[0].astype(F32), (0, LANES - N_EXPERTS)).reshape(1, LANES)
    x1, h, lg = _mix(x_prompt.reshape(b * s, d), yr.reshape(b * s, -1), od.reshape(b * s, hw), ga, gb,
                     wr, wd, wo, gf, wrt, br, tm)
    sx1, sh, slg = _mix(x_sample.reshape(ns, d), syr, sod.reshape(ns, hw), sga, sgb,
                        wr, wd, wo, gf, wrt, br, ns)

    gfin = g_final.reshape(1, d)
    tc = 256
    y_p = _channel_mix(x1, h, lg, w_gu[0], b_gu[0], w_down[0], b_down[0], gfin, tc)
    y_s = _channel_mix(sx1, sh, slg, w_gu[0], b_gu[0], w_down[0], b_down[0], gfin, min(tc, ns))

    k_p = jnp.concatenate([jnp.broadcast_to(mdk[None], (b, nm, hw)), r3(dk)], axis=1)
    v_p = jnp.concatenate([jnp.broadcast_to(mdv[None], (b, nm, hw)), r3(dv)], axis=1)
    shp = lambda a: a.reshape(1, a.shape[0], a.shape[1], DIFF_HEADS, HEAD_W)
    return (y_p.reshape(b, s, d), y_s.reshape(db, ds, d), ret_p[None], shp(k_p), shp(v_p),
            ret_s[None], shp(sdk3), shp(sdv3))
```

```python
import functools
import math

import jax
import jax.numpy as jnp
from jax import lax
from jax.experimental import pallas as pl
from jax.experimental.pallas import tpu as pltpu

F32 = jnp.float32
BF16 = jnp.bfloat16

EPS = 1e-6
N_META = 16
CHUNK = 64
RET_HEADS = 4
RET_DK = 128
RET_DV = 256
RET_THETA = 10000.0
DIFF_HEADS = 8
DIFF_DH = 64
ROT_DIM = DIFF_DH // 4
ROPE_THETA = 500000.0
N_EXPERTS = 32
TOP_K = 4
SWIGLU_LIMIT = 7.0
SWIGLU_ALPHA = 1.702

LANES = 128
SUBLANES = 8
HEAD_W = 2 * DIFF_DH
MOE_ROWS = 256
VMEM_LIMIT = 56 * 1024 * 1024
NEG = -1e30


def _cparams(sem):
    return pltpu.CompilerParams(dimension_semantics=sem, vmem_limit_bytes=VMEM_LIMIT)


def _tile_rows_store(ref, val):
    n = val.shape[0]
    for c in range(SUBLANES):
        ref[pl.ds(c, n, stride=SUBLANES), :] = val[:, c * LANES:(c + 1) * LANES]


def _tile_rows_load(ref, slot, first, n):
    return jnp.concatenate([ref[slot, pl.ds(first * SUBLANES + c, n, stride=SUBLANES), :]
                            for c in range(SUBLANES)], axis=1)


def _ret_tables(pos):
    angle = RET_THETA ** (-jnp.linspace(0.0, 1.0, RET_DK // 2, dtype=F32))
    angle = jnp.repeat(angle, 2)
    ang = pos.astype(F32)[:, None] * angle[None, :]
    cos, sin = jnp.cos(ang), jnp.sin(ang)
    even = (jnp.arange(RET_DK) % 2 == 0)[None, :]
    return cos, jnp.where(even, -sin, 0.0), jnp.where(even, 0.0, sin)


def _rope_tables(pos):
    half = ROT_DIM // 2
    inv = ROPE_THETA ** (-jnp.arange(half, dtype=F32) * (2.0 / ROT_DIM))
    ang = pos.astype(F32)[:, None] * inv[None, :]
    cos, sin = jnp.cos(ang), jnp.sin(ang)
    jj = jnp.arange(HEAD_W) % DIFF_DH
    first = (jj < half)[None, :]
    second = ((jj >= half) & (jj < ROT_DIM))[None, :]
    cos_l = jnp.take(cos, jj % half, axis=1)
    sin_l = jnp.take(sin, jj % half, axis=1)
    c = jnp.where(first | second, cos_l, 1.0)
    sa = jnp.where(first, -sin_l, 0.0)
    sb = jnp.where(second, sin_l, 0.0)
    return c, sa, sb


def _tables(pos):
    return jnp.concatenate(_ret_tables(pos) + _rope_tables(pos), axis=1)


def _proj_kernel(x_ref, g_ref, w_ref, tab_ref, rq_ref, rk_ref, rv_ref, rg_ref,
                 dq_ref, dk_ref, dv_ref, ga_ref, gb_ref):
    x = x_ref[...]
    ms = jnp.mean(x * x, axis=-1, keepdims=True)
    h = (x * lax.rsqrt(ms + EPS) * g_ref[...]).astype(BF16)

    def mm(c0, width):
        return jnp.dot(h, w_ref[:, c0:c0 + width], preferred_element_type=F32)

    def rot(p, t0, near, far):
        c = tab_ref[:, t0:t0 + LANES]
        sa = tab_ref[:, t0 + LANES:t0 + 2 * LANES]
        sb = tab_ref[:, t0 + 2 * LANES:t0 + 3 * LANES]
        return p * c + pltpu.roll(p, LANES - near, 1) * sa + pltpu.roll(p, far, 1) * sb

    kw = RET_HEADS * RET_DK
    vw = RET_HEADS * RET_DV
    dw = DIFF_HEADS * HEAD_W
    c0 = 0
    p = mm(c0, kw)
    for hh in range(RET_HEADS):
        sl = slice(hh * LANES, (hh + 1) * LANES)
        rq_ref[:, sl] = rot(p[:, sl], 0, 1, 1).astype(BF16)
    c0 += kw
    p = mm(c0, kw)
    for hh in range(RET_HEADS):
        sl = slice(hh * LANES, (hh + 1) * LANES)
        rk_ref[:, sl] = (rot(p[:, sl], 0, 1, 1) * (RET_DK ** -0.5)).astype(BF16)
    c0 += kw
    rv_ref[...] = mm(c0, vw).astype(BF16)
    c0 += vw
    rg_ref[...] = mm(c0, vw).astype(BF16)
    c0 += vw
    half = ROT_DIM // 2
    p = mm(c0, dw)
    for hh in range(DIFF_HEADS):
        sl = slice(hh * LANES, (hh + 1) * LANES)
        dq_ref[:, sl] = (rot(p[:, sl], 3 * LANES, half, half) * (DIFF_DH ** -0.5)).astype(BF16)
    c0 += dw
    p = mm(c0, dw)
    for hh in range(DIFF_HEADS):
        sl = slice(hh * LANES, (hh + 1) * LANES)
        dk_ref[:, sl] = rot(p[:, sl], 3 * LANES, half, half)
    c0 += dw
    dv_ref[...] = mm(c0, dw)
    c0 += dw
    d = x.shape[1]
    ga_ref[...] = mm(c0, d).astype(BF16)
    c0 += d
    gb_ref[...] = mm(c0, d).astype(BF16)


def _proj(x, g, w_bf, tab, tm, tab_blocks):
    t, d = x.shape
    kw, vw, dw = RET_HEADS * RET_DK, RET_HEADS * RET_DV, DIFF_HEADS * HEAD_W
    widths = (kw, kw, vw, vw, dw, dw, dw, d, d)
    dtypes = (BF16, BF16, BF16, BF16, BF16, F32, F32, BF16, BF16)
    row = lambda i: (i, 0)
    const = lambda i: (0, 0)
    return pl.pallas_call(
        _proj_kernel,
        grid=(t // tm,),
        in_specs=[pl.BlockSpec((tm, d), row),
                  pl.BlockSpec((1, d), const),
                  pl.BlockSpec(w_bf.shape, const),
                  pl.BlockSpec((tm, tab.shape[1]), lambda i: (i % tab_blocks, 0))],
        out_specs=[pl.BlockSpec((tm, w), row) for w in widths],
        out_shape=[jax.ShapeDtypeStruct((t, w), dt) for w, dt in zip(widths, dtypes)],
        compiler_params=_cparams(("arbitrary",)),
        name="proj",
    )(x, g, w_bf, tab)


def _ret_log_gammas():
    return tuple(math.log(1.0 - 2.0 ** (-5.0 - hh)) for hh in range(RET_HEADS))


def _ret_kernel(q_ref, k_ref, v_ref, g_ref, s0_ref, y_ref, st_ref, *, cb, n_valid):
    @pl.when(pl.program_id(1) == 0)
    def _():
        st_ref[...] = s0_ref[...]

    row = lax.broadcasted_iota(jnp.int32, (cb, cb), 0)
    col = lax.broadcasted_iota(jnp.int32, (cb, cb), 1)
    dist = (row - col).astype(F32)
    idx = lax.broadcasted_iota(jnp.int32, (cb, 1), 0).astype(F32)
    for hh, lg in enumerate(_ret_log_gammas()):
        q = q_ref[0, :, hh * RET_DK:(hh + 1) * RET_DK]
        k = k_ref[0, :, hh * RET_DK:(hh + 1) * RET_DK]
        v = v_ref[0, :, hh * RET_DV:(hh + 1) * RET_DV]
        g = g_ref[0, :, hh * RET_DV:(hh + 1) * RET_DV].astype(F32)
        st = st_ref[0, hh]
        decay = jnp.where(dist >= 0, jnp.exp(lg * jnp.maximum(dist, 0.0)), 0.0)
        s = lax.dot_general(q, k, (((1,), (1,)), ((), ())), preferred_element_type=F32) * decay
        o = jnp.dot(s.astype(BF16), v, preferred_element_type=F32)
        o = o + jnp.dot(q, st.astype(BF16), preferred_element_type=F32) * jnp.exp(lg * (idx + 1.0))
        kw = (k.astype(F32) * jnp.exp(lg * (n_valid - 1.0 - idx))).astype(BF16)
        st_ref[0, hh] = math.exp(lg * n_valid) * st + lax.dot_general(
            kw, v, (((0,), (0,)), ((), ())), preferred_element_type=F32)
        on = o * lax.rsqrt(jnp.mean(o * o, axis=-1, keepdims=True) + EPS)
        y_ref[0, :, hh * RET_DV:(hh + 1) * RET_DV] = (g * jax.nn.sigmoid(g) * on).astype(BF16)


def _retention(q, k, v, g, s0, cb, n_valid):
    b, s, _ = q.shape
    s0_map = (lambda i, c: (i, 0, 0, 0)) if s0.shape[0] == b else (lambda i, c: (0, 0, 0, 0))
    blk = lambda w: pl.BlockSpec((1, cb, w), lambda i, c: (i, c, 0))
    st_block = (1, RET_HEADS, RET_DK, RET_DV)
    return pl.pallas_call(
        functools.partial(_ret_kernel, cb=cb, n_valid=n_valid),
        grid=(b, s // cb),
        in_specs=[blk(q.shape[2]), blk(k.shape[2]), blk(v.shape[2]), blk(g.shape[2]),
                  pl.BlockSpec(st_block, s0_map)],
        out_specs=[blk(v.shape[2]), pl.BlockSpec(st_block, lambda i, c: (i, 0, 0, 0))],
        out_shape=[jax.ShapeDtypeStruct(v.shape, BF16),
                   jax.ShapeDtypeStruct((b,) + st_block[1:], F32)],
        compiler_params=_cparams(("arbitrary", "arbitrary")),
        name="retention",
    )(q, k, v, g, s0)


def _scores(qq, kk):
    return lax.dot_general(qq, kk, (((1,), (1,)), ((), ())), preferred_element_type=F32)


def _softmax_seed(scores, values):
    ms = [jnp.max(s, axis=-1, keepdims=True) for s in scores]
    ps = [jnp.exp(s - m) for s, m in zip(scores, ms)]
    out = []
    for m, p, vv in zip(ms, ps, values):
        out += [m, jnp.sum(p, axis=-1, keepdims=True),
                jnp.dot(p.astype(BF16), vv, preferred_element_type=F32)]
    return out


def _softmax_step(carry, scores, values):
    m2s = [jnp.maximum(carry[3 * c], jnp.max(s, axis=-1, keepdims=True)) for c, s in enumerate(scores)]
    ps = [jnp.exp(s - m2) for s, m2 in zip(scores, m2s)]
    out = []
    for c, (m2, p, vv) in enumerate(zip(m2s, ps, values)):
        m, l, acc = carry[3 * c:3 * c + 3]
        a = jnp.exp(m - m2)
        out += [m2, a * l + jnp.sum(p, axis=-1, keepdims=True),
                a * acc + jnp.dot(p.astype(BF16), vv, preferred_element_type=F32)]
    return out


def _sub_norm(o1, o2, lam, gs, coef):
    o = o1 - lam * o2
    return o * lax.rsqrt(jnp.mean(o * o, axis=-1, keepdims=True) + EPS) * gs * coef


def _attn_prompt_kernel(lam_ref, q_ref, km_ref, vm_ref, ke_ref, ve_ref, gs_ref, o_ref, kb_ref, vb_ref,
                        *, sq, tq, hp, n_ext, coef):
    kb_ref[...] = km_ref[0].astype(BF16)
    vb_ref[...] = vm_ref[0].astype(BF16)
    lam = lam_ref[0]
    gs = gs_ref[...]
    lo = lax.broadcasted_iota(jnp.int32, (1, HEAD_W), 1) < DIFF_DH
    ext_ok = lax.broadcasted_iota(jnp.int32, (1, ke_ref.shape[1]), 1) < n_ext
    diag = (lax.broadcasted_iota(jnp.int32, (tq, tq), 1) // CHUNK
            <= lax.broadcasted_iota(jnp.int32, (tq, tq), 0) // CHUNK)
    heads = [slice(hh * HEAD_W, (hh + 1) * HEAD_W) for hh in range(hp)]

    def q_tile(qi, _):
        q0 = pl.multiple_of(qi * tq, tq)
        qs = []
        for sl in heads:
            q = q_ref[0, pl.ds(q0, tq), sl]
            zero = jnp.zeros_like(q)
            qs += [jnp.where(lo, q, zero), jnp.where(lo, zero, q)]

        ext_s = [jnp.where(ext_ok, _scores(qq, ke_ref[0, :, heads[c // 2]].astype(BF16)), NEG)
                 for c, qq in enumerate(qs)]
        carry = _softmax_seed(ext_s, [ve_ref[0, :, heads[c // 2]].astype(BF16) for c in range(len(qs))])

        def step(j, carry, mask=None):
            k0 = pl.multiple_of(j * tq, tq)
            ss = [_scores(qq, kb_ref[pl.ds(k0, tq), heads[c // 2]]) for c, qq in enumerate(qs)]
            if mask is not None:
                ss = [jnp.where(mask, s, NEG) for s in ss]
            return tuple(_softmax_step(carry, ss, [vb_ref[pl.ds(k0, tq), heads[c // 2]] for c in range(len(qs))]))

        carry = lax.fori_loop(0, qi, step, tuple(carry))
        carry = step(qi, carry, diag)
        for hh, sl in enumerate(heads):
            _, l1, a1, _, l2, a2 = carry[6 * hh:6 * hh + 6]
            o_ref[0, pl.ds(q0, tq), sl] = _sub_norm(a1 / l1, a2 / l2, lam, gs, coef).astype(BF16)
        return 0

    lax.fori_loop(0, sq // tq, q_tile, 0)


def _attention_prompt(q, k_main, v_main, k_ext, v_ext, n_ext, lam, g_subln, coef, tq, hp):
    b, sq, hw = q.shape
    w = hp * HEAD_W
    bh = pl.BlockSpec((1, sq, w), lambda i, h: (i, 0, h))
    ext = pl.BlockSpec((1, k_ext.shape[1], w), lambda i, h: (0, 0, h))
    return pl.pallas_call(
        functools.partial(_attn_prompt_kernel, sq=sq, tq=tq, hp=hp, n_ext=n_ext, coef=coef),
        grid=(b, hw // w),
        in_specs=[pl.BlockSpec(memory_space=pltpu.SMEM), bh, bh, bh, ext, ext,
                  pl.BlockSpec((1, HEAD_W), lambda i, h: (0, 0))],
        out_specs=bh,
        out_shape=jax.ShapeDtypeStruct(q.shape, BF16),
        scratch_shapes=[pltpu.VMEM((sq, w), BF16), pltpu.VMEM((sq, w), BF16)],
        compiler_params=_cparams(("arbitrary", "arbitrary")),
        name="attention_prompt",
    )(lam, q, k_main, v_main, k_ext, v_ext, g_subln)


def _attn_sample_kernel(lam_ref, q_ref, kc_ref, vc_ref, ke_ref, ve_ref, gs_ref, o_ref, m_ref, l_ref, acc_ref,
                        *, ds, nh, n_ext, coef):
    t = pl.program_id(1)
    tkv = kc_ref.shape[1] // nh
    lo = lax.broadcasted_iota(jnp.int32, (1, HEAD_W), 1) < DIFF_DH
    ext_ok = lax.broadcasted_iota(jnp.int32, (1, ke_ref.shape[1]), 1) < n_ext
    heads = [slice(hh * HEAD_W, (hh + 1) * HEAD_W) for hh in range(nh)]

    def stacked_q(sl):
        q = q_ref[0, :, sl]
        zero = jnp.zeros_like(q)
        return jnp.concatenate([jnp.where(lo, q, zero), jnp.where(lo, zero, q)], axis=0)

    qs = [stacked_q(sl) for sl in heads]

    def write_stats(stats):
        for hh in range(nh):
            m_ref[hh], l_ref[hh], acc_ref[hh] = stats[3 * hh:3 * hh + 3]

    @pl.when(t == 0)
    def _():
        ext_s = [jnp.where(ext_ok, _scores(qq, ke_ref[0, :, sl].astype(BF16)), NEG) for qq, sl in zip(qs, heads)]
        write_stats(_softmax_seed(ext_s, [ve_ref[0, :, sl].astype(BF16) for sl in heads]))

    carry = []
    for hh in range(nh):
        carry += [m_ref[hh], l_ref[hh], acc_ref[hh]]
    ss = [_scores(qq, kc_ref[0, pl.ds(hh, tkv, stride=nh), :].astype(BF16)) for hh, qq in enumerate(qs)]
    write_stats(_softmax_step(carry, ss, [vc_ref[0, pl.ds(hh, tkv, stride=nh), :].astype(BF16) for hh in range(nh)]))

    @pl.when(t == pl.num_programs(1) - 1)
    def _():
        lam = lam_ref[0]
        for hh, sl in enumerate(heads):
            o = acc_ref[hh] / l_ref[hh]
            o_ref[0, :, sl] = _sub_norm(o[:ds], o[ds:], lam, gs_ref[...], coef).astype(BF16)


def _attention_sample(q, k_cache, v_cache, k_ext, v_ext, n_ext, lam, g_subln, coef, tkv):
    b, ds, hw = q.shape
    _, pl_len, nh, _ = k_cache.shape
    per_b = lambda rows: pl.BlockSpec((1, rows, hw), lambda i, t: (i, 0, 0))
    cache = pl.BlockSpec((1, tkv * nh, HEAD_W), lambda i, t: (i, t, 0))
    return pl.pallas_call(
        functools.partial(_attn_sample_kernel, ds=ds, nh=nh, n_ext=n_ext, coef=coef),
        grid=(b, pl_len // tkv),
        in_specs=[pl.BlockSpec(memory_space=pltpu.SMEM), per_b(ds), cache, cache,
                  per_b(k_ext.shape[1]), per_b(k_ext.shape[1]),
                  pl.BlockSpec((1, HEAD_W), lambda i, t: (0, 0))],
        out_specs=per_b(ds),
        out_shape=jax.ShapeDtypeStruct(q.shape, BF16),
        scratch_shapes=[pltpu.VMEM((nh, 2 * ds, 1), F32), pltpu.VMEM((nh, 2 * ds, 1), F32),
                        pltpu.VMEM((nh, 2 * ds, HEAD_W), F32)],
        compiler_params=_cparams(("arbitrary", "arbitrary")),
        name="attention_sample",
    )(lam, q, k_cache.reshape(b, pl_len * nh, HEAD_W), v_cache.reshape(b, pl_len * nh, HEAD_W),
      k_ext, v_ext, g_subln)


def _mix_kernel(x_ref, yr_ref, od_ref, ga_ref, gb_ref, wr_ref, wd_ref, wo_ref, gf_ref, wrt_ref, br_ref,
                x1_ref, h_ref, lg_ref):
    ya = jnp.dot(yr_ref[...], wr_ref[...], preferred_element_type=F32)
    yb = jnp.dot(od_ref[...], wd_ref[...], preferred_element_type=F32)
    z = jax.nn.sigmoid(ga_ref[...].astype(F32)) * ya + jax.nn.sigmoid(gb_ref[...].astype(F32)) * yb
    x1 = x_ref[...] + jnp.dot(z.astype(BF16), wo_ref[...], preferred_element_type=F32)
    x1_ref[...] = x1
    h = x1 * lax.rsqrt(jnp.mean(x1 * x1, axis=-1, keepdims=True) + EPS) * gf_ref[...]
    _tile_rows_store(h_ref, h)
    lg_ref[...] = jnp.dot(h.astype(BF16), wrt_ref[...], preferred_element_type=F32) + br_ref[...]


def _mix(x, yr, od, ga, gb, wr, wd, wo, gf, wrt, br, tm):
    t, d = x.shape
    row = lambda w: pl.BlockSpec((tm, w), lambda i: (i, 0))
    full = lambda a: pl.BlockSpec(a.shape, lambda i: (0, 0))
    return pl.pallas_call(
        _mix_kernel,
        grid=(t // tm,),
        in_specs=[row(d), row(yr.shape[1]), row(od.shape[1]), row(d), row(d),
                  full(wr), full(wd), full(wo), full(gf), full(wrt), full(br)],
        out_specs=[row(d), pl.BlockSpec((tm * SUBLANES, LANES), lambda i: (i, 0)), row(LANES)],
        out_shape=[jax.ShapeDtypeStruct((t, d), F32), jax.ShapeDtypeStruct((t * SUBLANES, LANES), F32),
                   jax.ShapeDtypeStruct((t, LANES), F32)],
        compiler_params=_cparams(("arbitrary",)),
        name="mix",
    )(x, yr, od, ga, gb, wr, wd, wo, gf, wrt, br)


GATHER_UNROLL = 8


def _gather_rows(src_hbm, idx_ref, n, buf_ref, slot, sem):
    def body(g, _):
        for u in range(GATHER_UNROLL):
            r = g * GATHER_UNROLL + u
            src = pl.multiple_of(idx_ref[0, 0, r], SUBLANES)
            dst = pl.multiple_of(r * SUBLANES, SUBLANES)
            pltpu.make_async_copy(src_hbm.at[pl.ds(src, SUBLANES)], buf_ref.at[slot, pl.ds(dst, SUBLANES)],
                                  sem.at[slot]).start(priority=u % 2)
        return 0
    lax.fori_loop(0, n // GATHER_UNROLL, body, 0)


def _wait_rows(src_hbm, n, buf_ref, slot, sem):
    pltpu.make_async_copy(src_hbm.at[pl.ds(0, n * SUBLANES)], buf_ref.at[slot], sem.at[slot]).wait()


def _moe_kernel(be_ref, nu_ref, tok_ref, tokn_ref, h_hbm, wgu_ref, bgu_ref, wdn_ref, bdn_ref, out_ref,
                buf_ref, wgu_bf, wdn_bf, sem):
    i = pl.program_id(0)
    n_used = nu_ref[0]
    slot = i % 2
    d_ff = wdn_bf.shape[0]

    @pl.when(i == 0)
    def _():
        _gather_rows(h_hbm, tok_ref, MOE_ROWS, buf_ref, 0, sem)

    @pl.when(i + 1 < n_used)
    def _():
        _gather_rows(h_hbm, tokn_ref, MOE_ROWS, buf_ref, 1 - slot, sem)

    @pl.when(i < n_used)
    def _():
        @pl.when((i == 0) | (be_ref[i] != be_ref[jnp.maximum(i - 1, 0)]))
        def _():
            wgu_bf[...] = wgu_ref[0].astype(BF16)
            wdn_bf[...] = wdn_ref[0].astype(BF16)

        _wait_rows(h_hbm, MOE_ROWS, buf_ref, slot, sem)

        xb = _tile_rows_load(buf_ref, slot, 0, MOE_ROWS).astype(BF16)
        gu = jnp.dot(xb, wgu_bf[...], preferred_element_type=F32) + bgu_ref[0]
        gate = jnp.minimum(gu[:, :d_ff], SWIGLU_LIMIT)
        up = jnp.clip(gu[:, d_ff:], -SWIGLU_LIMIT, SWIGLU_LIMIT)
        act = (up + 1.0) * gate * jax.nn.sigmoid(SWIGLU_ALPHA * gate)
        _tile_rows_store(out_ref, jnp.dot(act.astype(BF16), wdn_bf[...], preferred_element_type=F32) + bdn_ref[0])

    @pl.when(i >= n_used)
    def _():
        out_ref[...] = jnp.zeros_like(out_ref)


def _moe(h, tok, block_e, n_used, w_gu, b_gu, w_down, b_down):
    d = h.shape[1] * SUBLANES
    nb = tok.shape[0]
    ne, _, two_ff = w_gu.shape
    d_ff = two_ff // 2
    e_map = lambda i, be, nu: (be[i], 0, 0)
    grid_spec = pltpu.PrefetchScalarGridSpec(
        num_scalar_prefetch=2,
        grid=(nb,),
        in_specs=[
            pl.BlockSpec((1, 1, MOE_ROWS), lambda i, be, nu: (i, 0, 0), memory_space=pltpu.SMEM),
            pl.BlockSpec((1, 1, MOE_ROWS), lambda i, be, nu: (jnp.minimum(i + 1, nb - 1), 0, 0),
                         memory_space=pltpu.SMEM),
            pl.BlockSpec(memory_space=pl.ANY),
            pl.BlockSpec((1, d, two_ff), e_map),
            pl.BlockSpec((1, 1, two_ff), e_map),
            pl.BlockSpec((1, d_ff, d), e_map),
            pl.BlockSpec((1, 1, d), e_map),
        ],
        out_specs=pl.BlockSpec((MOE_ROWS * SUBLANES, LANES), lambda i, be, nu: (i, 0)),
        scratch_shapes=[pltpu.VMEM((2, MOE_ROWS * SUBLANES, LANES), F32),
                        pltpu.VMEM((d, two_ff), BF16),
                        pltpu.VMEM((d_ff, d), BF16),
                        pltpu.SemaphoreType.DMA((2,))],
    )
    return pl.pallas_call(
        _moe_kernel,
        grid_spec=grid_spec,
        out_shape=jax.ShapeDtypeStruct((nb * MOE_ROWS * SUBLANES, LANES), F32),
        compiler_params=_cparams(("arbitrary",)),
        name="moe",
    )(block_e, n_used, tok, tok, h, w_gu, b_gu.reshape(ne, 1, two_ff), w_down, b_down.reshape(ne, 1, d))


def _combine_kernel(dst_ref, dstn_ref, x_ref, gate_ref, gf_ref, rows_hbm, y_ref, buf_ref, sem, *, tm, nsteps):
    i = pl.program_id(0)
    slot = i % 2
    n = tm * TOP_K

    @pl.when(i == 0)
    def _():
        _gather_rows(rows_hbm, dst_ref, n, buf_ref, 0, sem)

    @pl.when(i + 1 < nsteps)
    def _():
        _gather_rows(rows_hbm, dstn_ref, n, buf_ref, 1 - slot, sem)

    _wait_rows(rows_hbm, n, buf_ref, slot, sem)

    y = x_ref[...]
    gate = gate_ref[...]
    for k in range(TOP_K):
        y = y + gate[:, k:k + 1] * _tile_rows_load(buf_ref, slot, k * tm, tm)
    y_ref[...] = y * lax.rsqrt(jnp.mean(y * y, axis=-1, keepdims=True) + EPS) * gf_ref[...]


def _combine(x1, gates, dest, rows, g_final, tm):
    t, d = x1.shape
    nsteps = t // tm
    n = tm * TOP_K
    return pl.pallas_call(
        functools.partial(_combine_kernel, tm=tm, nsteps=nsteps),
        grid=(nsteps,),
        in_specs=[pl.BlockSpec((1, 1, n), lambda i: (i, 0, 0), memory_space=pltpu.SMEM),
                  pl.BlockSpec((1, 1, n), lambda i: (jnp.minimum(i + 1, nsteps - 1), 0, 0),
                               memory_space=pltpu.SMEM),
                  pl.BlockSpec((tm, d), lambda i: (i, 0)),
                  pl.BlockSpec((tm, TOP_K), lambda i: (i, 0)),
                  pl.BlockSpec((1, d), lambda i: (0, 0)),
                  pl.BlockSpec(memory_space=pl.ANY)],
        out_specs=pl.BlockSpec((tm, d), lambda i: (i, 0)),
        out_shape=jax.ShapeDtypeStruct((t, d), F32),
        scratch_shapes=[pltpu.VMEM((2, n * SUBLANES, LANES), F32), pltpu.SemaphoreType.DMA((2,))],
        compiler_params=_cparams(("arbitrary",)),
        name="combine",
    )(dest, dest, x1, gates, g_final, rows)


def _route(logits):
    t = logits.shape[0]
    n_assign = t * TOP_K
    top_v, top_i = lax.top_k(logits, TOP_K)
    gates = jax.nn.softmax(top_v, axis=-1)
    flat_e = top_i.reshape(-1)
    onehot = (flat_e[:, None] == jnp.arange(N_EXPERTS, dtype=flat_e.dtype)[None, :]).astype(jnp.int32)
    csum = jnp.cumsum(onehot, axis=0)
    counts = csum[-1]
    rank = jnp.take_along_axis(csum, flat_e[:, None], axis=1)[:, 0] - 1
    padded = (counts + MOE_ROWS - 1) // MOE_ROWS * MOE_ROWS
    pend = jnp.cumsum(padded)
    pstart = pend - padded
    start = jnp.cumsum(counts) - counts
    dest = (pstart[flat_e] + rank).astype(jnp.int32)
    nb = n_assign // MOE_ROWS + N_EXPERTS
    n_used = (pend[-1] // MOE_ROWS).astype(jnp.int32).reshape(1)
    block_e = jnp.minimum(jnp.searchsorted(pend, jnp.arange(nb, dtype=jnp.int32) * MOE_ROWS, side='right'),
                          N_EXPERTS - 1).astype(jnp.int32)
    order = jnp.argsort(flat_e)
    r = jnp.arange(nb * MOE_ROWS, dtype=jnp.int32)
    e_r = jnp.repeat(block_e, MOE_ROWS)
    within = r - pstart[e_r]
    src = jnp.clip(start[e_r] + jnp.minimum(within, counts[e_r] - 1), 0, n_assign - 1)
    tok = (order[src] // TOP_K).astype(jnp.int32)
    return gates, dest * SUBLANES, (tok * SUBLANES).reshape(nb, 1, MOE_ROWS), block_e, n_used


def _channel_mix(x1, h, logits, w_gu, b_gu, w_down, b_down, g_final, tm):
    t = x1.shape[0]
    gates, dest, tok, block_e, n_used = _route(logits[:, :N_EXPERTS])
    rows = _moe(h, tok, block_e, n_used, w_gu, b_gu, w_down, b_down)
    dest = dest.reshape(t // tm, tm, TOP_K).swapaxes(1, 2).reshape(t // tm, 1, TOP_K * tm)
    return _combine(x1, gates, dest, rows, g_final, tm)


def kernel(x_prompt, x_sample, cache_k, cache_v, state_ret, meta_tokens, g_mix, w_in,
           lam_q1, lam_k1, lam_q2, lam_k2, g_subln, w_ret_out, w_diff_out, w_out,
           g_ffn, w_router, b_router, w_gu, b_gu, w_down, b_down, g_final):
    b, s, d = x_prompt.shape
    db, ds, _ = x_sample.shape
    depth, _, pl_len = cache_k.shape[:3]
    assert depth == 1, "single-layer step only"
    nm = meta_tokens.shape[0]
    hw = DIFF_HEADS * HEAD_W

    lam_init = 0.8 - 0.6 * math.exp(-0.3 * 0)
    lam = (jnp.exp(jnp.sum(lam_q1[0] * lam_k1[0])) - jnp.exp(jnp.sum(lam_q2[0] * lam_k2[0]))
           + lam_init).reshape(1).astype(F32)
    coef = 1.0 - lam_init

    w_in_bf = w_in[0].astype(BF16)
    g_mix2 = g_mix[0].reshape(1, d)

    tm = 512 if (b * s) % 512 == 0 else s
    tab_p = _tables(N_META + jnp.arange(s, dtype=jnp.int32))
    rq, rk, rv, rg, dq, dk, dv, ga, gb = _proj(x_prompt.reshape(b * s, d), g_mix2, w_in_bf, tab_p, tm, s // tm)

    pos_small = jnp.concatenate([jnp.tile(N_META + pl_len + jnp.arange(ds, dtype=jnp.int32), db),
                                 jnp.arange(nm, dtype=jnp.int32)])
    x_small = jnp.concatenate([x_sample.reshape(db * ds, d), meta_tokens.astype(F32)], axis=0)
    small = _proj(x_small, g_mix2, w_in_bf, _tables(pos_small), x_small.shape[0], 1)
    ns = db * ds
    srq, srk, srv, srg, sdq, sdk, sdv, sga, sgb = [a[:ns] for a in small]
    mrq, mrk, mrv, mrg, _, mdk, mdv, _, _ = [a[ns:] for a in small]

    cb_small = 128

    def pad_rows(a, nb_, n_):
        a = a.reshape(nb_, n_, a.shape[-1])
        return jnp.pad(a, ((0, 0), (0, cb_small - n_), (0, 0)))

    zero_state = jnp.zeros((1, RET_HEADS, RET_DK, RET_DV), F32)
    _, s_meta = _retention(pad_rows(mrq, 1, nm), pad_rows(mrk, 1, nm), pad_rows(mrv, 1, nm),
                           pad_rows(mrg, 1, nm), zero_state, cb_small, nm)
    cb = 256
    r3 = lambda a: a.reshape(b, s, a.shape[-1])
    yr, ret_p = _retention(r3(rq), r3(rk), r3(rv), r3(rg), s_meta, cb, cb)
    syr, ret_s = _retention(pad_rows(srq, db, ds), pad_rows(srk, db, ds), pad_rows(srv, db, ds),
                            pad_rows(srg, db, ds), state_ret[0], cb_small, ds)
    syr = syr[:, :ds].reshape(ns, -1)

    g_sub = g_subln[0].reshape(1, HEAD_W)
    ext_rows = 128
    pad_ext = lambda a: jnp.pad(a, ((0, 0), (0, ext_rows - a.shape[1]), (0, 0)))
    od = _attention_prompt(r3(dq), r3(dk), r3(dv), pad_ext(mdk[None]), pad_ext(mdv[None]), nm,
                           lam, g_sub, coef, 256, 4)
    sdk3, sdv3 = sdk.reshape(db, ds, hw), sdv.reshape(db, ds, hw)
    ke_s = jnp.concatenate([jnp.broadcast_to(mdk[None], (db, nm, hw)), sdk3], axis=1)
    ve_s = jnp.concatenate([jnp.broadcast_to(mdv[None], (db, nm, hw)), sdv3], axis=1)
    sod = _attention_sample(sdq.reshape(db, ds, hw), cache_k[0], cache_v[0], pad_ext(ke_s), pad_ext(ve_s),
                            nm + ds, lam, g_sub, coef, 512)

    wr, wd, wo = w_ret_out[0].astype(BF16), w_diff_out[0].astype(BF16), w_out[0].astype(BF16)
    gf = g_ffn[0].reshape(1, d)
    wrt = jnp.pad(w_router[0], ((0, 0), (0, LANES - N_EXPERTS))).astype(BF16)
    br = jnp.pad(b_router[0].astype(F32), (0, LANES - N_EXPERTS)).reshape(1, LANES)
    x1, h, lg = _mix(x_prompt.reshape(b * s, d), yr.reshape(b * s, -1), od.reshape(b * s, hw), ga, gb,
                     wr, wd, wo, gf, wrt, br, tm)
    sx1, sh, slg = _mix(x_sample.reshape(ns, d), syr, sod.reshape(ns, hw), sga, sgb,
                        wr, wd, wo, gf, wrt, br, ns)

    gfin = g_final.reshape(1, d)
    tc = 256
    y_p = _channel_mix(x1, h, lg, w_gu[0], b_gu[0], w_down[0], b_down[0], gfin, tc)
    y_s = _channel_mix(sx1, sh, slg, w_gu[0], b_gu[0], w_down[0], b_down[0], gfin, min(tc, ns))

    k_p = jnp.concatenate([jnp.broadcast_to(mdk[None], (b, nm, hw)), r3(dk)], axis=1)
    v_p = jnp.concatenate([jnp.broadcast_to(mdv[None], (b, nm, hw)), r3(dv)], axis=1)
    shp = lambda a: a.reshape(1, a.shape[0], a.shape[1], DIFF_HEADS, HEAD_W)
    return (y_p.reshape(b, s, d), y_s.reshape(db, ds, d), ret_p[None], shp(k_p), shp(v_p),
            ret_s[None], shp(sdk3), shp(sdv3))
```

```python
import functools
import math

import jax
import jax.numpy as jnp
from jax import lax
from jax.experimental import pallas as pl
from jax.experimental.pallas import tpu as pltpu

F32 = jnp.float32
BF16 = jnp.bfloat16

EPS = 1e-6
N_META = 16
CHUNK = 64
RET_HEADS = 4
RET_DK = 128
RET_DV = 256
RET_THETA = 10000.0
DIFF_HEADS = 8
DIFF_DH = 64
ROT_DIM = DIFF_DH // 4
ROPE_THETA = 500000.0
N_EXPERTS = 32
TOP_K = 4
SWIGLU_LIMIT = 7.0
SWIGLU_ALPHA = 1.702

LANES = 128
SUBLANES = 8
HEAD_W = 2 * DIFF_DH
MOE_ROWS = 256
VMEM_LIMIT = 56 * 1024 * 1024
NEG = -1e30
Q_SCALE = DIFF_DH ** -0.5 * math.log2(math.e)


def _cparams(sem):
    return pltpu.CompilerParams(dimension_semantics=sem, vmem_limit_bytes=VMEM_LIMIT)


def _tile_rows_store(ref, val):
    n = val.shape[0]
    for c in range(SUBLANES):
        ref[pl.ds(c, n, stride=SUBLANES), :] = val[:, c * LANES:(c + 1) * LANES]


def _tile_rows_load(ref, slot, first, n):
    return jnp.concatenate([ref[slot, pl.ds(first * SUBLANES + c, n, stride=SUBLANES), :]
                            for c in range(SUBLANES)], axis=1)


def _ret_tables(pos):
    angle = RET_THETA ** (-jnp.linspace(0.0, 1.0, RET_DK // 2, dtype=F32))
    angle = jnp.repeat(angle, 2)
    ang = pos.astype(F32)[:, None] * angle[None, :]
    cos, sin = jnp.cos(ang), jnp.sin(ang)
    even = (jnp.arange(RET_DK) % 2 == 0)[None, :]
    return cos, jnp.where(even, -sin, 0.0), jnp.where(even, 0.0, sin)


def _rope_tables(pos):
    half = ROT_DIM // 2
    inv = ROPE_THETA ** (-jnp.arange(half, dtype=F32) * (2.0 / ROT_DIM))
    ang = pos.astype(F32)[:, None] * inv[None, :]
    cos, sin = jnp.cos(ang), jnp.sin(ang)
    jj = jnp.arange(HEAD_W) % DIFF_DH
    first = (jj < half)[None, :]
    second = ((jj >= half) & (jj < ROT_DIM))[None, :]
    cos_l = jnp.take(cos, jj % half, axis=1)
    sin_l = jnp.take(sin, jj % half, axis=1)
    c = jnp.where(first | second, cos_l, 1.0)
    sa = jnp.where(first, -sin_l, 0.0)
    sb = jnp.where(second, sin_l, 0.0)
    return c, sa, sb


def _tables(pos):
    return jnp.concatenate(_ret_tables(pos) + _rope_tables(pos), axis=1)


def _proj_kernel(x_ref, g_ref, w_ref, tab_ref, rq_ref, rk_ref, rv_ref, rg_ref,
                 dq_ref, dk_ref, dv_ref, ga_ref, gb_ref):
    x = x_ref[...]
    ms = jnp.mean(x * x, axis=-1, keepdims=True)
    h = (x * lax.rsqrt(ms + EPS) * g_ref[...]).astype(BF16)

    def mm(c0, width):
        return jnp.dot(h, w_ref[:, c0:c0 + width], preferred_element_type=F32)

    def rot(p, t0, near, far):
        c = tab_ref[:, t0:t0 + LANES]
        sa = tab_ref[:, t0 + LANES:t0 + 2 * LANES]
        sb = tab_ref[:, t0 + 2 * LANES:t0 + 3 * LANES]
        return p * c + pltpu.roll(p, LANES - near, 1) * sa + pltpu.roll(p, far, 1) * sb

    kw = RET_HEADS * RET_DK
    vw = RET_HEADS * RET_DV
    dw = DIFF_HEADS * HEAD_W
    c0 = 0
    p = mm(c0, kw)
    for hh in range(RET_HEADS):
        sl = slice(hh * LANES, (hh + 1) * LANES)
        rq_ref[:, sl] = rot(p[:, sl], 0, 1, 1).astype(BF16)
    c0 += kw
    p = mm(c0, kw)
    for hh in range(RET_HEADS):
        sl = slice(hh * LANES, (hh + 1) * LANES)
        rk_ref[:, sl] = (rot(p[:, sl], 0, 1, 1) * (RET_DK ** -0.5)).astype(BF16)
    c0 += kw
    rv_ref[...] = mm(c0, vw).astype(BF16)
    c0 += vw
    rg_ref[...] = mm(c0, vw).astype(BF16)
    c0 += vw
    half = ROT_DIM // 2
    p = mm(c0, dw)
    for hh in range(DIFF_HEADS):
        sl = slice(hh * LANES, (hh + 1) * LANES)
        dq_ref[:, sl] = (rot(p[:, sl], 3 * LANES, half, half) * Q_SCALE).astype(BF16)
    c0 += dw
    p = mm(c0, dw)
    for hh in range(DIFF_HEADS):
        sl = slice(hh * LANES, (hh + 1) * LANES)
        dk_ref[:, sl] = rot(p[:, sl], 3 * LANES, half, half)
    c0 += dw
    dv_ref[...] = mm(c0, dw)
    c0 += dw
    d = x.shape[1]
    ga_ref[...] = mm(c0, d).astype(BF16)
    c0 += d
    gb_ref[...] = mm(c0, d).astype(BF16)


def _proj(x, g, w_bf, tab, tm, tab_blocks):
    t, d = x.shape
    kw, vw, dw = RET_HEADS * RET_DK, RET_HEADS * RET_DV, DIFF_HEADS * HEAD_W
    widths = (kw, kw, vw, vw, dw, dw, dw, d, d)
    dtypes = (BF16, BF16, BF16, BF16, BF16, F32, F32, BF16, BF16)
    row = lambda i: (i, 0)
    const = lambda i: (0, 0)
    return pl.pallas_call(
        _proj_kernel,
        grid=(t // tm,),
        in_specs=[pl.BlockSpec((tm, d), row),
                  pl.BlockSpec((1, d), const),
                  pl.BlockSpec(w_bf.shape, const),
                  pl.BlockSpec((tm, tab.shape[1]), lambda i: (i % tab_blocks, 0))],
        out_specs=[pl.BlockSpec((tm, w), row) for w in widths],
        out_shape=[jax.ShapeDtypeStruct((t, w), dt) for w, dt in zip(widths, dtypes)],
        compiler_params=_cparams(("arbitrary",)),
        name="proj",
    )(x, g, w_bf, tab)


def _ret_log_gammas():
    return tuple(math.log(1.0 - 2.0 ** (-5.0 - hh)) for hh in range(RET_HEADS))


def _ret_kernel(q_ref, k_ref, v_ref, g_ref, s0_ref, y_ref, st_ref, *, cb, n_valid):
    @pl.when(pl.program_id(1) == 0)
    def _():
        st_ref[...] = s0_ref[...]

    row = lax.broadcasted_iota(jnp.int32, (cb, cb), 0)
    col = lax.broadcasted_iota(jnp.int32, (cb, cb), 1)
    dist = (row - col).astype(F32)
    idx = lax.broadcasted_iota(jnp.int32, (cb, 1), 0).astype(F32)
    for hh, lg in enumerate(_ret_log_gammas()):
        q = q_ref[0, :, hh * RET_DK:(hh + 1) * RET_DK]
        k = k_ref[0, :, hh * RET_DK:(hh + 1) * RET_DK]
        v = v_ref[0, :, hh * RET_DV:(hh + 1) * RET_DV]
        g = g_ref[0, :, hh * RET_DV:(hh + 1) * RET_DV].astype(F32)
        st = st_ref[0, hh]
        decay = jnp.where(dist >= 0, jnp.exp(lg * jnp.maximum(dist, 0.0)), 0.0)
        s = lax.dot_general(q, k, (((1,), (1,)), ((), ())), preferred_element_type=F32) * decay
        o = jnp.dot(s.astype(BF16), v, preferred_element_type=F32)
        o = o + jnp.dot(q, st.astype(BF16), preferred_element_type=F32) * jnp.exp(lg * (idx + 1.0))
        kw = (k.astype(F32) * jnp.exp(lg * (n_valid - 1.0 - idx))).astype(BF16)
        st_ref[0, hh] = math.exp(lg * n_valid) * st + lax.dot_general(
            kw, v, (((0,), (0,)), ((), ())), preferred_element_type=F32)
        on = o * lax.rsqrt(jnp.mean(o * o, axis=-1, keepdims=True) + EPS)
        y_ref[0, :, hh * RET_DV:(hh + 1) * RET_DV] = (g * jax.nn.sigmoid(g) * on).astype(BF16)


def _retention(q, k, v, g, s0, cb, n_valid):
    b, s, _ = q.shape
    s0_map = (lambda i, c: (i, 0, 0, 0)) if s0.shape[0] == b else (lambda i, c: (0, 0, 0, 0))
    blk = lambda w: pl.BlockSpec((1, cb, w), lambda i, c: (i, c, 0))
    st_block = (1, RET_HEADS, RET_DK, RET_DV)
    return pl.pallas_call(
        functools.partial(_ret_kernel, cb=cb, n_valid=n_valid),
        grid=(b, s // cb),
        in_specs=[blk(q.shape[2]), blk(k.shape[2]), blk(v.shape[2]), blk(g.shape[2]),
                  pl.BlockSpec(st_block, s0_map)],
        out_specs=[blk(v.shape[2]), pl.BlockSpec(st_block, lambda i, c: (i, 0, 0, 0))],
        out_shape=[jax.ShapeDtypeStruct(v.shape, BF16),
                   jax.ShapeDtypeStruct((b,) + st_block[1:], F32)],
        compiler_params=_cparams(("arbitrary", "arbitrary")),
        name="retention",
    )(q, k, v, g, s0)


def _scores(qq, kk):
    return lax.dot_general(qq, kk, (((1,), (1,)), ((), ())), preferred_element_type=F32)


def _softmax_seed(scores, values):
    ms = [jnp.max(s, axis=-1, keepdims=True) for s in scores]
    ps = [jnp.exp2(s - m) for s, m in zip(scores, ms)]
    out = []
    for m, p, vv in zip(ms, ps, values):
        out += [m, jnp.sum(p, axis=-1, keepdims=True),
                jnp.dot(p.astype(BF16), vv, preferred_element_type=F32)]
    return out


def _softmax_step(carry, scores, values):
    m2s = [jnp.maximum(carry[3 * c], jnp.max(s, axis=-1, keepdims=True)) for c, s in enumerate(scores)]
    ps = [jnp.exp2(s - m2) for s, m2 in zip(scores, m2s)]
    out = []
    for c, (m2, p, vv) in enumerate(zip(m2s, ps, values)):
        m, l, acc = carry[3 * c:3 * c + 3]
        a = jnp.exp2(m - m2)
        out += [m2, a * l + jnp.sum(p, axis=-1, keepdims=True),
                a * acc + jnp.dot(p.astype(BF16), vv, preferred_element_type=F32)]
    return out


def _sub_norm(o1, o2, lam, gs, coef):
    o = o1 - lam * o2
    return o * lax.rsqrt(jnp.mean(o * o, axis=-1, keepdims=True) + EPS) * gs * coef


SUM_ROWS = 16
VT_ROWS = HEAD_W + SUM_ROWS


def _softmax_seed_t(scores_t, values_t):
    ms = [jnp.max(s, axis=0, keepdims=True) for s in scores_t]
    ps = [jnp.exp2(s - m) for s, m in zip(scores_t, ms)]
    out = []
    for m, p, vt in zip(ms, ps, values_t):
        out += [m, jnp.dot(vt, p.astype(BF16), preferred_element_type=F32)]
    return out


def _softmax_step_t(carry, scores_t, values_t):
    m2s = [jnp.maximum(carry[2 * c], jnp.max(s, axis=0, keepdims=True)) for c, s in enumerate(scores_t)]
    ps = [jnp.exp2(s - m2) for s, m2 in zip(scores_t, m2s)]
    out = []
    for c, (m2, p, vt) in enumerate(zip(m2s, ps, values_t)):
        m, acc = carry[2 * c:2 * c + 2]
        out += [m2, jnp.exp2(m - m2) * acc + jnp.dot(vt, p.astype(BF16), preferred_element_type=F32)]
    return out


def _attn_prompt_kernel(lam_ref, q_ref, km_ref, vm_ref, ke_ref, ve_ref, gst_ref, o_ref, kb_ref, vt_ref,
                        *, sq, tq, hp, n_ext, coef):
    heads = [slice(hh * HEAD_W, (hh + 1) * HEAD_W) for hh in range(hp)]
    vrows = [slice(hh * VT_ROWS, (hh + 1) * VT_ROWS) for hh in range(hp)]
    kb_ref[...] = km_ref[0].astype(BF16)
    ones = jnp.ones((SUM_ROWS, tq), BF16)
    for j in range(sq // tq):
        for hh, sl in enumerate(heads):
            vt_ref[j, hh * VT_ROWS:hh * VT_ROWS + HEAD_W, :] = vm_ref[0, j * tq:(j + 1) * tq, sl].T.astype(BF16)
            vt_ref[j, hh * VT_ROWS + HEAD_W:(hh + 1) * VT_ROWS, :] = ones
    ke = [ke_ref[0, :, sl].astype(BF16) for sl in heads]
    vet = [jnp.concatenate([ve_ref[0, :, sl].T.astype(BF16), jnp.ones((SUM_ROWS, ve_ref.shape[1]), BF16)], axis=0)
           for sl in heads]
    lam = lam_ref[0]
    lo = lax.broadcasted_iota(jnp.int32, (1, HEAD_W), 1) < DIFF_DH
    ext_ok = lax.broadcasted_iota(jnp.int32, (ke_ref.shape[1], 1), 0) < n_ext
    diag = (lax.broadcasted_iota(jnp.int32, (tq, tq), 0) // CHUNK
            <= lax.broadcasted_iota(jnp.int32, (tq, tq), 1) // CHUNK)

    def q_tile(qi, _):
        q0 = pl.multiple_of(qi * tq, tq)
        qs = []
        for sl in heads:
            q = q_ref[0, pl.ds(q0, tq), sl]
            zero = jnp.zeros_like(q)
            qs += [jnp.where(lo, q, zero), jnp.where(lo, zero, q)]

        ext_s = [jnp.where(ext_ok, _scores(ke[c // 2], qq), NEG) for c, qq in enumerate(qs)]
        carry = _softmax_seed_t(ext_s, [vet[c // 2] for c in range(len(qs))])

        def step(j, carry, mask=None):
            k0 = pl.multiple_of(j * tq, tq)
            ss = [_scores(kb_ref[pl.ds(k0, tq), heads[c // 2]], qq) for c, qq in enumerate(qs)]
            if mask is not None:
                ss = [jnp.where(mask, s, NEG) for s in ss]
            return tuple(_softmax_step_t(carry, ss, [vt_ref[j, vrows[c // 2], :] for c in range(len(qs))]))

        carry = lax.fori_loop(0, qi, step, tuple(carry))
        carry = step(qi, carry, diag)
        for hh, sl in enumerate(heads):
            _, a1, _, a2 = carry[4 * hh:4 * hh + 4]
            o = a1[:HEAD_W] / a1[HEAD_W:HEAD_W + 1] - lam * (a2[:HEAD_W] / a2[HEAD_W:HEAD_W + 1])
            on = o * lax.rsqrt(jnp.mean(o * o, axis=0, keepdims=True) + EPS) * gst_ref[...] * coef
            o_ref[0, pl.ds(q0, tq), sl] = on.T.astype(BF16)
        return 0

    lax.fori_loop(0, sq // tq, q_tile, 0)


def _attention_prompt(q, k_main, v_main, k_ext, v_ext, n_ext, lam, g_subln, coef, tq, hp):
    b, sq, hw = q.shape
    w = hp * HEAD_W
    bh = pl.BlockSpec((1, sq, w), lambda i, h: (i, 0, h))
    ext = pl.BlockSpec((1, k_ext.shape[1], w), lambda i, h: (0, 0, h))
    gst = jnp.broadcast_to(g_subln.reshape(HEAD_W, 1), (HEAD_W, tq))
    return pl.pallas_call(
        functools.partial(_attn_prompt_kernel, sq=sq, tq=tq, hp=hp, n_ext=n_ext, coef=coef),
        grid=(b, hw // w),
        in_specs=[pl.BlockSpec(memory_space=pltpu.SMEM), bh, bh, bh, ext, ext,
                  pl.BlockSpec((HEAD_W, tq), lambda i, h: (0, 0))],
        out_specs=bh,
        out_shape=jax.ShapeDtypeStruct(q.shape, BF16),
        scratch_shapes=[pltpu.VMEM((sq, w), BF16), pltpu.VMEM((sq // tq, hp * VT_ROWS, tq), BF16)],
        compiler_params=_cparams(("arbitrary", "arbitrary")),
        name="attention_prompt",
    )(lam, q, k_main, v_main, k_ext, v_ext, gst)


def _attn_sample_kernel(lam_ref, q_ref, kc_ref, vc_ref, ke_ref, ve_ref, gs_ref, o_ref, m_ref, l_ref, acc_ref,
                        *, ds, nh, n_ext, coef):
    t = pl.program_id(1)
    tkv = kc_ref.shape[1] // nh
    lo = lax.broadcasted_iota(jnp.int32, (1, HEAD_W), 1) < DIFF_DH
    ext_ok = lax.broadcasted_iota(jnp.int32, (1, ke_ref.shape[1]), 1) < n_ext
    heads = [slice(hh * HEAD_W, (hh + 1) * HEAD_W) for hh in range(nh)]

    def stacked_q(sl):
        q = q_ref[0, :, sl]
        zero = jnp.zeros_like(q)
        return jnp.concatenate([jnp.where(lo, q, zero), jnp.where(lo, zero, q)], axis=0)

    qs = [stacked_q(sl) for sl in heads]

    def write_stats(stats):
        for hh in range(nh):
            m_ref[hh], l_ref[hh], acc_ref[hh] = stats[3 * hh:3 * hh + 3]

    @pl.when(t == 0)
    def _():
        ext_s = [jnp.where(ext_ok, _scores(qq, ke_ref[0, :, sl].astype(BF16)), NEG) for qq, sl in zip(qs, heads)]
        write_stats(_softmax_seed(ext_s, [ve_ref[0, :, sl].astype(BF16) for sl in heads]))

    carry = []
    for hh in range(nh):
        carry += [m_ref[hh], l_ref[hh], acc_ref[hh]]
    ss = [_scores(qq, kc_ref[0, pl.ds(hh, tkv, stride=nh), :].astype(BF16)) for hh, qq in enumerate(qs)]
    write_stats(_softmax_step(carry, ss, [vc_ref[0, pl.ds(hh, tkv, stride=nh), :].astype(BF16) for hh in range(nh)]))

    @pl.when(t == pl.num_programs(1) - 1)
    def _():
        lam = lam_ref[0]
        for hh, sl in enumerate(heads):
            o = acc_ref[hh] / l_ref[hh]
            o_ref[0, :, sl] = _sub_norm(o[:ds], o[ds:], lam, gs_ref[...], coef).astype(BF16)


def _attention_sample(q, k_cache, v_cache, k_ext, v_ext, n_ext, lam, g_subln, coef, tkv):
    b, ds, hw = q.shape
    _, pl_len, nh, _ = k_cache.shape
    per_b = lambda rows: pl.BlockSpec((1, rows, hw), lambda i, t: (i, 0, 0))
    cache = pl.BlockSpec((1, tkv * nh, HEAD_W), lambda i, t: (i, t, 0))
    return pl.pallas_call(
        functools.partial(_attn_sample_kernel, ds=ds, nh=nh, n_ext=n_ext, coef=coef),
        grid=(b, pl_len // tkv),
        in_specs=[pl.BlockSpec(memory_space=pltpu.SMEM), per_b(ds), cache, cache,
                  per_b(k_ext.shape[1]), per_b(k_ext.shape[1]),
                  pl.BlockSpec((1, HEAD_W), lambda i, t: (0, 0))],
        out_specs=per_b(ds),
        out_shape=jax.ShapeDtypeStruct(q.shape, BF16),
        scratch_shapes=[pltpu.VMEM((nh, 2 * ds, 1), F32), pltpu.VMEM((nh, 2 * ds, 1), F32),
                        pltpu.VMEM((nh, 2 * ds, HEAD_W), F32)],
        compiler_params=_cparams(("arbitrary", "arbitrary")),
        name="attention_sample",
    )(lam, q, k_cache.reshape(b, pl_len * nh, HEAD_W), v_cache.reshape(b, pl_len * nh, HEAD_W),
      k_ext, v_ext, g_subln)


def _mix_kernel(x_ref, yr_ref, od_ref, ga_ref, gb_ref, wr_ref, wd_ref, wo_ref, gf_ref, wrt_ref, br_ref,
                x1_ref, h_ref, lgt_ref):
    ya = jnp.dot(yr_ref[...], wr_ref[...], preferred_element_type=F32)
    yb = jnp.dot(od_ref[...], wd_ref[...], preferred_element_type=F32)
    z = jax.nn.sigmoid(ga_ref[...].astype(F32)) * ya + jax.nn.sigmoid(gb_ref[...].astype(F32)) * yb
    x1 = x_ref[...] + jnp.dot(z.astype(BF16), wo_ref[...], preferred_element_type=F32)
    x1_ref[...] = x1
    h = x1 * lax.rsqrt(jnp.mean(x1 * x1, axis=-1, keepdims=True) + EPS) * gf_ref[...]
    _tile_rows_store(h_ref, h)
    lgt_ref[...] = _scores(wrt_ref[...], h.astype(BF16)) + br_ref[...]


def _mix_tail_kernel(*refs):
    th_ref, tlgt_ref = refs[11:13]
    h_ref, lgt_ref = refs[14:16]
    last = pl.num_programs(0) - 1

    @pl.when(pl.program_id(0) < last)
    def _():
        _mix_kernel(*refs[:11], *refs[13:])

    @pl.when(pl.program_id(0) == last)
    def _():
        h_ref[...] = th_ref[...]
        lgt_ref[...] = tlgt_ref[...]


def _mix(x, yr, od, ga, gb, wr, wd, wo, gf, wrt, br, tm, tail=None):
    t, d = x.shape
    ne = wrt.shape[0]
    nt = t // tm
    extra = 0 if tail is None else 1
    row = lambda w: pl.BlockSpec((tm, w), lambda i: (jnp.minimum(i, nt - 1), 0))
    full = lambda a: pl.BlockSpec(a.shape, lambda i: (0, 0))
    ops = (x, yr, od, ga, gb, wr, wd, wo, gf, wrt, br) + (() if tail is None else tuple(tail))
    return pl.pallas_call(
        _mix_kernel if tail is None else _mix_tail_kernel,
        grid=(nt + extra,),
        in_specs=[row(d), row(yr.shape[1]), row(od.shape[1]), row(d), row(d)] + [full(a) for a in ops[5:]],
        out_specs=[row(d), pl.BlockSpec((tm * SUBLANES, LANES), lambda i: (i, 0)),
                   pl.BlockSpec((ne, tm), lambda i: (0, i))],
        out_shape=[jax.ShapeDtypeStruct((t, d), F32),
                   jax.ShapeDtypeStruct(((t + extra * tm) * SUBLANES, LANES), F32),
                   jax.ShapeDtypeStruct((ne, t + extra * tm), F32)],
        compiler_params=_cparams(("arbitrary",)),
        name="mix",
    )(*ops)


def _route_kernel(lgt_ref, e_ref, gate_ref, rank_ref, cnt_ref):
    @pl.when(pl.program_id(0) == 0)
    def _():
        cnt_ref[...] = jnp.zeros_like(cnt_ref)

    v = lgt_ref[...]
    ne, tn = v.shape
    eid = lax.broadcasted_iota(jnp.int32, (ne, tn), 0)
    tops, sels = [], []
    for k in range(TOP_K):
        m = jnp.max(v, axis=0, keepdims=True)
        idx = jnp.min(jnp.where(v == m, eid, ne), axis=0, keepdims=True)
        sel = eid == idx
        e_ref[k:k + 1, :] = idx
        tops.append(m)
        sels.append(sel)
        v = jnp.where(sel, -jnp.inf, v)
    ex = [jnp.exp(m - tops[0]) for m in tops]
    den = ex[0] + ex[1] + ex[2] + ex[3]
    for k in range(TOP_K):
        gate_ref[k:k + 1, :] = ex[k] / den

    chosen = sels[0] | sels[1] | sels[2] | sels[3]
    before = (lax.broadcasted_iota(jnp.int32, (tn, tn), 0) < lax.broadcasted_iota(jnp.int32, (tn, tn), 1))
    prior = jnp.dot(chosen.astype(BF16), before.astype(BF16), preferred_element_type=F32) + cnt_ref[:, 0:1]
    for k in range(TOP_K):
        rank_ref[k:k + 1, :] = jnp.sum(jnp.where(sels[k], prior, 0.0), axis=0, keepdims=True).astype(jnp.int32)
    cnt_ref[...] = cnt_ref[...] + jnp.sum(chosen.astype(F32), axis=1, keepdims=True)


def _route_topk(lgt, tn):
    ne, t = lgt.shape
    kt = pl.BlockSpec((TOP_K, tn), lambda i: (0, i))
    return pl.pallas_call(
        _route_kernel,
        grid=(t // tn,),
        in_specs=[pl.BlockSpec((ne, tn), lambda i: (0, i))],
        out_specs=[kt, kt, kt, pl.BlockSpec((ne, LANES), lambda i: (0, 0))],
        out_shape=[jax.ShapeDtypeStruct((TOP_K, t), jnp.int32), jax.ShapeDtypeStruct((TOP_K, t), F32),
                   jax.ShapeDtypeStruct((TOP_K, t), jnp.int32), jax.ShapeDtypeStruct((ne, LANES), F32)],
        compiler_params=_cparams(("arbitrary",)),
        name="route",
    )(lgt)


GATHER_UNROLL = 8


def _gather_rows(src_hbm, idx_ref, n, buf_ref, slot, sem):
    def body(g, _):
        for u in range(GATHER_UNROLL):
            r = g * GATHER_UNROLL + u
            src = pl.multiple_of(idx_ref[0, 0, r], SUBLANES)
            dst = pl.multiple_of(r * SUBLANES, SUBLANES)
            pltpu.make_async_copy(src_hbm.at[pl.ds(src, SUBLANES)], buf_ref.at[slot, pl.ds(dst, SUBLANES)],
                                  sem.at[slot]).start(priority=u % 2)
        return 0
    lax.fori_loop(0, n // GATHER_UNROLL, body, 0)


def _wait_rows(src_hbm, n, buf_ref, slot, sem):
    pltpu.make_async_copy(src_hbm.at[pl.ds(0, n * SUBLANES)], buf_ref.at[slot], sem.at[slot]).wait()


def _moe_kernel(be_ref, nu_ref, tok_ref, tokn_ref, h_hbm, wgu_ref, bgu_ref, wdn_ref, bdn_ref, out_ref,
                buf_ref, wgu_bf, wdn_bf, sem):
    i = pl.program_id(0)
    n_used = nu_ref[0]
    slot = i % 2
    d_ff = wdn_bf.shape[0]

    @pl.when(i == 0)
    def _():
        _gather_rows(h_hbm, tok_ref, MOE_ROWS, buf_ref, 0, sem)

    @pl.when(i + 1 < n_used)
    def _():
        _gather_rows(h_hbm, tokn_ref, MOE_ROWS, buf_ref, 1 - slot, sem)

    @pl.when(i < n_used)
    def _():
        @pl.when((i == 0) | (be_ref[i] != be_ref[jnp.maximum(i - 1, 0)]))
        def _():
            wgu_bf[...] = wgu_ref[0].astype(BF16)
            wdn_bf[...] = wdn_ref[0].astype(BF16)

        _wait_rows(h_hbm, MOE_ROWS, buf_ref, slot, sem)

        xb = _tile_rows_load(buf_ref, slot, 0, MOE_ROWS).astype(BF16)
        gu = jnp.dot(xb, wgu_bf[...], preferred_element_type=F32) + bgu_ref[0]
        gate = jnp.minimum(gu[:, :d_ff], SWIGLU_LIMIT)
        up = jnp.clip(gu[:, d_ff:], -SWIGLU_LIMIT, SWIGLU_LIMIT)
        act = (up + 1.0) * gate * jax.nn.sigmoid(SWIGLU_ALPHA * gate)
        _tile_rows_store(out_ref, jnp.dot(act.astype(BF16), wdn_bf[...], preferred_element_type=F32) + bdn_ref[0])

    @pl.when(i >= n_used)
    def _():
        out_ref[...] = jnp.zeros_like(out_ref)


def _moe(h, tok, block_e, n_used, w_gu, b_gu, w_down, b_down):
    d = h.shape[1] * SUBLANES
    nb = tok.shape[0]
    ne, _, two_ff = w_gu.shape
    d_ff = two_ff // 2
    e_map = lambda i, be, nu: (be[i], 0, 0)
    grid_spec = pltpu.PrefetchScalarGridSpec(
        num_scalar_prefetch=2,
        grid=(nb,),
        in_specs=[
            pl.BlockSpec((1, 1, MOE_ROWS), lambda i, be, nu: (i, 0, 0), memory_space=pltpu.SMEM),
            pl.BlockSpec((1, 1, MOE_ROWS), lambda i, be, nu: (jnp.minimum(i + 1, nb - 1), 0, 0),
                         memory_space=pltpu.SMEM),
            pl.BlockSpec(memory_space=pl.ANY),
            pl.BlockSpec((1, d, two_ff), e_map),
            pl.BlockSpec((1, 1, two_ff), e_map),
            pl.BlockSpec((1, d_ff, d), e_map),
            pl.BlockSpec((1, 1, d), e_map),
        ],
        out_specs=pl.BlockSpec((MOE_ROWS * SUBLANES, LANES), lambda i, be, nu: (i, 0)),
        scratch_shapes=[pltpu.VMEM((2, MOE_ROWS * SUBLANES, LANES), F32),
                        pltpu.VMEM((d, two_ff), BF16),
                        pltpu.VMEM((d_ff, d), BF16),
                        pltpu.SemaphoreType.DMA((2,))],
    )
    return pl.pallas_call(
        _moe_kernel,
        grid_spec=grid_spec,
        out_shape=jax.ShapeDtypeStruct((nb * MOE_ROWS * SUBLANES, LANES), F32),
        compiler_params=_cparams(("arbitrary",)),
        name="moe",
    )(block_e, n_used, tok, tok, h, w_gu, b_gu.reshape(ne, 1, two_ff), w_down, b_down.reshape(ne, 1, d))


def _combine_kernel(dst_ref, dstn_ref, x_ref, gate_ref, gf_ref, rows_hbm, y_ref, buf_ref, sem, *, tm, nsteps):
    i = pl.program_id(0)
    slot = i % 2
    n = tm * TOP_K

    @pl.when(i == 0)
    def _():
        _gather_rows(rows_hbm, dst_ref, n, buf_ref, 0, sem)

    @pl.when(i + 1 < nsteps)
    def _():
        _gather_rows(rows_hbm, dstn_ref, n, buf_ref, 1 - slot, sem)

    _wait_rows(rows_hbm, n, buf_ref, slot, sem)

    y = x_ref[...]
    gate = gate_ref[...]
    for k in range(TOP_K):
        y = y + gate[:, k:k + 1] * _tile_rows_load(buf_ref, slot, k * tm, tm)
    y_ref[...] = y * lax.rsqrt(jnp.mean(y * y, axis=-1, keepdims=True) + EPS) * gf_ref[...]


def _combine(x1, gates, dest, rows, g_final, tm):
    t, d = x1.shape
    nsteps = t // tm
    n = tm * TOP_K
    return pl.pallas_call(
        functools.partial(_combine_kernel, tm=tm, nsteps=nsteps),
        grid=(nsteps,),
        in_specs=[pl.BlockSpec((1, 1, n), lambda i: (i, 0, 0), memory_space=pltpu.SMEM),
                  pl.BlockSpec((1, 1, n), lambda i: (jnp.minimum(i + 1, nsteps - 1), 0, 0),
                               memory_space=pltpu.SMEM),
                  pl.BlockSpec((tm, d), lambda i: (i, 0)),
                  pl.BlockSpec((tm, TOP_K), lambda i: (i, 0)),
                  pl.BlockSpec((1, d), lambda i: (0, 0)),
                  pl.BlockSpec(memory_space=pl.ANY)],
        out_specs=pl.BlockSpec((tm, d), lambda i: (i, 0)),
        out_shape=jax.ShapeDtypeStruct((t, d), F32),
        scratch_shapes=[pltpu.VMEM((2, n * SUBLANES, LANES), F32), pltpu.SemaphoreType.DMA((2,))],
        compiler_params=_cparams(("arbitrary",)),
        name="combine",
    )(dest, dest, x1, gates, g_final, rows)


def _route(lgt, tn):
    t = lgt.shape[1]
    n_assign = t * TOP_K
    e4, gates, rank, cnt = _route_topk(lgt, tn)
    counts = cnt[:, 0].astype(jnp.int32)
    padded = (counts + MOE_ROWS - 1) // MOE_ROWS * MOE_ROWS
    pend = jnp.cumsum(padded)
    pstart = pend - padded
    start = jnp.cumsum(counts) - counts
    dest = pstart[e4] + rank
    nb = n_assign // MOE_ROWS + N_EXPERTS
    n_used = (pend[-1] // MOE_ROWS).astype(jnp.int32).reshape(1)
    block_e = jnp.minimum(jnp.searchsorted(pend, jnp.arange(nb, dtype=jnp.int32) * MOE_ROWS, side='right'),
                          N_EXPERTS - 1).astype(jnp.int32)
    order = jnp.argsort(e4.T.reshape(-1))
    r = jnp.arange(nb * MOE_ROWS, dtype=jnp.int32)
    e_r = jnp.repeat(block_e, MOE_ROWS)
    within = r - pstart[e_r]
    src = jnp.clip(start[e_r] + jnp.minimum(within, counts[e_r] - 1), 0, n_assign - 1)
    tok = (order[src] // TOP_K).astype(jnp.int32)
    return gates.T, dest * SUBLANES, (tok * SUBLANES).reshape(nb, 1, MOE_ROWS), block_e, n_used


def kernel(x_prompt, x_sample, cache_k, cache_v, state_ret, meta_tokens, g_mix, w_in,
           lam_q1, lam_k1, lam_q2, lam_k2, g_subln, w_ret_out, w_diff_out, w_out,
           g_ffn, w_router, b_router, w_gu, b_gu, w_down, b_down, g_final):
    b, s, d = x_prompt.shape
    db, ds, _ = x_sample.shape
    depth, _, pl_len = cache_k.shape[:3]
    assert depth == 1, "single-layer step only"
    nm = meta_tokens.shape[0]
    hw = DIFF_HEADS * HEAD_W

    lam_init = 0.8 - 0.6 * math.exp(-0.3 * 0)
    lam = (jnp.exp(jnp.sum(lam_q1[0] * lam_k1[0])) - jnp.exp(jnp.sum(lam_q2[0] * lam_k2[0]))
           + lam_init).reshape(1).astype(F32)
    coef = 1.0 - lam_init

    w_in_bf = w_in[0].astype(BF16)
    g_mix2 = g_mix[0].reshape(1, d)

    tm = 512 if (b * s) % 512 == 0 else s
    tab_p = _tables(N_META + jnp.arange(s, dtype=jnp.int32))
    rq, rk, rv, rg, dq, dk, dv, ga, gb = _proj(x_prompt.reshape(b * s, d), g_mix2, w_in_bf, tab_p, tm, s // tm)

    pos_small = jnp.concatenate([jnp.tile(N_META + pl_len + jnp.arange(ds, dtype=jnp.int32), db),
                                 jnp.arange(nm, dtype=jnp.int32)])
    x_small = jnp.concatenate([x_sample.reshape(db * ds, d), meta_tokens.astype(F32)], axis=0)
    small = _proj(x_small, g_mix2, w_in_bf, _tables(pos_small), x_small.shape[0], 1)
    ns = db * ds
    srq, srk, srv, srg, sdq, sdk, sdv, sga, sgb = [a[:ns] for a in small]
    mrq, mrk, mrv, mrg, _, mdk, mdv, _, _ = [a[ns:] for a in small]

    cb_small = 128

    def pad_rows(a, nb_, n_):
        a = a.reshape(nb_, n_, a.shape[-1])
        return jnp.pad(a, ((0, 0), (0, cb_small - n_), (0, 0)))

    zero_state = jnp.zeros((1, RET_HEADS, RET_DK, RET_DV), F32)
    _, s_meta = _retention(pad_rows(mrq, 1, nm), pad_rows(mrk, 1, nm), pad_rows(mrv, 1, nm),
                           pad_rows(mrg, 1, nm), zero_state, cb_small, nm)
    cb = 256
    r3 = lambda a: a.reshape(b, s, a.shape[-1])
    yr, ret_p = _retention(r3(rq), r3(rk), r3(rv), r3(rg), s_meta, cb, cb)
    syr, ret_s = _retention(pad_rows(srq, db, ds), pad_rows(srk, db, ds), pad_rows(srv, db, ds),
                            pad_rows(srg, db, ds), state_ret[0], cb_small, ds)
    syr = syr[:, :ds].reshape(ns, -1)

    g_sub = g_subln[0].reshape(1, HEAD_W)
    ext_rows = 128
    pad_ext = lambda a: jnp.pad(a, ((0, 0), (0, ext_rows - a.shape[1]), (0, 0)))
    od = _attention_prompt(r3(dq), r3(dk), r3(dv), pad_ext(mdk[None]), pad_ext(mdv[None]), nm,
                           lam, g_sub, coef, 256, 4)
    sdk3, sdv3 = sdk.reshape(db, ds, hw), sdv.reshape(db, ds, hw)
    ke_s = jnp.concatenate([jnp.broadcast_to(mdk[None], (db, nm, hw)), sdk3], axis=1)
    ve_s = jnp.concatenate([jnp.broadcast_to(mdv[None], (db, nm, hw)), sdv3], axis=1)
    sod = _attention_sample(sdq.reshape(db, ds, hw), cache_k[0], cache_v[0], pad_ext(ke_s), pad_ext(ve_s),
                            nm + ds, lam, g_sub, coef, 512)

    wr, wd, wo = w_ret_out[0].astype(BF16), w_diff_out[0].astype(BF16), w_out[0].astype(BF16)
    gf = g_ffn[0].reshape(1, d)
    wrt = w_router[0].T.astype(BF16)
    br = b_router[0].astype(F32).reshape(N_EXPERTS, 1)
    sx1, sh, slgt = _mix(x_sample.reshape(ns, d), syr, sod.reshape(ns, hw), sga, sgb, wr, wd, wo, gf, wrt, br, ns)
    x1, h, lgt = _mix(x_prompt.reshape(b * s, d), yr.reshape(b * s, -1), od.reshape(b * s, hw), ga, gb,
                      wr, wd, wo, gf, wrt, br, ns, (sh, slgt))

    t = b * s
    gates, dest, tok, block_e, n_used = _route(lgt, ns)
    rows = _moe(h, tok, block_e, n_used, w_gu[0], b_gu[0], w_down[0], b_down[0])
    gfin = g_final.reshape(1, d)

    def combine(x1g, lo, n):
        tc = min(256, n)
        dg = dest[:, lo:lo + n].reshape(TOP_K, n // tc, tc).swapaxes(0, 1).reshape(n // tc, 1, TOP_K * tc)
        return _combine(x1g, gates[lo:lo + n], dg, rows, gfin, tc)

    y_p = combine(x1, 0, t)
    y_s = combine(sx1, t, ns)

    k_p = jnp.concatenate([jnp.broadcast_to(mdk[None], (b, nm, hw)), r3(dk)], axis=1)
    v_p = jnp.concatenate([jnp.broadcast_to(mdv[None], (b, nm, hw)), r3(dv)], axis=1)
    shp = lambda a: a.reshape(1, a.shape[0], a.shape[1], DIFF_HEADS, HEAD_W)
    return (y_p.reshape(b, s, d), y_s.reshape(db, ds, d), ret_p[None], shp(k_p), shp(v_p),
            ret_s[None], shp(sdk3), shp(sdv3))
```

```python
import functools
import math

import jax
import jax.numpy as jnp
from jax import lax
from jax.experimental import pallas as pl
from jax.experimental.pallas import tpu as pltpu

F32 = jnp.float32
BF16 = jnp.bfloat16

EPS = 1e-6
N_META = 16
CHUNK = 64
RET_HEADS = 4
RET_DK = 128
RET_DV = 256
RET_THETA = 10000.0
DIFF_HEADS = 8
DIFF_DH = 64
ROT_DIM = DIFF_DH // 4
ROPE_THETA = 500000.0
N_EXPERTS = 32
TOP_K = 4
SWIGLU_LIMIT = 7.0
SWIGLU_ALPHA = 1.702

LANES = 128
SUBLANES = 8
HEAD_W = 2 * DIFF_DH
MOE_ROWS = 256
VMEM_LIMIT = 56 * 1024 * 1024
NEG = -1e30
Q_SCALE = DIFF_DH ** -0.5 * math.log2(math.e)


def _cparams(sem):
    return pltpu.CompilerParams(dimension_semantics=sem, vmem_limit_bytes=VMEM_LIMIT)


def _tile_rows_store(ref, val):
    n = val.shape[0]
    for c in range(SUBLANES):
        ref[pl.ds(c, n, stride=SUBLANES), :] = val[:, c * LANES:(c + 1) * LANES]


def _tile_rows_load(ref, slot, first, n):
    return jnp.concatenate([ref[slot, pl.ds(first * SUBLANES + c, n, stride=SUBLANES), :]
                            for c in range(SUBLANES)], axis=1)


def _ret_tables(pos):
    angle = RET_THETA ** (-jnp.linspace(0.0, 1.0, RET_DK // 2, dtype=F32))
    angle = jnp.repeat(angle, 2)
    ang = pos.astype(F32)[:, None] * angle[None, :]
    cos, sin = jnp.cos(ang), jnp.sin(ang)
    even = (jnp.arange(RET_DK) % 2 == 0)[None, :]
    return cos, jnp.where(even, -sin, 0.0), jnp.where(even, 0.0, sin)


def _rope_tables(pos):
    half = ROT_DIM // 2
    inv = ROPE_THETA ** (-jnp.arange(half, dtype=F32) * (2.0 / ROT_DIM))
    ang = pos.astype(F32)[:, None] * inv[None, :]
    cos, sin = jnp.cos(ang), jnp.sin(ang)
    jj = jnp.arange(HEAD_W) % DIFF_DH
    first = (jj < half)[None, :]
    second = ((jj >= half) & (jj < ROT_DIM))[None, :]
    cos_l = jnp.take(cos, jj % half, axis=1)
    sin_l = jnp.take(sin, jj % half, axis=1)
    c = jnp.where(first | second, cos_l, 1.0)
    sa = jnp.where(first, -sin_l, 0.0)
    sb = jnp.where(second, sin_l, 0.0)
    return c, sa, sb


def _tables(pos):
    return jnp.concatenate(_ret_tables(pos) + _rope_tables(pos), axis=1)


def _proj_kernel(x_ref, g_ref, w_ref, tab_ref, rq_ref, rk_ref, rv_ref, rg_ref,
                 dq_ref, dk_ref, dv_ref, ga_ref, gb_ref):
    x = x_ref[...]
    ms = jnp.mean(x * x, axis=-1, keepdims=True)
    h = (x * lax.rsqrt(ms + EPS) * g_ref[...]).astype(BF16)

    def mm(c0, width):
        return jnp.dot(h, w_ref[:, c0:c0 + width], preferred_element_type=F32)

    def rot(p, t0, near, far):
        c = tab_ref[:, t0:t0 + LANES]
        sa = tab_ref[:, t0 + LANES:t0 + 2 * LANES]
        sb = tab_ref[:, t0 + 2 * LANES:t0 + 3 * LANES]
        return p * c + pltpu.roll(p, LANES - near, 1) * sa + pltpu.roll(p, far, 1) * sb

    kw = RET_HEADS * RET_DK
    vw = RET_HEADS * RET_DV
    dw = DIFF_HEADS * HEAD_W
    c0 = 0
    p = mm(c0, kw)
    for hh in range(RET_HEADS):
        sl = slice(hh * LANES, (hh + 1) * LANES)
        rq_ref[:, sl] = rot(p[:, sl], 0, 1, 1).astype(BF16)
    c0 += kw
    p = mm(c0, kw)
    for hh in range(RET_HEADS):
        sl = slice(hh * LANES, (hh + 1) * LANES)
        rk_ref[:, sl] = (rot(p[:, sl], 0, 1, 1) * (RET_DK ** -0.5)).astype(BF16)
    c0 += kw
    rv_ref[...] = mm(c0, vw).astype(BF16)
    c0 += vw
    rg_ref[...] = mm(c0, vw).astype(BF16)
    c0 += vw
    half = ROT_DIM // 2
    p = mm(c0, dw)
    for hh in range(DIFF_HEADS):
        sl = slice(hh * LANES, (hh + 1) * LANES)
        dq_ref[:, sl] = (rot(p[:, sl], 3 * LANES, half, half) * Q_SCALE).astype(BF16)
    c0 += dw
    p = mm(c0, dw)
    for hh in range(DIFF_HEADS):
        sl = slice(hh * LANES, (hh + 1) * LANES)
        dk_ref[:, sl] = rot(p[:, sl], 3 * LANES, half, half)
    c0 += dw
    dv_ref[...] = mm(c0, dw)
    c0 += dw
    d = x.shape[1]
    ga_ref[...] = mm(c0, d).astype(BF16)
    c0 += d
    gb_ref[...] = mm(c0, d).astype(BF16)


def _proj(x, g, w_bf, tab, tm, tab_blocks):
    t, d = x.shape
    kw, vw, dw = RET_HEADS * RET_DK, RET_HEADS * RET_DV, DIFF_HEADS * HEAD_W
    widths = (kw, kw, vw, vw, dw, dw, dw, d, d)
    dtypes = (BF16, BF16, BF16, BF16, BF16, F32, F32, BF16, BF16)
    row = lambda i: (i, 0)
    const = lambda i: (0, 0)
    return pl.pallas_call(
        _proj_kernel,
        grid=(t // tm,),
        in_specs=[pl.BlockSpec((tm, d), row),
                  pl.BlockSpec((1, d), const),
                  pl.BlockSpec(w_bf.shape, const),
                  pl.BlockSpec((tm, tab.shape[1]), lambda i: (i % tab_blocks, 0))],
        out_specs=[pl.BlockSpec((tm, w), row) for w in widths],
        out_shape=[jax.ShapeDtypeStruct((t, w), dt) for w, dt in zip(widths, dtypes)],
        compiler_params=_cparams(("arbitrary",)),
        name="proj",
    )(x, g, w_bf, tab)


def _ret_log_gammas():
    return tuple(math.log(1.0 - 2.0 ** (-5.0 - hh)) for hh in range(RET_HEADS))


def _ret_kernel(q_ref, k_ref, v_ref, g_ref, s0_ref, y_ref, st_ref, *, cb, n_valid):
    @pl.when(pl.program_id(1) == 0)
    def _():
        st_ref[...] = s0_ref[...]

    row = lax.broadcasted_iota(jnp.int32, (cb, cb), 0)
    col = lax.broadcasted_iota(jnp.int32, (cb, cb), 1)
    dist = (row - col).astype(F32)
    idx = lax.broadcasted_iota(jnp.int32, (cb, 1), 0).astype(F32)
    for hh, lg in enumerate(_ret_log_gammas()):
        q = q_ref[0, :, hh * RET_DK:(hh + 1) * RET_DK]
        k = k_ref[0, :, hh * RET_DK:(hh + 1) * RET_DK]
        v = v_ref[0, :, hh * RET_DV:(hh + 1) * RET_DV]
        g = g_ref[0, :, hh * RET_DV:(hh + 1) * RET_DV].astype(F32)
        st = st_ref[0, hh]
        decay = jnp.where(dist >= 0, jnp.exp(lg * jnp.maximum(dist, 0.0)), 0.0)
        s = lax.dot_general(q, k, (((1,), (1,)), ((), ())), preferred_element_type=F32) * decay
        o = jnp.dot(s.astype(BF16), v, preferred_element_type=F32)
        o = o + jnp.dot(q, st.astype(BF16), preferred_element_type=F32) * jnp.exp(lg * (idx + 1.0))
        kw = (k.astype(F32) * jnp.exp(lg * (n_valid - 1.0 - idx))).astype(BF16)
        st_ref[0, hh] = math.exp(lg * n_valid) * st + lax.dot_general(
            kw, v, (((0,), (0,)), ((), ())), preferred_element_type=F32)
        on = o * lax.rsqrt(jnp.mean(o * o, axis=-1, keepdims=True) + EPS)
        y_ref[0, :, hh * RET_DV:(hh + 1) * RET_DV] = (g * jax.nn.sigmoid(g) * on).astype(BF16)


def _retention(q, k, v, g, s0, cb, n_valid):
    b, s, _ = q.shape
    s0_map = (lambda i, c: (i, 0, 0, 0)) if s0.shape[0] == b else (lambda i, c: (0, 0, 0, 0))
    blk = lambda w: pl.BlockSpec((1, cb, w), lambda i, c: (i, c, 0))
    st_block = (1, RET_HEADS, RET_DK, RET_DV)
    return pl.pallas_call(
        functools.partial(_ret_kernel, cb=cb, n_valid=n_valid),
        grid=(b, s // cb),
        in_specs=[blk(q.shape[2]), blk(k.shape[2]), blk(v.shape[2]), blk(g.shape[2]),
                  pl.BlockSpec(st_block, s0_map)],
        out_specs=[blk(v.shape[2]), pl.BlockSpec(st_block, lambda i, c: (i, 0, 0, 0))],
        out_shape=[jax.ShapeDtypeStruct(v.shape, BF16),
                   jax.ShapeDtypeStruct((b,) + st_block[1:], F32)],
        compiler_params=_cparams(("arbitrary", "arbitrary")),
        name="retention",
    )(q, k, v, g, s0)


def _scores(qq, kk):
    return lax.dot_general(qq, kk, (((1,), (1,)), ((), ())), preferred_element_type=F32)


def _softmax_seed(scores, values):
    ms = [jnp.max(s, axis=-1, keepdims=True) for s in scores]
    ps = [jnp.exp2(s - m) for s, m in zip(scores, ms)]
    out = []
    for m, p, vv in zip(ms, ps, values):
        out += [m, jnp.sum(p, axis=-1, keepdims=True),
                jnp.dot(p.astype(BF16), vv, preferred_element_type=F32)]
    return out


def _softmax_step(carry, scores, values):
    m2s = [jnp.maximum(carry[3 * c], jnp.max(s, axis=-1, keepdims=True)) for c, s in enumerate(scores)]
    ps = [jnp.exp2(s - m2) for s, m2 in zip(scores, m2s)]
    out = []
    for c, (m2, p, vv) in enumerate(zip(m2s, ps, values)):
        m, l, acc = carry[3 * c:3 * c + 3]
        a = jnp.exp2(m - m2)
        out += [m2, a * l + jnp.sum(p, axis=-1, keepdims=True),
                a * acc + jnp.dot(p.astype(BF16), vv, preferred_element_type=F32)]
    return out


def _sub_norm(o1, o2, lam, gs, coef):
    o = o1 - lam * o2
    return o * lax.rsqrt(jnp.mean(o * o, axis=-1, keepdims=True) + EPS) * gs * coef


SUM_ROWS = 16
VT_ROWS = HEAD_W + SUM_ROWS


def _softmax_seed_t(scores_t, values_t):
    ms = [jnp.max(s, axis=0, keepdims=True) for s in scores_t]
    ps = [jnp.exp2(s - m) for s, m in zip(scores_t, ms)]
    out = []
    for m, p, vt in zip(ms, ps, values_t):
        out += [m, jnp.dot(vt, p.astype(BF16), preferred_element_type=F32)]
    return out


def _softmax_step_t(carry, scores_t, values_t):
    m2s = [jnp.maximum(carry[2 * c], jnp.max(s, axis=0, keepdims=True)) for c, s in enumerate(scores_t)]
    ps = [jnp.exp2(s - m2) for s, m2 in zip(scores_t, m2s)]
    out = []
    for c, (m2, p, vt) in enumerate(zip(m2s, ps, values_t)):
        m, acc = carry[2 * c:2 * c + 2]
        out += [m2, jnp.exp2(m - m2) * acc + jnp.dot(vt, p.astype(BF16), preferred_element_type=F32)]
    return out


def _attn_prompt_kernel(lam_ref, q_ref, km_ref, vm_ref, ke_ref, ve_ref, gst_ref, o_ref, kb_ref, vt_ref,
                        *, sq, tq, hp, n_ext, coef):
    heads = [slice(hh * HEAD_W, (hh + 1) * HEAD_W) for hh in range(hp)]
    vrows = [slice(hh * VT_ROWS, (hh + 1) * VT_ROWS) for hh in range(hp)]
    kb_ref[...] = km_ref[0].astype(BF16)
    ones = jnp.ones((SUM_ROWS, tq), BF16)
    for j in range(sq // tq):
        for hh, sl in enumerate(heads):
            vt_ref[j, hh * VT_ROWS:hh * VT_ROWS + HEAD_W, :] = vm_ref[0, j * tq:(j + 1) * tq, sl].T.astype(BF16)
            vt_ref[j, hh * VT_ROWS + HEAD_W:(hh + 1) * VT_ROWS, :] = ones
    ke = [ke_ref[0, :, sl].astype(BF16) for sl in heads]
    vet = [jnp.concatenate([ve_ref[0, :, sl].T.astype(BF16), jnp.ones((SUM_ROWS, ve_ref.shape[1]), BF16)], axis=0)
           for sl in heads]
    lam = lam_ref[0]
    lo = lax.broadcasted_iota(jnp.int32, (1, HEAD_W), 1) < DIFF_DH
    ext_ok = lax.broadcasted_iota(jnp.int32, (ke_ref.shape[1], 1), 0) < n_ext
    diag = (lax.broadcasted_iota(jnp.int32, (tq, tq), 0) // CHUNK
            <= lax.broadcasted_iota(jnp.int32, (tq, tq), 1) // CHUNK)

    def q_tile(qi, _):
        q0 = pl.multiple_of(qi * tq, tq)
        qs = []
        for sl in heads:
            q = q_ref[0, pl.ds(q0, tq), sl]
            zero = jnp.zeros_like(q)
            qs += [jnp.where(lo, q, zero), jnp.where(lo, zero, q)]

        ext_s = [jnp.where(ext_ok, _scores(ke[c // 2], qq), NEG) for c, qq in enumerate(qs)]
        carry = _softmax_seed_t(ext_s, [vet[c // 2] for c in range(len(qs))])

        def step(j, carry, mask=None):
            k0 = pl.multiple_of(j * tq, tq)
            ss = [_scores(kb_ref[pl.ds(k0, tq), heads[c // 2]], qq) for c, qq in enumerate(qs)]
            if mask is not None:
                ss = [jnp.where(mask, s, NEG) for s in ss]
            return tuple(_softmax_step_t(carry, ss, [vt_ref[j, vrows[c // 2], :] for c in range(len(qs))]))

        carry = lax.fori_loop(0, qi, step, tuple(carry))
        carry = step(qi, carry, diag)
        for hh, sl in enumerate(heads):
            _, a1, _, a2 = carry[4 * hh:4 * hh + 4]
            o = a1[:HEAD_W] / a1[HEAD_W:HEAD_W + 1] - lam * (a2[:HEAD_W] / a2[HEAD_W:HEAD_W + 1])
            on = o * lax.rsqrt(jnp.mean(o * o, axis=0, keepdims=True) + EPS) * gst_ref[...] * coef
            o_ref[0, pl.ds(q0, tq), sl] = on.T.astype(BF16)
        return 0

    lax.fori_loop(0, sq // tq, q_tile, 0)


def _attention_prompt(q, k_main, v_main, k_ext, v_ext, n_ext, lam, g_subln, coef, tq, hp):
    b, sq, hw = q.shape
    w = hp * HEAD_W
    bh = pl.BlockSpec((1, sq, w), lambda i, h: (i, 0, h))
    ext = pl.BlockSpec((1, k_ext.shape[1], w), lambda i, h: (0, 0, h))
    gst = jnp.broadcast_to(g_subln.reshape(HEAD_W, 1), (HEAD_W, tq))
    return pl.pallas_call(
        functools.partial(_attn_prompt_kernel, sq=sq, tq=tq, hp=hp, n_ext=n_ext, coef=coef),
        grid=(b, hw // w),
        in_specs=[pl.BlockSpec(memory_space=pltpu.SMEM), bh, bh, bh, ext, ext,
                  pl.BlockSpec((HEAD_W, tq), lambda i, h: (0, 0))],
        out_specs=bh,
        out_shape=jax.ShapeDtypeStruct(q.shape, BF16),
        scratch_shapes=[pltpu.VMEM((sq, w), BF16), pltpu.VMEM((sq // tq, hp * VT_ROWS, tq), BF16)],
        compiler_params=_cparams(("arbitrary", "arbitrary")),
        name="attention_prompt",
    )(lam, q, k_main, v_main, k_ext, v_ext, gst)


def _attn_sample_kernel(lam_ref, q_ref, kc_ref, vc_ref, ke_ref, ve_ref, gs_ref, o_ref, m_ref, l_ref, acc_ref,
                        *, ds, nh, n_ext, coef):
    t = pl.program_id(1)
    tkv = kc_ref.shape[1] // nh
    lo = lax.broadcasted_iota(jnp.int32, (1, HEAD_W), 1) < DIFF_DH
    ext_ok = lax.broadcasted_iota(jnp.int32, (1, ke_ref.shape[1]), 1) < n_ext
    heads = [slice(hh * HEAD_W, (hh + 1) * HEAD_W) for hh in range(nh)]

    def stacked_q(sl):
        q = q_ref[0, :, sl]
        zero = jnp.zeros_like(q)
        return jnp.concatenate([jnp.where(lo, q, zero), jnp.where(lo, zero, q)], axis=0)

    qs = [stacked_q(sl) for sl in heads]

    def write_stats(stats):
        for hh in range(nh):
            m_ref[hh], l_ref[hh], acc_ref[hh] = stats[3 * hh:3 * hh + 3]

    @pl.when(t == 0)
    def _():
        ext_s = [jnp.where(ext_ok, _scores(qq, ke_ref[0, :, sl].astype(BF16)), NEG) for qq, sl in zip(qs, heads)]
        write_stats(_softmax_seed(ext_s, [ve_ref[0, :, sl].astype(BF16) for sl in heads]))

    carry = []
    for hh in range(nh):
        carry += [m_ref[hh], l_ref[hh], acc_ref[hh]]
    ss = [_scores(qq, kc_ref[0, pl.ds(hh, tkv, stride=nh), :].astype(BF16)) for hh, qq in enumerate(qs)]
    write_stats(_softmax_step(carry, ss, [vc_ref[0, pl.ds(hh, tkv, stride=nh), :].astype(BF16) for hh in range(nh)]))

    @pl.when(t == pl.num_programs(1) - 1)
    def _():
        lam = lam_ref[0]
        for hh, sl in enumerate(heads):
            o = acc_ref[hh] / l_ref[hh]
            o_ref[0, :, sl] = _sub_norm(o[:ds], o[ds:], lam, gs_ref[...], coef).astype(BF16)


def _attention_sample(q, k_cache, v_cache, k_ext, v_ext, n_ext, lam, g_subln, coef, tkv):
    b, ds, hw = q.shape
    _, pl_len, nh, _ = k_cache.shape
    per_b = lambda rows: pl.BlockSpec((1, rows, hw), lambda i, t: (i, 0, 0))
    cache = pl.BlockSpec((1, tkv * nh, HEAD_W), lambda i, t: (i, t, 0))
    return pl.pallas_call(
        functools.partial(_attn_sample_kernel, ds=ds, nh=nh, n_ext=n_ext, coef=coef),
        grid=(b, pl_len // tkv),
        in_specs=[pl.BlockSpec(memory_space=pltpu.SMEM), per_b(ds), cache, cache,
                  per_b(k_ext.shape[1]), per_b(k_ext.shape[1]),
                  pl.BlockSpec((1, HEAD_W), lambda i, t: (0, 0))],
        out_specs=per_b(ds),
        out_shape=jax.ShapeDtypeStruct(q.shape, BF16),
        scratch_shapes=[pltpu.VMEM((nh, 2 * ds, 1), F32), pltpu.VMEM((nh, 2 * ds, 1), F32),
                        pltpu.VMEM((nh, 2 * ds, HEAD_W), F32)],
        compiler_params=_cparams(("arbitrary", "arbitrary")),
        name="attention_sample",
    )(lam, q, k_cache.reshape(b, pl_len * nh, HEAD_W), v_cache.reshape(b, pl_len * nh, HEAD_W),
      k_ext, v_ext, g_subln)


def _mix_kernel(x_ref, yr_ref, od_ref, ga_ref, gb_ref, wr_ref, wd_ref, wo_ref, gf_ref, wrt_ref, br_ref,
                x1_ref, h_ref, lgt_ref):
    ya = jnp.dot(yr_ref[...], wr_ref[...], preferred_element_type=F32)
    yb = jnp.dot(od_ref[...], wd_ref[...], preferred_element_type=F32)
    z = jax.nn.sigmoid(ga_ref[...].astype(F32)) * ya + jax.nn.sigmoid(gb_ref[...].astype(F32)) * yb
    x1 = x_ref[...] + jnp.dot(z.astype(BF16), wo_ref[...], preferred_element_type=F32)
    x1_ref[...] = x1
    h = x1 * lax.rsqrt(jnp.mean(x1 * x1, axis=-1, keepdims=True) + EPS) * gf_ref[...]
    _tile_rows_store(h_ref, h)
    lgt_ref[...] = _scores(wrt_ref[...], h.astype(BF16)) + br_ref[...]


def _mix_tail_kernel(*refs):
    th_ref, tlgt_ref = refs[11:13]
    h_ref, lgt_ref = refs[14:16]
    last = pl.num_programs(0) - 1

    @pl.when(pl.program_id(0) < last)
    def _():
        _mix_kernel(*refs[:11], *refs[13:])

    @pl.when(pl.program_id(0) == last)
    def _():
        h_ref[...] = th_ref[...]
        lgt_ref[...] = tlgt_ref[...]


def _mix(x, yr, od, ga, gb, wr, wd, wo, gf, wrt, br, tm, tail=None):
    t, d = x.shape
    ne = wrt.shape[0]
    nt = t // tm
    extra = 0 if tail is None else 1
    row = lambda w: pl.BlockSpec((tm, w), lambda i: (jnp.minimum(i, nt - 1), 0))
    full = lambda a: pl.BlockSpec(a.shape, lambda i: (0, 0))
    ops = (x, yr, od, ga, gb, wr, wd, wo, gf, wrt, br) + (() if tail is None else tuple(tail))
    return pl.pallas_call(
        _mix_kernel if tail is None else _mix_tail_kernel,
        grid=(nt + extra,),
        in_specs=[row(d), row(yr.shape[1]), row(od.shape[1]), row(d), row(d)] + [full(a) for a in ops[5:]],
        out_specs=[row(d), pl.BlockSpec((tm * SUBLANES, LANES), lambda i: (i, 0)),
                   pl.BlockSpec((ne, tm), lambda i: (0, i))],
        out_shape=[jax.ShapeDtypeStruct((t, d), F32),
                   jax.ShapeDtypeStruct(((t + extra * tm) * SUBLANES, LANES), F32),
                   jax.ShapeDtypeStruct((ne, t + extra * tm), F32)],
        compiler_params=_cparams(("arbitrary",)),
        name="mix",
    )(*ops)


def _route_kernel(lgt_ref, e_ref, gate_ref, rank_ref, cnt_ref):
    @pl.when(pl.program_id(0) == 0)
    def _():
        cnt_ref[...] = jnp.zeros_like(cnt_ref)

    v = lgt_ref[...]
    ne, tn = v.shape
    eid = lax.broadcasted_iota(jnp.int32, (ne, tn), 0)
    tops, sels = [], []
    for k in range(TOP_K):
        m = jnp.max(v, axis=0, keepdims=True)
        idx = jnp.min(jnp.where(v == m, eid, ne), axis=0, keepdims=True)
        sel = eid == idx
        e_ref[k:k + 1, :] = idx
        tops.append(m)
        sels.append(sel)
        v = jnp.where(sel, -jnp.inf, v)
    ex = [jnp.exp(m - tops[0]) for m in tops]
    den = ex[0] + ex[1] + ex[2] + ex[3]
    for k in range(TOP_K):
        gate_ref[k:k + 1, :] = ex[k] / den

    chosen = sels[0] | sels[1] | sels[2] | sels[3]
    before = (lax.broadcasted_iota(jnp.int32, (tn, tn), 0) < lax.broadcasted_iota(jnp.int32, (tn, tn), 1))
    prior = jnp.dot(chosen.astype(BF16), before.astype(BF16), preferred_element_type=F32) + cnt_ref[:, 0:1]
    for k in range(TOP_K):
        rank_ref[k:k + 1, :] = jnp.sum(jnp.where(sels[k], prior, 0.0), axis=0, keepdims=True).astype(jnp.int32)
    cnt_ref[...] = cnt_ref[...] + jnp.sum(chosen.astype(F32), axis=1, keepdims=True)


def _route_topk(lgt, tn):
    ne, t = lgt.shape
    kt = pl.BlockSpec((TOP_K, tn), lambda i: (0, i))
    return pl.pallas_call(
        _route_kernel,
        grid=(t // tn,),
        in_specs=[pl.BlockSpec((ne, tn), lambda i: (0, i))],
        out_specs=[kt, kt, kt, pl.BlockSpec((ne, LANES), lambda i: (0, 0))],
        out_shape=[jax.ShapeDtypeStruct((TOP_K, t), jnp.int32), jax.ShapeDtypeStruct((TOP_K, t), F32),
                   jax.ShapeDtypeStruct((TOP_K, t), jnp.int32), jax.ShapeDtypeStruct((ne, LANES), F32)],
        compiler_params=_cparams(("arbitrary",)),
        name="route",
    )(lgt)


GATHER_UNROLL = 8


def _gather_rows(src_hbm, idx_ref, n, buf_ref, slot, sem):
    def body(g, _):
        for u in range(GATHER_UNROLL):
            r = g * GATHER_UNROLL + u
            src = pl.multiple_of(idx_ref[0, 0, r], SUBLANES)
            dst = pl.multiple_of(r * SUBLANES, SUBLANES)
            pltpu.make_async_copy(src_hbm.at[pl.ds(src, SUBLANES)], buf_ref.at[slot, pl.ds(dst, SUBLANES)],
                                  sem.at[slot]).start(priority=u % 2)
        return 0
    lax.fori_loop(0, n // GATHER_UNROLL, body, 0)


def _wait_rows(src_hbm, n, buf_ref, slot, sem):
    pltpu.make_async_copy(src_hbm.at[pl.ds(0, n * SUBLANES)], buf_ref.at[slot], sem.at[slot]).wait()


def _moe_kernel(be_ref, nu_ref, tok_ref, tokn_ref, h_hbm, wgu_ref, bgu_ref, wdn_ref, bdn_ref, out_ref,
                buf_ref, wgu_bf, wdn_bf, sem):
    i = pl.program_id(0)
    n_used = nu_ref[0]
    slot = i % 2
    d_ff = wdn_bf.shape[0]

    @pl.when(i == 0)
    def _():
        _gather_rows(h_hbm, tok_ref, MOE_ROWS, buf_ref, 0, sem)

    @pl.when(i + 1 < n_used)
    def _():
        _gather_rows(h_hbm, tokn_ref, MOE_ROWS, buf_ref, 1 - slot, sem)

    @pl.when(i < n_used)
    def _():
        @pl.when((i == 0) | (be_ref[i] != be_ref[jnp.maximum(i - 1, 0)]))
        def _():
            wgu_bf[...] = wgu_ref[0].astype(BF16)
            wdn_bf[...] = wdn_ref[0].astype(BF16)

        _wait_rows(h_hbm, MOE_ROWS, buf_ref, slot, sem)

        xb = _tile_rows_load(buf_ref, slot, 0, MOE_ROWS).astype(BF16)
        gu = jnp.dot(xb, wgu_bf[...], preferred_element_type=F32) + bgu_ref[0]
        gate = jnp.minimum(gu[:, :d_ff], SWIGLU_LIMIT)
        up = jnp.clip(gu[:, d_ff:], -SWIGLU_LIMIT, SWIGLU_LIMIT)
        act = (up + 1.0) * gate * jax.nn.sigmoid(SWIGLU_ALPHA * gate)
        _tile_rows_store(out_ref, jnp.dot(act.astype(BF16), wdn_bf[...], preferred_element_type=F32) + bdn_ref[0])

    @pl.when(i >= n_used)
    def _():
        out_ref[...] = jnp.zeros_like(out_ref)


def _moe(h, tok, block_e, n_used, w_gu, b_gu, w_down, b_down):
    d = h.shape[1] * SUBLANES
    nb = tok.shape[0]
    ne, _, two_ff = w_gu.shape
    d_ff = two_ff // 2
    e_map = lambda i, be, nu: (be[i], 0, 0)
    grid_spec = pltpu.PrefetchScalarGridSpec(
        num_scalar_prefetch=2,
        grid=(nb,),
        in_specs=[
            pl.BlockSpec((1, 1, MOE_ROWS), lambda i, be, nu: (i, 0, 0), memory_space=pltpu.SMEM),
            pl.BlockSpec((1, 1, MOE_ROWS), lambda i, be, nu: (jnp.minimum(i + 1, nb - 1), 0, 0),
                         memory_space=pltpu.SMEM),
            pl.BlockSpec(memory_space=pl.ANY),
            pl.BlockSpec((1, d, two_ff), e_map),
            pl.BlockSpec((1, 1, two_ff), e_map),
            pl.BlockSpec((1, d_ff, d), e_map),
            pl.BlockSpec((1, 1, d), e_map),
        ],
        out_specs=pl.BlockSpec((MOE_ROWS * SUBLANES, LANES), lambda i, be, nu: (i, 0)),
        scratch_shapes=[pltpu.VMEM((2, MOE_ROWS * SUBLANES, LANES), F32),
                        pltpu.VMEM((d, two_ff), BF16),
                        pltpu.VMEM((d_ff, d), BF16),
                        pltpu.SemaphoreType.DMA((2,))],
    )
    return pl.pallas_call(
        _moe_kernel,
        grid_spec=grid_spec,
        out_shape=jax.ShapeDtypeStruct((nb * MOE_ROWS * SUBLANES, LANES), F32),
        compiler_params=_cparams(("arbitrary",)),
        name="moe",
    )(block_e, n_used, tok, tok, h, w_gu, b_gu.reshape(ne, 1, two_ff), w_down, b_down.reshape(ne, 1, d))


def _combine_kernel(dst_ref, dstn_ref, x_ref, gate_ref, gf_ref, rows_hbm, y_ref, buf_ref, sem, *, tm, nsteps):
    i = pl.program_id(0)
    slot = i % 2
    n = tm * TOP_K

    @pl.when(i == 0)
    def _():
        _gather_rows(rows_hbm, dst_ref, n, buf_ref, 0, sem)

    @pl.when(i + 1 < nsteps)
    def _():
        _gather_rows(rows_hbm, dstn_ref, n, buf_ref, 1 - slot, sem)

    _wait_rows(rows_hbm, n, buf_ref, slot, sem)

    y = x_ref[...]
    gate = gate_ref[...]
    for k in range(TOP_K):
        y = y + gate[:, k:k + 1] * _tile_rows_load(buf_ref, slot, k * tm, tm)
    y_ref[...] = y * lax.rsqrt(jnp.mean(y * y, axis=-1, keepdims=True) + EPS) * gf_ref[...]


def _combine(x1, gates, dest, rows, g_final, tm):
    t, d = x1.shape
    nsteps = t // tm
    n = tm * TOP_K
    return pl.pallas_call(
        functools.partial(_combine_kernel, tm=tm, nsteps=nsteps),
        grid=(nsteps,),
        in_specs=[pl.BlockSpec((1, 1, n), lambda i: (i, 0, 0), memory_space=pltpu.SMEM),
                  pl.BlockSpec((1, 1, n), lambda i: (jnp.minimum(i + 1, nsteps - 1), 0, 0),
                               memory_space=pltpu.SMEM),
                  pl.BlockSpec((tm, d), lambda i: (i, 0)),
                  pl.BlockSpec((tm, TOP_K), lambda i: (i, 0)),
                  pl.BlockSpec((1, d), lambda i: (0, 0)),
                  pl.BlockSpec(memory_space=pl.ANY)],
        out_specs=pl.BlockSpec((tm, d), lambda i: (i, 0)),
        out_shape=jax.ShapeDtypeStruct((t, d), F32),
        scratch_shapes=[pltpu.VMEM((2, n * SUBLANES, LANES), F32), pltpu.SemaphoreType.DMA((2,))],
        compiler_params=_cparams(("arbitrary",)),
        name="combine",
    )(dest, dest, x1, gates, g_final, rows)


def _route(lgt, tn):
    t = lgt.shape[1]
    n_assign = t * TOP_K
    e4, gates, rank, cnt = _route_topk(lgt, tn)
    counts = cnt[:, 0].astype(jnp.int32)
    padded = (counts + MOE_ROWS - 1) // MOE_ROWS * MOE_ROWS
    pend = jnp.cumsum(padded)
    pstart = pend - padded
    start = jnp.cumsum(counts) - counts
    e_ids = jnp.arange(N_EXPERTS, dtype=jnp.int32)
    dest = rank + jnp.sum(jnp.where(e4[None] == e_ids[:, None, None], pstart[:, None, None], 0), axis=0)
    nb = n_assign // MOE_ROWS + N_EXPERTS
    n_used = (pend[-1] // MOE_ROWS).astype(jnp.int32).reshape(1)
    blk0 = jnp.arange(nb, dtype=jnp.int32) * MOE_ROWS
    block_e = jnp.minimum(jnp.sum((pend[None, :] <= blk0[:, None]).astype(jnp.int32), axis=1), N_EXPERTS - 1)
    of_block = lambda table: jnp.sum(jnp.where(block_e[:, None] == e_ids[None, :], table[None, :], 0), axis=1)
    order = jnp.argsort(e4.T.reshape(-1))
    within = (blk0 - of_block(pstart))[:, None] + jnp.arange(MOE_ROWS, dtype=jnp.int32)[None, :]
    src = jnp.clip(of_block(start)[:, None] + jnp.minimum(within, of_block(counts)[:, None] - 1), 0, n_assign - 1)
    tok = (order[src] // TOP_K).astype(jnp.int32)
    return gates.T, dest * SUBLANES, (tok * SUBLANES).reshape(nb, 1, MOE_ROWS), block_e, n_used


def kernel(x_prompt, x_sample, cache_k, cache_v, state_ret, meta_tokens, g_mix, w_in,
           lam_q1, lam_k1, lam_q2, lam_k2, g_subln, w_ret_out, w_diff_out, w_out,
           g_ffn, w_router, b_router, w_gu, b_gu, w_down, b_down, g_final):
    b, s, d = x_prompt.shape
    db, ds, _ = x_sample.shape
    depth, _, pl_len = cache_k.shape[:3]
    assert depth == 1, "single-layer step only"
    nm = meta_tokens.shape[0]
    hw = DIFF_HEADS * HEAD_W

    lam_init = 0.8 - 0.6 * math.exp(-0.3 * 0)
    lam = (jnp.exp(jnp.sum(lam_q1[0] * lam_k1[0])) - jnp.exp(jnp.sum(lam_q2[0] * lam_k2[0]))
           + lam_init).reshape(1).astype(F32)
    coef = 1.0 - lam_init

    w_in_bf = w_in[0].astype(BF16)
    g_mix2 = g_mix[0].reshape(1, d)

    tm = 512 if (b * s) % 512 == 0 else s
    tab_p = _tables(N_META + jnp.arange(s, dtype=jnp.int32))
    rq, rk, rv, rg, dq, dk, dv, ga, gb = _proj(x_prompt.reshape(b * s, d), g_mix2, w_in_bf, tab_p, tm, s // tm)

    pos_small = jnp.concatenate([jnp.tile(N_META + pl_len + jnp.arange(ds, dtype=jnp.int32), db),
                                 jnp.arange(nm, dtype=jnp.int32)])
    x_small = jnp.concatenate([x_sample.reshape(db * ds, d), meta_tokens.astype(F32)], axis=0)
    small = _proj(x_small, g_mix2, w_in_bf, _tables(pos_small), x_small.shape[0], 1)
    ns = db * ds
    srq, srk, srv, srg, sdq, sdk, sdv, sga, sgb = [a[:ns] for a in small]
    mrq, mrk, mrv, mrg, _, mdk, mdv, _, _ = [a[ns:] for a in small]

    cb_small = 128

    def pad_rows(a, nb_, n_):
        a = a.reshape(nb_, n_, a.shape[-1])
        return jnp.pad(a, ((0, 0), (0, cb_small - n_), (0, 0)))

    zero_state = jnp.zeros((1, RET_HEADS, RET_DK, RET_DV), F32)
    _, s_meta = _retention(pad_rows(mrq, 1, nm), pad_rows(mrk, 1, nm), pad_rows(mrv, 1, nm),
                           pad_rows(mrg, 1, nm), zero_state, cb_small, nm)
    cb = 256
    r3 = lambda a: a.reshape(b, s, a.shape[-1])
    yr, ret_p = _retention(r3(rq), r3(rk), r3(rv), r3(rg), s_meta, cb, cb)
    syr, ret_s = _retention(pad_rows(srq, db, ds), pad_rows(srk, db, ds), pad_rows(srv, db, ds),
                            pad_rows(srg, db, ds), state_ret[0], cb_small, ds)
    syr = syr[:, :ds].reshape(ns, -1)

    g_sub = g_subln[0].reshape(1, HEAD_W)
    ext_rows = 128
    pad_ext = lambda a: jnp.pad(a, ((0, 0), (0, ext_rows - a.shape[1]), (0, 0)))
    od = _attention_prompt(r3(dq), r3(dk), r3(dv), pad_ext(mdk[None]), pad_ext(mdv[None]), nm,
                           lam, g_sub, coef, 256, 4)
    sdk3, sdv3 = sdk.reshape(db, ds, hw), sdv.reshape(db, ds, hw)
    ke_s = jnp.concatenate([jnp.broadcast_to(mdk[None], (db, nm, hw)), sdk3], axis=1)
    ve_s = jnp.concatenate([jnp.broadcast_to(mdv[None], (db, nm, hw)), sdv3], axis=1)
    sod = _attention_sample(sdq.reshape(db, ds, hw), cache_k[0], cache_v[0], pad_ext(ke_s), pad_ext(ve_s),
                            nm + ds, lam, g_sub, coef, 512)

    wr, wd, wo = w_ret_out[0].astype(BF16), w_diff_out[0].astype(BF16), w_out[0].astype(BF16)
    gf = g_ffn[0].reshape(1, d)
    wrt = w_router[0].T.astype(BF16)
    br = b_router[0].astype(F32).reshape(N_EXPERTS, 1)
    sx1, sh, slgt = _mix(x_sample.reshape(ns, d), syr, sod.reshape(ns, hw), sga, sgb, wr, wd, wo, gf, wrt, br, ns)
    x1, h, lgt = _mix(x_prompt.reshape(b * s, d), yr.reshape(b * s, -1), od.reshape(b * s, hw), ga, gb,
                      wr, wd, wo, gf, wrt, br, ns, (sh, slgt))

    t = b * s
    gates, dest, tok, block_e, n_used = _route(lgt, ns)
    rows = _moe(h, tok, block_e, n_used, w_gu[0], b_gu[0], w_down[0], b_down[0])
    gfin = g_final.reshape(1, d)

    def combine(x1g, lo, n):
        tc = min(256, n)
        dg = dest[:, lo:lo + n].reshape(TOP_K, n // tc, tc).swapaxes(0, 1).reshape(n // tc, 1, TOP_K * tc)
        return _combine(x1g, gates[lo:lo + n], dg, rows, gfin, tc)

    y_p = combine(x1, 0, t)
    y_s = combine(sx1, t, ns)

    k_p = jnp.concatenate([jnp.broadcast_to(mdk[None], (b, nm, hw)), r3(dk)], axis=1)
    v_p = jnp.concatenate([jnp.broadcast_to(mdv[None], (b, nm, hw)), r3(dv)], axis=1)
    shp = lambda a: a.reshape(1, a.shape[0], a.shape[1], DIFF_HEADS, HEAD_W)
    return (y_p.reshape(b, s, d), y_s.reshape(db, ds, d), ret_p[None], shp(k_p), shp(v_p),
            ret_s[None], shp(sdk3), shp(sdv3))
```

```python
import functools
import math

import jax
import jax.numpy as jnp
from jax import lax
from jax.experimental import pallas as pl
from jax.experimental.pallas import tpu as pltpu

F32 = jnp.float32
BF16 = jnp.bfloat16

EPS = 1e-6
N_META = 16
CHUNK = 64
RET_HEADS = 4
RET_DK = 128
RET_DV = 256
RET_THETA = 10000.0
DIFF_HEADS = 8
DIFF_DH = 64
ROT_DIM = DIFF_DH // 4
ROPE_THETA = 500000.0
N_EXPERTS = 32
TOP_K = 4
SWIGLU_LIMIT = 7.0
SWIGLU_ALPHA = 1.702

LANES = 128
SUBLANES = 8
HEAD_W = 2 * DIFF_DH
MOE_ROWS = 256
VMEM_LIMIT = 56 * 1024 * 1024
NEG = -1e30
Q_SCALE = DIFF_DH ** -0.5 * math.log2(math.e)


def _cparams(sem):
    return pltpu.CompilerParams(dimension_semantics=sem, vmem_limit_bytes=VMEM_LIMIT)


def _tile_rows_store(ref, val):
    n = val.shape[0]
    for c in range(SUBLANES):
        ref[pl.ds(c, n, stride=SUBLANES), :] = val[:, c * LANES:(c + 1) * LANES]


def _tile_rows_load(ref, slot, first, n):
    return jnp.concatenate([ref[slot, pl.ds(first * SUBLANES + c, n, stride=SUBLANES), :]
                            for c in range(SUBLANES)], axis=1)


def _ret_tables(pos):
    angle = RET_THETA ** (-jnp.linspace(0.0, 1.0, RET_DK // 2, dtype=F32))
    angle = jnp.repeat(angle, 2)
    ang = pos.astype(F32)[:, None] * angle[None, :]
    cos, sin = jnp.cos(ang), jnp.sin(ang)
    even = (jnp.arange(RET_DK) % 2 == 0)[None, :]
    return cos, jnp.where(even, -sin, 0.0), jnp.where(even, 0.0, sin)


def _rope_tables(pos):
    half = ROT_DIM // 2
    inv = ROPE_THETA ** (-jnp.arange(half, dtype=F32) * (2.0 / ROT_DIM))
    ang = pos.astype(F32)[:, None] * inv[None, :]
    cos, sin = jnp.cos(ang), jnp.sin(ang)
    jj = jnp.arange(HEAD_W) % DIFF_DH
    first = (jj < half)[None, :]
    second = ((jj >= half) & (jj < ROT_DIM))[None, :]
    cos_l = jnp.take(cos, jj % half, axis=1)
    sin_l = jnp.take(sin, jj % half, axis=1)
    c = jnp.where(first | second, cos_l, 1.0)
    sa = jnp.where(first, -sin_l, 0.0)
    sb = jnp.where(second, sin_l, 0.0)
    return c, sa, sb


def _tables(pos):
    return jnp.concatenate(_ret_tables(pos) + _rope_tables(pos), axis=1)


def _proj_kernel(x_ref, g_ref, w_ref, tab_ref, rq_ref, rk_ref, rv_ref, rg_ref,
                 dq_ref, dk_ref, dv_ref, ga_ref, gb_ref):
    x = x_ref[...]
    ms = jnp.mean(x * x, axis=-1, keepdims=True)
    h = (x * lax.rsqrt(ms + EPS) * g_ref[...]).astype(BF16)

    def mm(c0, width):
        return jnp.dot(h, w_ref[:, c0:c0 + width], preferred_element_type=F32)

    def rot(p, t0, near, far):
        c = tab_ref[:, t0:t0 + LANES]
        sa = tab_ref[:, t0 + LANES:t0 + 2 * LANES]
        sb = tab_ref[:, t0 + 2 * LANES:t0 + 3 * LANES]
        return p * c + pltpu.roll(p, LANES - near, 1) * sa + pltpu.roll(p, far, 1) * sb

    kw = RET_HEADS * RET_DK
    vw = RET_HEADS * RET_DV
    dw = DIFF_HEADS * HEAD_W
    c0 = 0
    p = mm(c0, kw)
    for hh in range(RET_HEADS):
        sl = slice(hh * LANES, (hh + 1) * LANES)
        rq_ref[:, sl] = rot(p[:, sl], 0, 1, 1).astype(BF16)
    c0 += kw
    p = mm(c0, kw)
    for hh in range(RET_HEADS):
        sl = slice(hh * LANES, (hh + 1) * LANES)
        rk_ref[:, sl] = (rot(p[:, sl], 0, 1, 1) * (RET_DK ** -0.5)).astype(BF16)
    c0 += kw
    rv_ref[...] = mm(c0, vw).astype(BF16)
    c0 += vw
    rg_ref[...] = mm(c0, vw).astype(BF16)
    c0 += vw
    half = ROT_DIM // 2
    p = mm(c0, dw)
    for hh in range(DIFF_HEADS):
        sl = slice(hh * LANES, (hh + 1) * LANES)
        dq_ref[:, sl] = (rot(p[:, sl], 3 * LANES, half, half) * Q_SCALE).astype(BF16)
    c0 += dw
    p = mm(c0, dw)
    for hh in range(DIFF_HEADS):
        sl = slice(hh * LANES, (hh + 1) * LANES)
        dk_ref[:, sl] = rot(p[:, sl], 3 * LANES, half, half)
    c0 += dw
    dv_ref[...] = mm(c0, dw)
    c0 += dw
    d = x.shape[1]
    ga_ref[...] = mm(c0, d).astype(BF16)
    c0 += d
    gb_ref[...] = mm(c0, d).astype(BF16)


def _proj(x, g, w_bf, tab, tm, tab_blocks):
    t, d = x.shape
    kw, vw, dw = RET_HEADS * RET_DK, RET_HEADS * RET_DV, DIFF_HEADS * HEAD_W
    widths = (kw, kw, vw, vw, dw, dw, dw, d, d)
    dtypes = (BF16, BF16, BF16, BF16, BF16, F32, F32, BF16, BF16)
    row = lambda i: (i, 0)
    const = lambda i: (0, 0)
    return pl.pallas_call(
        _proj_kernel,
        grid=(t // tm,),
        in_specs=[pl.BlockSpec((tm, d), row),
                  pl.BlockSpec((1, d), const),
                  pl.BlockSpec(w_bf.shape, const),
                  pl.BlockSpec((tm, tab.shape[1]), lambda i: (i % tab_blocks, 0))],
        out_specs=[pl.BlockSpec((tm, w), row) for w in widths],
        out_shape=[jax.ShapeDtypeStruct((t, w), dt) for w, dt in zip(widths, dtypes)],
        compiler_params=_cparams(("arbitrary",)),
        name="proj",
    )(x, g, w_bf, tab)


def _ret_log_gammas():
    return tuple(math.log(1.0 - 2.0 ** (-5.0 - hh)) for hh in range(RET_HEADS))


def _ret_kernel(q_ref, k_ref, v_ref, g_ref, s0_ref, y_ref, st_ref, *, cb, n_valid):
    @pl.when(pl.program_id(1) == 0)
    def _():
        st_ref[...] = s0_ref[...]

    row = lax.broadcasted_iota(jnp.int32, (cb, cb), 0)
    col = lax.broadcasted_iota(jnp.int32, (cb, cb), 1)
    dist = (row - col).astype(F32)
    idx = lax.broadcasted_iota(jnp.int32, (cb, 1), 0).astype(F32)
    for hh, lg in enumerate(_ret_log_gammas()):
        q = q_ref[0, :, hh * RET_DK:(hh + 1) * RET_DK]
        k = k_ref[0, :, hh * RET_DK:(hh + 1) * RET_DK]
        v = v_ref[0, :, hh * RET_DV:(hh + 1) * RET_DV]
        g = g_ref[0, :, hh * RET_DV:(hh + 1) * RET_DV].astype(F32)
        st = st_ref[0, hh]
        decay = jnp.where(dist >= 0, jnp.exp(lg * jnp.maximum(dist, 0.0)), 0.0)
        s = lax.dot_general(q, k, (((1,), (1,)), ((), ())), preferred_element_type=F32) * decay
        o = jnp.dot(s.astype(BF16), v, preferred_element_type=F32)
        o = o + jnp.dot(q, st.astype(BF16), preferred_element_type=F32) * jnp.exp(lg * (idx + 1.0))
        kw = (k.astype(F32) * jnp.exp(lg * (n_valid - 1.0 - idx))).astype(BF16)
        st_ref[0, hh] = math.exp(lg * n_valid) * st + lax.dot_general(
            kw, v, (((0,), (0,)), ((), ())), preferred_element_type=F32)
        on = o * lax.rsqrt(jnp.mean(o * o, axis=-1, keepdims=True) + EPS)
        y_ref[0, :, hh * RET_DV:(hh + 1) * RET_DV] = (g * jax.nn.sigmoid(g) * on).astype(BF16)


def _retention(q, k, v, g, s0, cb, n_valid):
    b, s, _ = q.shape
    s0_map = (lambda i, c: (i, 0, 0, 0)) if s0.shape[0] == b else (lambda i, c: (0, 0, 0, 0))
    blk = lambda w: pl.BlockSpec((1, cb, w), lambda i, c: (i, c, 0))
    st_block = (1, RET_HEADS, RET_DK, RET_DV)
    return pl.pallas_call(
        functools.partial(_ret_kernel, cb=cb, n_valid=n_valid),
        grid=(b, s // cb),
        in_specs=[blk(q.shape[2]), blk(k.shape[2]), blk(v.shape[2]), blk(g.shape[2]),
                  pl.BlockSpec(st_block, s0_map)],
        out_specs=[blk(v.shape[2]), pl.BlockSpec(st_block, lambda i, c: (i, 0, 0, 0))],
        out_shape=[jax.ShapeDtypeStruct(v.shape, BF16),
                   jax.ShapeDtypeStruct((b,) + st_block[1:], F32)],
        compiler_params=_cparams(("arbitrary", "arbitrary")),
        name="retention",
    )(q, k, v, g, s0)


def _scores(qq, kk):
    return lax.dot_general(qq, kk, (((1,), (1,)), ((), ())), preferred_element_type=F32)


def _softmax_seed(scores, values):
    ms = [jnp.max(s, axis=-1, keepdims=True) for s in scores]
    ps = [jnp.exp2(s - m) for s, m in zip(scores, ms)]
    out = []
    for m, p, vv in zip(ms, ps, values):
        out += [m, jnp.sum(p, axis=-1, keepdims=True),
                jnp.dot(p.astype(BF16), vv, preferred_element_type=F32)]
    return out


def _softmax_step(carry, scores, values):
    m2s = [jnp.maximum(carry[3 * c], jnp.max(s, axis=-1, keepdims=True)) for c, s in enumerate(scores)]
    ps = [jnp.exp2(s - m2) for s, m2 in zip(scores, m2s)]
    out = []
    for c, (m2, p, vv) in enumerate(zip(m2s, ps, values)):
        m, l, acc = carry[3 * c:3 * c + 3]
        a = jnp.exp2(m - m2)
        out += [m2, a * l + jnp.sum(p, axis=-1, keepdims=True),
                a * acc + jnp.dot(p.astype(BF16), vv, preferred_element_type=F32)]
    return out


def _sub_norm(o1, o2, lam, gs, coef):
    o = o1 - lam * o2
    return o * lax.rsqrt(jnp.mean(o * o, axis=-1, keepdims=True) + EPS) * gs * coef


SUM_ROWS = 16
VT_ROWS = HEAD_W + SUM_ROWS


def _softmax_seed_t(scores_t, values_t):
    ms = [jnp.max(s, axis=0, keepdims=True) for s in scores_t]
    ps = [jnp.exp2(s - m) for s, m in zip(scores_t, ms)]
    out = []
    for m, p, vt in zip(ms, ps, values_t):
        out += [m, jnp.dot(vt, p.astype(BF16), preferred_element_type=F32)]
    return out


def _softmax_step_t(carry, scores_t, values_t):
    m2s = [jnp.maximum(carry[2 * c], jnp.max(s, axis=0, keepdims=True)) for c, s in enumerate(scores_t)]
    ps = [jnp.exp2(s - m2) for s, m2 in zip(scores_t, m2s)]
    out = []
    for c, (m2, p, vt) in enumerate(zip(m2s, ps, values_t)):
        m, acc = carry[2 * c:2 * c + 2]
        out += [m2, jnp.exp2(m - m2) * acc + jnp.dot(vt, p.astype(BF16), preferred_element_type=F32)]
    return out


def _attn_prompt_kernel(lam_ref, q_ref, km_ref, vm_ref, ke_ref, ve_ref, gst_ref, o_ref, kb_ref, vt_ref,
                        *, sq, tq, hp, n_ext, coef):
    heads = [slice(hh * HEAD_W, (hh + 1) * HEAD_W) for hh in range(hp)]
    vrows = [slice(hh * VT_ROWS, (hh + 1) * VT_ROWS) for hh in range(hp)]
    kb_ref[...] = km_ref[0].astype(BF16)
    ones = jnp.ones((SUM_ROWS, tq), BF16)
    for j in range(sq // tq):
        for hh, sl in enumerate(heads):
            vt_ref[j, hh * VT_ROWS:hh * VT_ROWS + HEAD_W, :] = vm_ref[0, j * tq:(j + 1) * tq, sl].T.astype(BF16)
            vt_ref[j, hh * VT_ROWS + HEAD_W:(hh + 1) * VT_ROWS, :] = ones
    ke = [ke_ref[0, :, sl].astype(BF16) for sl in heads]
    vet = [jnp.concatenate([ve_ref[0, :, sl].T.astype(BF16), jnp.ones((SUM_ROWS, ve_ref.shape[1]), BF16)], axis=0)
           for sl in heads]
    lam = lam_ref[0]
    lo = lax.broadcasted_iota(jnp.int32, (1, HEAD_W), 1) < DIFF_DH
    ext_ok = lax.broadcasted_iota(jnp.int32, (ke_ref.shape[1], 1), 0) < n_ext
    diag = (lax.broadcasted_iota(jnp.int32, (tq, tq), 0) // CHUNK
            <= lax.broadcasted_iota(jnp.int32, (tq, tq), 1) // CHUNK)

    def q_tile(qi, _):
        q0 = pl.multiple_of(qi * tq, tq)
        qs = []
        for sl in heads:
            q = q_ref[0, pl.ds(q0, tq), sl]
            zero = jnp.zeros_like(q)
            qs += [jnp.where(lo, q, zero), jnp.where(lo, zero, q)]

        ext_s = [jnp.where(ext_ok, _scores(ke[c // 2], qq), NEG) for c, qq in enumerate(qs)]
        carry = _softmax_seed_t(ext_s, [vet[c // 2] for c in range(len(qs))])

        def step(j, carry, mask=None):
            k0 = pl.multiple_of(j * tq, tq)
            ss = [_scores(kb_ref[pl.ds(k0, tq), heads[c // 2]], qq) for c, qq in enumerate(qs)]
            if mask is not None:
                ss = [jnp.where(mask, s, NEG) for s in ss]
            return tuple(_softmax_step_t(carry, ss, [vt_ref[j, vrows[c // 2], :] for c in range(len(qs))]))

        carry = lax.fori_loop(0, qi, step, tuple(carry))
        carry = step(qi, carry, diag)
        for hh, sl in enumerate(heads):
            _, a1, _, a2 = carry[4 * hh:4 * hh + 4]
            o = a1[:HEAD_W] / a1[HEAD_W:HEAD_W + 1] - lam * (a2[:HEAD_W] / a2[HEAD_W:HEAD_W + 1])
            on = o * lax.rsqrt(jnp.mean(o * o, axis=0, keepdims=True) + EPS) * gst_ref[...] * coef
            o_ref[0, pl.ds(q0, tq), sl] = on.T.astype(BF16)
        return 0

    lax.fori_loop(0, sq // tq, q_tile, 0)


def _attention_prompt(q, k_main, v_main, k_ext, v_ext, n_ext, lam, g_subln, coef, tq, hp):
    b, sq, hw = q.shape
    w = hp * HEAD_W
    bh = pl.BlockSpec((1, sq, w), lambda i, h: (i, 0, h))
    ext = pl.BlockSpec((1, k_ext.shape[1], w), lambda i, h: (0, 0, h))
    gst = jnp.broadcast_to(g_subln.reshape(HEAD_W, 1), (HEAD_W, tq))
    return pl.pallas_call(
        functools.partial(_attn_prompt_kernel, sq=sq, tq=tq, hp=hp, n_ext=n_ext, coef=coef),
        grid=(b, hw // w),
        in_specs=[pl.BlockSpec(memory_space=pltpu.SMEM), bh, bh, bh, ext, ext,
                  pl.BlockSpec((HEAD_W, tq), lambda i, h: (0, 0))],
        out_specs=bh,
        out_shape=jax.ShapeDtypeStruct(q.shape, BF16),
        scratch_shapes=[pltpu.VMEM((sq, w), BF16), pltpu.VMEM((sq // tq, hp * VT_ROWS, tq), BF16)],
        compiler_params=_cparams(("arbitrary", "arbitrary")),
        name="attention_prompt",
    )(lam, q, k_main, v_main, k_ext, v_ext, gst)


def _attn_sample_kernel(lam_ref, q_ref, kc_ref, vc_ref, ke_ref, ve_ref, gs_ref, o_ref, m_ref, l_ref, acc_ref,
                        *, ds, nh, n_ext, coef):
    t = pl.program_id(1)
    tkv = kc_ref.shape[1] // nh
    lo = lax.broadcasted_iota(jnp.int32, (1, HEAD_W), 1) < DIFF_DH
    ext_ok = lax.broadcasted_iota(jnp.int32, (1, ke_ref.shape[1]), 1) < n_ext
    heads = [slice(hh * HEAD_W, (hh + 1) * HEAD_W) for hh in range(nh)]

    def stacked_q(sl):
        q = q_ref[0, :, sl]
        zero = jnp.zeros_like(q)
        return jnp.concatenate([jnp.where(lo, q, zero), jnp.where(lo, zero, q)], axis=0)

    qs = [stacked_q(sl) for sl in heads]

    def write_stats(stats):
        for hh in range(nh):
            m_ref[hh], l_ref[hh], acc_ref[hh] = stats[3 * hh:3 * hh + 3]

    @pl.when(t == 0)
    def _():
        ext_s = [jnp.where(ext_ok, _scores(qq, ke_ref[0, :, sl].astype(BF16)), NEG) for qq, sl in zip(qs, heads)]
        write_stats(_softmax_seed(ext_s, [ve_ref[0, :, sl].astype(BF16) for sl in heads]))

    carry = []
    for hh in range(nh):
        carry += [m_ref[hh], l_ref[hh], acc_ref[hh]]
    ss = [_scores(qq, kc_ref[0, pl.ds(hh, tkv, stride=nh), :].astype(BF16)) for hh, qq in enumerate(qs)]
    write_stats(_softmax_step(carry, ss, [vc_ref[0, pl.ds(hh, tkv, stride=nh), :].astype(BF16) for hh in range(nh)]))

    @pl.when(t == pl.num_programs(1) - 1)
    def _():
        lam = lam_ref[0]
        for hh, sl in enumerate(heads):
            o = acc_ref[hh] / l_ref[hh]
            o_ref[0, :, sl] = _sub_norm(o[:ds], o[ds:], lam, gs_ref[...], coef).astype(BF16)


def _attention_sample(q, k_cache, v_cache, k_ext, v_ext, n_ext, lam, g_subln, coef, tkv):
    b, ds, hw = q.shape
    _, pl_len, nh, _ = k_cache.shape
    per_b = lambda rows: pl.BlockSpec((1, rows, hw), lambda i, t: (i, 0, 0))
    cache = pl.BlockSpec((1, tkv * nh, HEAD_W), lambda i, t: (i, t, 0))
    return pl.pallas_call(
        functools.partial(_attn_sample_kernel, ds=ds, nh=nh, n_ext=n_ext, coef=coef),
        grid=(b, pl_len // tkv),
        in_specs=[pl.BlockSpec(memory_space=pltpu.SMEM), per_b(ds), cache, cache,
                  per_b(k_ext.shape[1]), per_b(k_ext.shape[1]),
                  pl.BlockSpec((1, HEAD_W), lambda i, t: (0, 0))],
        out_specs=per_b(ds),
        out_shape=jax.ShapeDtypeStruct(q.shape, BF16),
        scratch_shapes=[pltpu.VMEM((nh, 2 * ds, 1), F32), pltpu.VMEM((nh, 2 * ds, 1), F32),
                        pltpu.VMEM((nh, 2 * ds, HEAD_W), F32)],
        compiler_params=_cparams(("arbitrary", "arbitrary")),
        name="attention_sample",
    )(lam, q, k_cache.reshape(b, pl_len * nh, HEAD_W), v_cache.reshape(b, pl_len * nh, HEAD_W),
      k_ext, v_ext, g_subln)


def _mix_kernel(x_ref, yr_ref, od_ref, ga_ref, gb_ref, wr_ref, wd_ref, wo_ref, gf_ref, wrt_ref, br_ref,
                x1_ref, h_ref, lgt_ref):
    ya = jnp.dot(yr_ref[...], wr_ref[...], preferred_element_type=F32)
    yb = jnp.dot(od_ref[...], wd_ref[...], preferred_element_type=F32)
    z = jax.nn.sigmoid(ga_ref[...].astype(F32)) * ya + jax.nn.sigmoid(gb_ref[...].astype(F32)) * yb
    x1 = x_ref[...] + jnp.dot(z.astype(BF16), wo_ref[...], preferred_element_type=F32)
    x1_ref[...] = x1
    h = x1 * lax.rsqrt(jnp.mean(x1 * x1, axis=-1, keepdims=True) + EPS) * gf_ref[...]
    _tile_rows_store(h_ref, h)
    lgt_ref[...] = _scores(wrt_ref[...], h.astype(BF16)) + br_ref[...]


def _mix_tail_kernel(*refs):
    th_ref, tlgt_ref = refs[11:13]
    h_ref, lgt_ref = refs[14:16]
    last = pl.num_programs(0) - 1

    @pl.when(pl.program_id(0) < last)
    def _():
        _mix_kernel(*refs[:11], *refs[13:])

    @pl.when(pl.program_id(0) == last)
    def _():
        h_ref[...] = th_ref[...]
        lgt_ref[...] = tlgt_ref[...]


def _mix(x, yr, od, ga, gb, wr, wd, wo, gf, wrt, br, tm, tail=None):
    t, d = x.shape
    ne = wrt.shape[0]
    nt = t // tm
    extra = 0 if tail is None else 1
    row = lambda w: pl.BlockSpec((tm, w), lambda i: (jnp.minimum(i, nt - 1), 0))
    full = lambda a: pl.BlockSpec(a.shape, lambda i: (0, 0))
    ops = (x, yr, od, ga, gb, wr, wd, wo, gf, wrt, br) + (() if tail is None else tuple(tail))
    return pl.pallas_call(
        _mix_kernel if tail is None else _mix_tail_kernel,
        grid=(nt + extra,),
        in_specs=[row(d), row(yr.shape[1]), row(od.shape[1]), row(d), row(d)] + [full(a) for a in ops[5:]],
        out_specs=[row(d), pl.BlockSpec((tm * SUBLANES, LANES), lambda i: (i, 0)),
                   pl.BlockSpec((ne, tm), lambda i: (0, i))],
        out_shape=[jax.ShapeDtypeStruct((t, d), F32),
                   jax.ShapeDtypeStruct(((t + extra * tm) * SUBLANES, LANES), F32),
                   jax.ShapeDtypeStruct((ne, t + extra * tm), F32)],
        compiler_params=_cparams(("arbitrary",)),
        name="mix",
    )(*ops)


def _route_kernel(lgt_ref, e_ref, gate_ref, rank_ref, cnt_ref):
    @pl.when(pl.program_id(0) == 0)
    def _():
        cnt_ref[...] = jnp.zeros_like(cnt_ref)

    v = lgt_ref[...]
    ne, tn = v.shape
    eid = lax.broadcasted_iota(jnp.int32, (ne, tn), 0)
    tops, sels = [], []
    for k in range(TOP_K):
        m = jnp.max(v, axis=0, keepdims=True)
        idx = jnp.min(jnp.where(v == m, eid, ne), axis=0, keepdims=True)
        sel = eid == idx
        e_ref[k:k + 1, :] = idx
        tops.append(m)
        sels.append(sel)
        v = jnp.where(sel, -jnp.inf, v)
    ex = [jnp.exp(m - tops[0]) for m in tops]
    den = ex[0] + ex[1] + ex[2] + ex[3]
    for k in range(TOP_K):
        gate_ref[k:k + 1, :] = ex[k] / den

    chosen = sels[0] | sels[1] | sels[2] | sels[3]
    before = (lax.broadcasted_iota(jnp.int32, (tn, tn), 0) < lax.broadcasted_iota(jnp.int32, (tn, tn), 1))
    prior = jnp.dot(chosen.astype(BF16), before.astype(BF16), preferred_element_type=F32) + cnt_ref[:, 0:1]
    for k in range(TOP_K):
        rank_ref[k:k + 1, :] = jnp.sum(jnp.where(sels[k], prior, 0.0), axis=0, keepdims=True).astype(jnp.int32)
    cnt_ref[...] = cnt_ref[...] + jnp.sum(chosen.astype(F32), axis=1, keepdims=True)


def _route_topk(lgt, tn):
    ne, t = lgt.shape
    kt = pl.BlockSpec((TOP_K, tn), lambda i: (0, i))
    return pl.pallas_call(
        _route_kernel,
        grid=(t // tn,),
        in_specs=[pl.BlockSpec((ne, tn), lambda i: (0, i))],
        out_specs=[kt, kt, kt, pl.BlockSpec((ne, LANES), lambda i: (0, 0))],
        out_shape=[jax.ShapeDtypeStruct((TOP_K, t), jnp.int32), jax.ShapeDtypeStruct((TOP_K, t), F32),
                   jax.ShapeDtypeStruct((TOP_K, t), jnp.int32), jax.ShapeDtypeStruct((ne, LANES), F32)],
        compiler_params=_cparams(("arbitrary",)),
        name="route",
    )(lgt)


GATHER_UNROLL = 8


def _gather_rows(src_hbm, idx_ref, n, buf_ref, slot, sem):
    def body(g, _):
        for u in range(GATHER_UNROLL):
            r = g * GATHER_UNROLL + u
            src = pl.multiple_of(idx_ref[0, 0, r], SUBLANES)
            dst = pl.multiple_of(r * SUBLANES, SUBLANES)
            pltpu.make_async_copy(src_hbm.at[pl.ds(src, SUBLANES)], buf_ref.at[slot, pl.ds(dst, SUBLANES)],
                                  sem.at[slot]).start(priority=u % 2)
        return 0
    lax.fori_loop(0, n // GATHER_UNROLL, body, 0)


def _wait_rows(src_hbm, n, buf_ref, slot, sem):
    pltpu.make_async_copy(src_hbm.at[pl.ds(0, n * SUBLANES)], buf_ref.at[slot], sem.at[slot]).wait()


def _gather_rows_static(src_hbm, idx_ref, r0, r1, buf_ref, slot, sem):
    for r in range(r0, r1):
        src = pl.multiple_of(idx_ref[0, 0, r], SUBLANES)
        pltpu.make_async_copy(src_hbm.at[pl.ds(src, SUBLANES)], buf_ref.at[slot, pl.ds(r * SUBLANES, SUBLANES)],
                              sem.at[slot]).start(priority=r % 2)


MOE_PHASES = 4


def _moe_kernel(be_ref, nu_ref, tok_ref, tokn_ref, h_hbm, wgu_ref, bgu_ref, wdn_ref, bdn_ref, out_ref,
                buf_ref, wgu_bf, wdn_bf, sem):
    i = pl.program_id(0)
    n_used = nu_ref[0]
    slot = i % 2
    d_ff, d = wdn_bf.shape
    group = MOE_ROWS // (2 * MOE_PHASES)

    def gather_next(g):
        _gather_rows_static(h_hbm, tokn_ref, g * group, (g + 1) * group, buf_ref, 1 - slot, sem)

    @pl.when(i == 0)
    def _():
        _gather_rows(h_hbm, tok_ref, MOE_ROWS, buf_ref, 0, sem)

    @pl.when(i < n_used)
    def _():
        @pl.when((i == 0) | (be_ref[i] != be_ref[jnp.maximum(i - 1, 0)]))
        def _():
            wgu_bf[...] = wgu_ref[0].astype(BF16)
            wdn_bf[...] = wdn_ref[0].astype(BF16)

        _wait_rows(h_hbm, MOE_ROWS, buf_ref, slot, sem)
        xb = _tile_rows_load(buf_ref, slot, 0, MOE_ROWS).astype(BF16)
        cw = d_ff // MOE_PHASES
        acts = []
        for c in range(MOE_PHASES):
            gather_next(c)
            g_cols = slice(c * cw, (c + 1) * cw)
            u_cols = slice(d_ff + c * cw, d_ff + (c + 1) * cw)
            gate = jnp.dot(xb, wgu_bf[:, g_cols], preferred_element_type=F32) + bgu_ref[0, :, g_cols]
            up = jnp.dot(xb, wgu_bf[:, u_cols], preferred_element_type=F32) + bgu_ref[0, :, u_cols]
            gate = jnp.minimum(gate, SWIGLU_LIMIT)
            up = jnp.clip(up, -SWIGLU_LIMIT, SWIGLU_LIMIT)
            acts.append(((up + 1.0) * gate * jax.nn.sigmoid(SWIGLU_ALPHA * gate)).astype(BF16))
        act = jnp.concatenate(acts, axis=1)
        ow = d // MOE_PHASES
        for c in range(MOE_PHASES):
            gather_next(MOE_PHASES + c)
            cols = slice(c * ow, (c + 1) * ow)
            o = jnp.dot(act, wdn_bf[:, cols], preferred_element_type=F32) + bdn_ref[0, :, cols]
            for cc in range(ow // LANES):
                out_ref[pl.ds(c * (ow // LANES) + cc, MOE_ROWS, stride=SUBLANES), :] = o[:, cc * LANES:(cc + 1) * LANES]

    @pl.when(i == n_used)
    def _():
        _wait_rows(h_hbm, MOE_ROWS, buf_ref, slot, sem)

    @pl.when(i >= n_used)
    def _():
        out_ref[...] = jnp.zeros_like(out_ref)


def _moe(h, tok, block_e, n_used, w_gu, b_gu, w_down, b_down):
    d = h.shape[1] * SUBLANES
    nb = tok.shape[0]
    ne, _, two_ff = w_gu.shape
    d_ff = two_ff // 2
    e_map = lambda i, be, nu: (be[i], 0, 0)
    grid_spec = pltpu.PrefetchScalarGridSpec(
        num_scalar_prefetch=2,
        grid=(nb,),
        in_specs=[
            pl.BlockSpec((1, 1, MOE_ROWS), lambda i, be, nu: (i, 0, 0), memory_space=pltpu.SMEM),
            pl.BlockSpec((1, 1, MOE_ROWS), lambda i, be, nu: (jnp.minimum(i + 1, nb - 1), 0, 0),
                         memory_space=pltpu.SMEM),
            pl.BlockSpec(memory_space=pl.ANY),
            pl.BlockSpec((1, d, two_ff), e_map),
            pl.BlockSpec((1, 1, two_ff), e_map),
            pl.BlockSpec((1, d_ff, d), e_map),
            pl.BlockSpec((1, 1, d), e_map),
        ],
        out_specs=pl.BlockSpec((MOE_ROWS * SUBLANES, LANES), lambda i, be, nu: (i, 0)),
        scratch_shapes=[pltpu.VMEM((2, MOE_ROWS * SUBLANES, LANES), F32),
                        pltpu.VMEM((d, two_ff), BF16),
                        pltpu.VMEM((d_ff, d), BF16),
                        pltpu.SemaphoreType.DMA((2,))],
    )
    return pl.pallas_call(
        _moe_kernel,
        grid_spec=grid_spec,
        out_shape=jax.ShapeDtypeStruct((nb * MOE_ROWS * SUBLANES, LANES), F32),
        compiler_params=_cparams(("arbitrary",)),
        name="moe",
    )(block_e, n_used, tok, tok, h, w_gu, b_gu.reshape(ne, 1, two_ff), w_down, b_down.reshape(ne, 1, d))


def _combine_kernel(dst_ref, dstn_ref, x_ref, gate_ref, gf_ref, rows_hbm, y_ref, buf_ref, sem, *, tm, nsteps):
    i = pl.program_id(0)
    slot = i % 2
    n = tm * TOP_K

    @pl.when(i == 0)
    def _():
        _gather_rows(rows_hbm, dst_ref, n, buf_ref, 0, sem)

    @pl.when(i + 1 < nsteps)
    def _():
        _gather_rows(rows_hbm, dstn_ref, n, buf_ref, 1 - slot, sem)

    _wait_rows(rows_hbm, n, buf_ref, slot, sem)

    y = x_ref[...]
    gate = gate_ref[...]
    for k in range(TOP_K):
        y = y + gate[:, k:k + 1] * _tile_rows_load(buf_ref, slot, k * tm, tm)
    y_ref[...] = y * lax.rsqrt(jnp.mean(y * y, axis=-1, keepdims=True) + EPS) * gf_ref[...]


def _combine(x1, gates, dest, rows, g_final, tm):
    t, d = x1.shape
    nsteps = t // tm
    n = tm * TOP_K
    return pl.pallas_call(
        functools.partial(_combine_kernel, tm=tm, nsteps=nsteps),
        grid=(nsteps,),
        in_specs=[pl.BlockSpec((1, 1, n), lambda i: (i, 0, 0), memory_space=pltpu.SMEM),
                  pl.BlockSpec((1, 1, n), lambda i: (jnp.minimum(i + 1, nsteps - 1), 0, 0),
                               memory_space=pltpu.SMEM),
                  pl.BlockSpec((tm, d), lambda i: (i, 0)),
                  pl.BlockSpec((tm, TOP_K), lambda i: (i, 0)),
                  pl.BlockSpec((1, d), lambda i: (0, 0)),
                  pl.BlockSpec(memory_space=pl.ANY)],
        out_specs=pl.BlockSpec((tm, d), lambda i: (i, 0)),
        out_shape=jax.ShapeDtypeStruct((t, d), F32),
        scratch_shapes=[pltpu.VMEM((2, n * SUBLANES, LANES), F32), pltpu.SemaphoreType.DMA((2,))],
        compiler_params=_cparams(("arbitrary",)),
        name="combine",
    )(dest, dest, x1, gates, g_final, rows)


def _route(lgt, tn):
    t = lgt.shape[1]
    n_assign = t * TOP_K
    e4, gates, rank, cnt = _route_topk(lgt, tn)
    counts = cnt[:, 0].astype(jnp.int32)
    padded = (counts + MOE_ROWS - 1) // MOE_ROWS * MOE_ROWS
    pend = jnp.cumsum(padded)
    pstart = pend - padded
    start = jnp.cumsum(counts) - counts
    e_ids = jnp.arange(N_EXPERTS, dtype=jnp.int32)
    dest = rank + jnp.sum(jnp.where(e4[None] == e_ids[:, None, None], pstart[:, None, None], 0), axis=0)
    nb = n_assign // MOE_ROWS + N_EXPERTS + 1
    n_used = (pend[-1] // MOE_ROWS).astype(jnp.int32).reshape(1)
    blk0 = jnp.arange(nb, dtype=jnp.int32) * MOE_ROWS
    block_e = jnp.minimum(jnp.sum((pend[None, :] <= blk0[:, None]).astype(jnp.int32), axis=1), N_EXPERTS - 1)
    of_block = lambda table: jnp.sum(jnp.where(block_e[:, None] == e_ids[None, :], table[None, :], 0), axis=1)
    order = jnp.argsort(e4.T.reshape(-1))
    within = (blk0 - of_block(pstart))[:, None] + jnp.arange(MOE_ROWS, dtype=jnp.int32)[None, :]
    src = jnp.clip(of_block(start)[:, None] + jnp.minimum(within, of_block(counts)[:, None] - 1), 0, n_assign - 1)
    tok = (order[src] // TOP_K).astype(jnp.int32)
    return gates.T, dest * SUBLANES, (tok * SUBLANES).reshape(nb, 1, MOE_ROWS), block_e, n_used


def kernel(x_prompt, x_sample, cache_k, cache_v, state_ret, meta_tokens, g_mix, w_in,
           lam_q1, lam_k1, lam_q2, lam_k2, g_subln, w_ret_out, w_diff_out, w_out,
           g_ffn, w_router, b_router, w_gu, b_gu, w_down, b_down, g_final):
    b, s, d = x_prompt.shape
    db, ds, _ = x_sample.shape
    depth, _, pl_len = cache_k.shape[:3]
    assert depth == 1, "single-layer step only"
    nm = meta_tokens.shape[0]
    hw = DIFF_HEADS * HEAD_W

    lam_init = 0.8 - 0.6 * math.exp(-0.3 * 0)
    lam = (jnp.exp(jnp.sum(lam_q1[0] * lam_k1[0])) - jnp.exp(jnp.sum(lam_q2[0] * lam_k2[0]))
           + lam_init).reshape(1).astype(F32)
    coef = 1.0 - lam_init

    w_in_bf = w_in[0].astype(BF16)
    g_mix2 = g_mix[0].reshape(1, d)

    tm = 512 if (b * s) % 512 == 0 else s
    tab_p = _tables(N_META + jnp.arange(s, dtype=jnp.int32))
    rq, rk, rv, rg, dq, dk, dv, ga, gb = _proj(x_prompt.reshape(b * s, d), g_mix2, w_in_bf, tab_p, tm, s // tm)

    pos_small = jnp.concatenate([jnp.tile(N_META + pl_len + jnp.arange(ds, dtype=jnp.int32), db),
                                 jnp.arange(nm, dtype=jnp.int32)])
    x_small = jnp.concatenate([x_sample.reshape(db * ds, d), meta_tokens.astype(F32)], axis=0)
    small = _proj(x_small, g_mix2, w_in_bf, _tables(pos_small), x_small.shape[0], 1)
    ns = db * ds
    srq, srk, srv, srg, sdq, sdk, sdv, sga, sgb = [a[:ns] for a in small]
    mrq, mrk, mrv, mrg, _, mdk, mdv, _, _ = [a[ns:] for a in small]

    cb_small = 128

    def pad_rows(a, nb_, n_):
        a = a.reshape(nb_, n_, a.shape[-1])
        return jnp.pad(a, ((0, 0), (0, cb_small - n_), (0, 0)))

    zero_state = jnp.zeros((1, RET_HEADS, RET_DK, RET_DV), F32)
    _, s_meta = _retention(pad_rows(mrq, 1, nm), pad_rows(mrk, 1, nm), pad_rows(mrv, 1, nm),
                           pad_rows(mrg, 1, nm), zero_state, cb_small, nm)
    cb = 256
    r3 = lambda a: a.reshape(b, s, a.shape[-1])
    yr, ret_p = _retention(r3(rq), r3(rk), r3(rv), r3(rg), s_meta, cb, cb)
    syr, ret_s = _retention(pad_rows(srq, db, ds), pad_rows(srk, db, ds), pad_rows(srv, db, ds),
                            pad_rows(srg, db, ds), state_ret[0], cb_small, ds)
    syr = syr[:, :ds].reshape(ns, -1)

    g_sub = g_subln[0].reshape(1, HEAD_W)
    ext_rows = 128
    pad_ext = lambda a: jnp.pad(a, ((0, 0), (0, ext_rows - a.shape[1]), (0, 0)))
    od = _attention_prompt(r3(dq), r3(dk), r3(dv), pad_ext(mdk[None]), pad_ext(mdv[None]), nm,
                           lam, g_sub, coef, 256, 4)
    sdk3, sdv3 = sdk.reshape(db, ds, hw), sdv.reshape(db, ds, hw)
    ke_s = jnp.concatenate([jnp.broadcast_to(mdk[None], (db, nm, hw)), sdk3], axis=1)
    ve_s = jnp.concatenate([jnp.broadcast_to(mdv[None], (db, nm, hw)), sdv3], axis=1)
    sod = _attention_sample(sdq.reshape(db, ds, hw), cache_k[0], cache_v[0], pad_ext(ke_s), pad_ext(ve_s),
                            nm + ds, lam, g_sub, coef, 512)

    wr, wd, wo = w_ret_out[0].astype(BF16), w_diff_out[0].astype(BF16), w_out[0].astype(BF16)
    gf = g_ffn[0].reshape(1, d)
    wrt = w_router[0].T.astype(BF16)
    br = b_router[0].astype(F32).reshape(N_EXPERTS, 1)
    sx1, sh, slgt = _mix(x_sample.reshape(ns, d), syr, sod.reshape(ns, hw), sga, sgb, wr, wd, wo, gf, wrt, br, ns)
    x1, h, lgt = _mix(x_prompt.reshape(b * s, d), yr.reshape(b * s, -1), od.reshape(b * s, hw), ga, gb,
                      wr, wd, wo, gf, wrt, br, ns, (sh, slgt))

    t = b * s
    gates, dest, tok, block_e, n_used = _route(lgt, ns)
    rows = _moe(h, tok, block_e, n_used, w_gu[0], b_gu[0], w_down[0], b_down[0])
    gfin = g_final.reshape(1, d)

    def combine(x1g, lo, n):
        tc = min(256, n)
        dg = dest[:, lo:lo + n].reshape(TOP_K, n // tc, tc).swapaxes(0, 1).reshape(n // tc, 1, TOP_K * tc)
        return _combine(x1g, gates[lo:lo + n], dg, rows, gfin, tc)

    y_p = combine(x1, 0, t)
    y_s = combine(sx1, t, ns)

    k_p = jnp.concatenate([jnp.broadcast_to(mdk[None], (b, nm, hw)), r3(dk)], axis=1)
    v_p = jnp.concatenate([jnp.broadcast_to(mdv[None], (b, nm, hw)), r3(dv)], axis=1)
    shp = lambda a: a.reshape(1, a.shape[0], a.shape[1], DIFF_HEADS, HEAD_W)
    return (y_p.reshape(b, s, d), y_s.reshape(db, ds, d), ret_p[None], shp(k_p), shp(v_p),
            ret_s[None], shp(sdk3), shp(sdv3))
```

```python
import functools
import math

import jax
import jax.numpy as jnp
from jax import lax
from jax.experimental import pallas as pl
from jax.experimental.pallas import tpu as pltpu

F32 = jnp.float32
BF16 = jnp.bfloat16

EPS = 1e-6
N_META = 16
CHUNK = 64
RET_HEADS = 4
RET_DK = 128
RET_DV = 256
RET_THETA = 10000.0
DIFF_HEADS = 8
DIFF_DH = 64
ROT_DIM = DIFF_DH // 4
ROPE_THETA = 500000.0
N_EXPERTS = 32
TOP_K = 4
SWIGLU_LIMIT = 7.0
SWIGLU_ALPHA = 1.702

LANES = 128
SUBLANES = 8
HEAD_W = 2 * DIFF_DH
MOE_ROWS = 512
VMEM_LIMIT = 56 * 1024 * 1024
NEG = -1e30
Q_SCALE = DIFF_DH ** -0.5 * math.log2(math.e)


def _cparams(sem):
    return pltpu.CompilerParams(dimension_semantics=sem, vmem_limit_bytes=VMEM_LIMIT)


def _tile_rows_store(ref, val):
    n = val.shape[0]
    for c in range(SUBLANES):
        ref[pl.ds(c, n, stride=SUBLANES), :] = val[:, c * LANES:(c + 1) * LANES]


def _tile_rows_load(ref, slot, first, n):
    return jnp.concatenate([ref[slot, pl.ds(first * SUBLANES + c, n, stride=SUBLANES), :]
                            for c in range(SUBLANES)], axis=1)


def _ret_tables(pos):
    angle = RET_THETA ** (-jnp.linspace(0.0, 1.0, RET_DK // 2, dtype=F32))
    angle = jnp.repeat(angle, 2)
    ang = pos.astype(F32)[:, None] * angle[None, :]
    cos, sin = jnp.cos(ang), jnp.sin(ang)
    even = (jnp.arange(RET_DK) % 2 == 0)[None, :]
    return cos, jnp.where(even, -sin, 0.0), jnp.where(even, 0.0, sin)


def _rope_tables(pos):
    half = ROT_DIM // 2
    inv = ROPE_THETA ** (-jnp.arange(half, dtype=F32) * (2.0 / ROT_DIM))
    ang = pos.astype(F32)[:, None] * inv[None, :]
    cos, sin = jnp.cos(ang), jnp.sin(ang)
    jj = jnp.arange(HEAD_W) % DIFF_DH
    first = (jj < half)[None, :]
    second = ((jj >= half) & (jj < ROT_DIM))[None, :]
    cos_l = jnp.take(cos, jj % half, axis=1)
    sin_l = jnp.take(sin, jj % half, axis=1)
    c = jnp.where(first | second, cos_l, 1.0)
    sa = jnp.where(first, -sin_l, 0.0)
    sb = jnp.where(second, sin_l, 0.0)
    return c, sa, sb


def _tables(pos):
    return jnp.concatenate(_ret_tables(pos) + _rope_tables(pos), axis=1)


def _proj_kernel(x_ref, g_ref, w_ref, tab_ref, rq_ref, rk_ref, rv_ref, rg_ref,
                 dq_ref, dk_ref, dv_ref, ga_ref, gb_ref):
    x = x_ref[...]
    ms = jnp.mean(x * x, axis=-1, keepdims=True)
    h = (x * lax.rsqrt(ms + EPS) * g_ref[...]).astype(BF16)

    def mm(c0, width):
        return jnp.dot(h, w_ref[:, c0:c0 + width], preferred_element_type=F32)

    def rot(p, t0, near, far):
        c = tab_ref[:, t0:t0 + LANES]
        sa = tab_ref[:, t0 + LANES:t0 + 2 * LANES]
        sb = tab_ref[:, t0 + 2 * LANES:t0 + 3 * LANES]
        return p * c + pltpu.roll(p, LANES - near, 1) * sa + pltpu.roll(p, far, 1) * sb

    kw = RET_HEADS * RET_DK
    vw = RET_HEADS * RET_DV
    dw = DIFF_HEADS * HEAD_W
    c0 = 0
    p = mm(c0, kw)
    for hh in range(RET_HEADS):
        sl = slice(hh * LANES, (hh + 1) * LANES)
        rq_ref[:, sl] = rot(p[:, sl], 0, 1, 1).astype(BF16)
    c0 += kw
    p = mm(c0, kw)
    for hh in range(RET_HEADS):
        sl = slice(hh * LANES, (hh + 1) * LANES)
        rk_ref[:, sl] = (rot(p[:, sl], 0, 1, 1) * (RET_DK ** -0.5)).astype(BF16)
    c0 += kw
    rv_ref[...] = mm(c0, vw).astype(BF16)
    c0 += vw
    rg_ref[...] = mm(c0, vw).astype(BF16)
    c0 += vw
    half = ROT_DIM // 2
    p = mm(c0, dw)
    for hh in range(DIFF_HEADS):
        sl = slice(hh * LANES, (hh + 1) * LANES)
        dq_ref[:, sl] = (rot(p[:, sl], 3 * LANES, half, half) * Q_SCALE).astype(BF16)
    c0 += dw
    p = mm(c0, dw)
    for hh in range(DIFF_HEADS):
        sl = slice(hh * LANES, (hh + 1) * LANES)
        dk_ref[:, sl] = rot(p[:, sl], 3 * LANES, half, half)
    c0 += dw
    dv_ref[...] = mm(c0, dw)
    c0 += dw
    d = x.shape[1]
    ga_ref[...] = mm(c0, d).astype(BF16)
    c0 += d
    gb_ref[...] = mm(c0, d).astype(BF16)


def _proj(x, g, w_bf, tab, tm, tab_blocks):
    t, d = x.shape
    kw, vw, dw = RET_HEADS * RET_DK, RET_HEADS * RET_DV, DIFF_HEADS * HEAD_W
    widths = (kw, kw, vw, vw, dw, dw, dw, d, d)
    dtypes = (BF16, BF16, BF16, BF16, BF16, F32, F32, BF16, BF16)
    row = lambda i: (i, 0)
    const = lambda i: (0, 0)
    return pl.pallas_call(
        _proj_kernel,
        grid=(t // tm,),
        in_specs=[pl.BlockSpec((tm, d), row),
                  pl.BlockSpec((1, d), const),
                  pl.BlockSpec(w_bf.shape, const),
                  pl.BlockSpec((tm, tab.shape[1]), lambda i: (i % tab_blocks, 0))],
        out_specs=[pl.BlockSpec((tm, w), row) for w in widths],
        out_shape=[jax.ShapeDtypeStruct((t, w), dt) for w, dt in zip(widths, dtypes)],
        compiler_params=_cparams(("arbitrary",)),
        name="proj",
    )(x, g, w_bf, tab)


def _ret_log_gammas():
    return tuple(math.log(1.0 - 2.0 ** (-5.0 - hh)) for hh in range(RET_HEADS))


def _ret_kernel(q_ref, k_ref, v_ref, g_ref, s0_ref, y_ref, st_ref, *, cb, n_valid):
    @pl.when(pl.program_id(1) == 0)
    def _():
        st_ref[...] = s0_ref[...]

    row = lax.broadcasted_iota(jnp.int32, (cb, cb), 0)
    col = lax.broadcasted_iota(jnp.int32, (cb, cb), 1)
    dist = (row - col).astype(F32)
    idx = lax.broadcasted_iota(jnp.int32, (cb, 1), 0).astype(F32)
    for hh, lg in enumerate(_ret_log_gammas()):
        q = q_ref[0, :, hh * RET_DK:(hh + 1) * RET_DK]
        k = k_ref[0, :, hh * RET_DK:(hh + 1) * RET_DK]
        v = v_ref[0, :, hh * RET_DV:(hh + 1) * RET_DV]
        g = g_ref[0, :, hh * RET_DV:(hh + 1) * RET_DV].astype(F32)
        st = st_ref[0, hh]
        decay = jnp.where(dist >= 0, jnp.exp(lg * jnp.maximum(dist, 0.0)), 0.0)
        s = lax.dot_general(q, k, (((1,), (1,)), ((), ())), preferred_element_type=F32) * decay
        o = jnp.dot(s.astype(BF16), v, preferred_element_type=F32)
        o = o + jnp.dot(q, st.astype(BF16), preferred_element_type=F32) * jnp.exp(lg * (idx + 1.0))
        kw = (k.astype(F32) * jnp.exp(lg * (n_valid - 1.0 - idx))).astype(BF16)
        st_ref[0, hh] = math.exp(lg * n_valid) * st + lax.dot_general(
            kw, v, (((0,), (0,)), ((), ())), preferred_element_type=F32)
        on = o * lax.rsqrt(jnp.mean(o * o, axis=-1, keepdims=True) + EPS)
        y_ref[0, :, hh * RET_DV:(hh + 1) * RET_DV] = (g * jax.nn.sigmoid(g) * on).astype(BF16)


def _retention(q, k, v, g, s0, cb, n_valid):
    b, s, _ = q.shape
    s0_map = (lambda i, c: (i, 0, 0, 0)) if s0.shape[0] == b else (lambda i, c: (0, 0, 0, 0))
    blk = lambda w: pl.BlockSpec((1, cb, w), lambda i, c: (i, c, 0))
    st_block = (1, RET_HEADS, RET_DK, RET_DV)
    return pl.pallas_call(
        functools.partial(_ret_kernel, cb=cb, n_valid=n_valid),
        grid=(b, s // cb),
        in_specs=[blk(q.shape[2]), blk(k.shape[2]), blk(v.shape[2]), blk(g.shape[2]),
                  pl.BlockSpec(st_block, s0_map)],
        out_specs=[blk(v.shape[2]), pl.BlockSpec(st_block, lambda i, c: (i, 0, 0, 0))],
        out_shape=[jax.ShapeDtypeStruct(v.shape, BF16),
                   jax.ShapeDtypeStruct((b,) + st_block[1:], F32)],
        compiler_params=_cparams(("arbitrary", "arbitrary")),
        name="retention",
    )(q, k, v, g, s0)


def _scores(qq, kk):
    return lax.dot_general(qq, kk, (((1,), (1,)), ((), ())), preferred_element_type=F32)


def _softmax_seed(scores, values):
    ms = [jnp.max(s, axis=-1, keepdims=True) for s in scores]
    ps = [jnp.exp2(s - m) for s, m in zip(scores, ms)]
    out = []
    for m, p, vv in zip(ms, ps, values):
        out += [m, jnp.sum(p, axis=-1, keepdims=True),
                jnp.dot(p.astype(BF16), vv, preferred_element_type=F32)]
    return out


def _softmax_step(carry, scores, values):
    m2s = [jnp.maximum(carry[3 * c], jnp.max(s, axis=-1, keepdims=True)) for c, s in enumerate(scores)]
    ps = [jnp.exp2(s - m2) for s, m2 in zip(scores, m2s)]
    out = []
    for c, (m2, p, vv) in enumerate(zip(m2s, ps, values)):
        m, l, acc = carry[3 * c:3 * c + 3]
        a = jnp.exp2(m - m2)
        out += [m2, a * l + jnp.sum(p, axis=-1, keepdims=True),
                a * acc + jnp.dot(p.astype(BF16), vv, preferred_element_type=F32)]
    return out


def _sub_norm(o1, o2, lam, gs, coef):
    o = o1 - lam * o2
    return o * lax.rsqrt(jnp.mean(o * o, axis=-1, keepdims=True) + EPS) * gs * coef


SUM_ROWS = 16
VT_ROWS = HEAD_W + SUM_ROWS


def _softmax_seed_t(scores_t, values_t):
    ms = [jnp.max(s, axis=0, keepdims=True) for s in scores_t]
    ps = [jnp.exp2(s - m) for s, m in zip(scores_t, ms)]
    out = []
    for m, p, vt in zip(ms, ps, values_t):
        out += [m, jnp.dot(vt, p.astype(BF16), preferred_element_type=F32)]
    return out


def _softmax_step_t(carry, scores_t, values_t):
    m2s = [jnp.maximum(carry[2 * c], jnp.max(s, axis=0, keepdims=True)) for c, s in enumerate(scores_t)]
    ps = [jnp.exp2(s - m2) for s, m2 in zip(scores_t, m2s)]
    out = []
    for c, (m2, p, vt) in enumerate(zip(m2s, ps, values_t)):
        m, acc = carry[2 * c:2 * c + 2]
        out += [m2, jnp.exp2(m - m2) * acc + jnp.dot(vt, p.astype(BF16), preferred_element_type=F32)]
    return out


def _attn_prompt_kernel(lam_ref, q_ref, km_ref, vm_ref, ke_ref, ve_ref, gst_ref, o_ref, kb_ref, vt_ref,
                        *, sq, tq, hp, n_ext, coef):
    heads = [slice(hh * HEAD_W, (hh + 1) * HEAD_W) for hh in range(hp)]
    vrows = [slice(hh * VT_ROWS, (hh + 1) * VT_ROWS) for hh in range(hp)]
    kb_ref[...] = km_ref[0].astype(BF16)
    ones = jnp.ones((SUM_ROWS, tq), BF16)
    for j in range(sq // tq):
        for hh, sl in enumerate(heads):
            vt_ref[j, hh * VT_ROWS:hh * VT_ROWS + HEAD_W, :] = vm_ref[0, j * tq:(j + 1) * tq, sl].T.astype(BF16)
            vt_ref[j, hh * VT_ROWS + HEAD_W:(hh + 1) * VT_ROWS, :] = ones
    ke = [ke_ref[0, :, sl].astype(BF16) for sl in heads]
    vet = [jnp.concatenate([ve_ref[0, :, sl].T.astype(BF16), jnp.ones((SUM_ROWS, ve_ref.shape[1]), BF16)], axis=0)
           for sl in heads]
    lam = lam_ref[0]
    lo = lax.broadcasted_iota(jnp.int32, (1, HEAD_W), 1) < DIFF_DH
    ext_ok = lax.broadcasted_iota(jnp.int32, (ke_ref.shape[1], 1), 0) < n_ext
    diag = (lax.broadcasted_iota(jnp.int32, (tq, tq), 0) // CHUNK
            <= lax.broadcasted_iota(jnp.int32, (tq, tq), 1) // CHUNK)

    def q_tile(qi, _):
        q0 = pl.multiple_of(qi * tq, tq)
        qs = []
        for sl in heads:
            q = q_ref[0, pl.ds(q0, tq), sl]
            zero = jnp.zeros_like(q)
            qs += [jnp.where(lo, q, zero), jnp.where(lo, zero, q)]

        ext_s = [jnp.where(ext_ok, _scores(ke[c // 2], qq), NEG) for c, qq in enumerate(qs)]
        carry = _softmax_seed_t(ext_s, [vet[c // 2] for c in range(len(qs))])

        def step(j, carry, mask=None):
            k0 = pl.multiple_of(j * tq, tq)
            ss = [_scores(kb_ref[pl.ds(k0, tq), heads[c // 2]], qq) for c, qq in enumerate(qs)]
            if mask is not None:
                ss = [jnp.where(mask, s, NEG) for s in ss]
            return tuple(_softmax_step_t(carry, ss, [vt_ref[j, vrows[c // 2], :] for c in range(len(qs))]))

        carry = lax.fori_loop(0, qi, step, tuple(carry))
        carry = step(qi, carry, diag)
        for hh, sl in enumerate(heads):
            _, a1, _, a2 = carry[4 * hh:4 * hh + 4]
            o = a1[:HEAD_W] / a1[HEAD_W:HEAD_W + 1] - lam * (a2[:HEAD_W] / a2[HEAD_W:HEAD_W + 1])
            on = o * lax.rsqrt(jnp.mean(o * o, axis=0, keepdims=True) + EPS) * gst_ref[...] * coef
            o_ref[0, pl.ds(q0, tq), sl] = on.T.astype(BF16)
        return 0

    lax.fori_loop(0, sq // tq, q_tile, 0)


def _attention_prompt(q, k_main, v_main, k_ext, v_ext, n_ext, lam, g_subln, coef, tq, hp):
    b, sq, hw = q.shape
    w = hp * HEAD_W
    bh = pl.BlockSpec((1, sq, w), lambda i, h: (i, 0, h))
    ext = pl.BlockSpec((1, k_ext.shape[1], w), lambda i, h: (0, 0, h))
    gst = jnp.broadcast_to(g_subln.reshape(HEAD_W, 1), (HEAD_W, tq))
    return pl.pallas_call(
        functools.partial(_attn_prompt_kernel, sq=sq, tq=tq, hp=hp, n_ext=n_ext, coef=coef),
        grid=(b, hw // w),
        in_specs=[pl.BlockSpec(memory_space=pltpu.SMEM), bh, bh, bh, ext, ext,
                  pl.BlockSpec((HEAD_W, tq), lambda i, h: (0, 0))],
        out_specs=bh,
        out_shape=jax.ShapeDtypeStruct(q.shape, BF16),
        scratch_shapes=[pltpu.VMEM((sq, w), BF16), pltpu.VMEM((sq // tq, hp * VT_ROWS, tq), BF16)],
        compiler_params=_cparams(("arbitrary", "arbitrary")),
        name="attention_prompt",
    )(lam, q, k_main, v_main, k_ext, v_ext, gst)


def _attn_sample_kernel(lam_ref, q_ref, kc_ref, vc_ref, ke_ref, ve_ref, gs_ref, o_ref, m_ref, l_ref, acc_ref,
                        *, ds, nh, n_ext, coef):
    t = pl.program_id(1)
    tkv = kc_ref.shape[1] // nh
    lo = lax.broadcasted_iota(jnp.int32, (1, HEAD_W), 1) < DIFF_DH
    ext_ok = lax.broadcasted_iota(jnp.int32, (1, ke_ref.shape[1]), 1) < n_ext
    heads = [slice(hh * HEAD_W, (hh + 1) * HEAD_W) for hh in range(nh)]

    def stacked_q(sl):
        q = q_ref[0, :, sl]
        zero = jnp.zeros_like(q)
        return jnp.concatenate([jnp.where(lo, q, zero), jnp.where(lo, zero, q)], axis=0)

    qs = [stacked_q(sl) for sl in heads]

    def write_stats(stats):
        for hh in range(nh):
            m_ref[hh], l_ref[hh], acc_ref[hh] = stats[3 * hh:3 * hh + 3]

    @pl.when(t == 0)
    def _():
        ext_s = [jnp.where(ext_ok, _scores(qq, ke_ref[0, :, sl].astype(BF16)), NEG) for qq, sl in zip(qs, heads)]
        write_stats(_softmax_seed(ext_s, [ve_ref[0, :, sl].astype(BF16) for sl in heads]))

    carry = []
    for hh in range(nh):
        carry += [m_ref[hh], l_ref[hh], acc_ref[hh]]
    ss = [_scores(qq, kc_ref[0, pl.ds(hh, tkv, stride=nh), :].astype(BF16)) for hh, qq in enumerate(qs)]
    write_stats(_softmax_step(carry, ss, [vc_ref[0, pl.ds(hh, tkv, stride=nh), :].astype(BF16) for hh in range(nh)]))

    @pl.when(t == pl.num_programs(1) - 1)
    def _():
        lam = lam_ref[0]
        for hh, sl in enumerate(heads):
            o = acc_ref[hh] / l_ref[hh]
            o_ref[0, :, sl] = _sub_norm(o[:ds], o[ds:], lam, gs_ref[...], coef).astype(BF16)


def _attention_sample(q, k_cache, v_cache, k_ext, v_ext, n_ext, lam, g_subln, coef, tkv):
    b, ds, hw = q.shape
    _, pl_len, nh, _ = k_cache.shape
    per_b = lambda rows: pl.BlockSpec((1, rows, hw), lambda i, t: (i, 0, 0))
    cache = pl.BlockSpec((1, tkv * nh, HEAD_W), lambda i, t: (i, t, 0))
    return pl.pallas_call(
        functools.partial(_attn_sample_kernel, ds=ds, nh=nh, n_ext=n_ext, coef=coef),
        grid=(b, pl_len // tkv),
        in_specs=[pl.BlockSpec(memory_space=pltpu.SMEM), per_b(ds), cache, cache,
                  per_b(k_ext.shape[1]), per_b(k_ext.shape[1]),
                  pl.BlockSpec((1, HEAD_W), lambda i, t: (0, 0))],
        out_specs=per_b(ds),
        out_shape=jax.ShapeDtypeStruct(q.shape, BF16),
        scratch_shapes=[pltpu.VMEM((nh, 2 * ds, 1), F32), pltpu.VMEM((nh, 2 * ds, 1), F32),
                        pltpu.VMEM((nh, 2 * ds, HEAD_W), F32)],
        compiler_params=_cparams(("arbitrary", "arbitrary")),
        name="attention_sample",
    )(lam, q, k_cache.reshape(b, pl_len * nh, HEAD_W), v_cache.reshape(b, pl_len * nh, HEAD_W),
      k_ext, v_ext, g_subln)


def _mix_kernel(x_ref, yr_ref, od_ref, ga_ref, gb_ref, wr_ref, wd_ref, wo_ref, gf_ref, wrt_ref, br_ref,
                x1_ref, h_ref, lgt_ref):
    ya = jnp.dot(yr_ref[...], wr_ref[...], preferred_element_type=F32)
    yb = jnp.dot(od_ref[...], wd_ref[...], preferred_element_type=F32)
    z = jax.nn.sigmoid(ga_ref[...].astype(F32)) * ya + jax.nn.sigmoid(gb_ref[...].astype(F32)) * yb
    x1 = x_ref[...] + jnp.dot(z.astype(BF16), wo_ref[...], preferred_element_type=F32)
    x1_ref[...] = x1
    h = x1 * lax.rsqrt(jnp.mean(x1 * x1, axis=-1, keepdims=True) + EPS) * gf_ref[...]
    _tile_rows_store(h_ref, h)
    lgt_ref[...] = _scores(wrt_ref[...], h.astype(BF16)) + br_ref[...]


def _mix_tail_kernel(*refs):
    th_ref, tlgt_ref = refs[11:13]
    h_ref, lgt_ref = refs[14:16]
    last = pl.num_programs(0) - 1

    @pl.when(pl.program_id(0) < last)
    def _():
        _mix_kernel(*refs[:11], *refs[13:])

    @pl.when(pl.program_id(0) == last)
    def _():
        h_ref[...] = th_ref[...]
        lgt_ref[...] = tlgt_ref[...]


def _mix(x, yr, od, ga, gb, wr, wd, wo, gf, wrt, br, tm, tail=None):
    t, d = x.shape
    ne = wrt.shape[0]
    nt = t // tm
    extra = 0 if tail is None else 1
    row = lambda w: pl.BlockSpec((tm, w), lambda i: (jnp.minimum(i, nt - 1), 0))
    full = lambda a: pl.BlockSpec(a.shape, lambda i: (0, 0))
    ops = (x, yr, od, ga, gb, wr, wd, wo, gf, wrt, br) + (() if tail is None else tuple(tail))
    return pl.pallas_call(
        _mix_kernel if tail is None else _mix_tail_kernel,
        grid=(nt + extra,),
        in_specs=[row(d), row(yr.shape[1]), row(od.shape[1]), row(d), row(d)] + [full(a) for a in ops[5:]],
        out_specs=[row(d), pl.BlockSpec((tm * SUBLANES, LANES), lambda i: (i, 0)),
                   pl.BlockSpec((ne, tm), lambda i: (0, i))],
        out_shape=[jax.ShapeDtypeStruct((t, d), F32),
                   jax.ShapeDtypeStruct(((t + extra * tm) * SUBLANES, LANES), F32),
                   jax.ShapeDtypeStruct((ne, t + extra * tm), F32)],
        compiler_params=_cparams(("arbitrary",)),
        name="mix",
    )(*ops)


def _route_kernel(lgt_ref, e_ref, gate_ref, rank_ref, cnt_ref):
    @pl.when(pl.program_id(0) == 0)
    def _():
        cnt_ref[...] = jnp.zeros_like(cnt_ref)

    v = lgt_ref[...]
    ne, tn = v.shape
    eid = lax.broadcasted_iota(jnp.int32, (ne, tn), 0)
    tops, sels = [], []
    for k in range(TOP_K):
        m = jnp.max(v, axis=0, keepdims=True)
        idx = jnp.min(jnp.where(v == m, eid, ne), axis=0, keepdims=True)
        sel = eid == idx
        e_ref[k:k + 1, :] = idx
        tops.append(m)
        sels.append(sel)
        v = jnp.where(sel, -jnp.inf, v)
    ex = [jnp.exp(m - tops[0]) for m in tops]
    den = ex[0] + ex[1] + ex[2] + ex[3]
    for k in range(TOP_K):
        gate_ref[k:k + 1, :] = ex[k] / den

    chosen = sels[0] | sels[1] | sels[2] | sels[3]
    before = (lax.broadcasted_iota(jnp.int32, (tn, tn), 0) < lax.broadcasted_iota(jnp.int32, (tn, tn), 1))
    prior = jnp.dot(chosen.astype(BF16), before.astype(BF16), preferred_element_type=F32) + cnt_ref[:, 0:1]
    for k in range(TOP_K):
        rank_ref[k:k + 1, :] = jnp.sum(jnp.where(sels[k], prior, 0.0), axis=0, keepdims=True).astype(jnp.int32)
    cnt_ref[...] = cnt_ref[...] + jnp.sum(chosen.astype(F32), axis=1, keepdims=True)


def _route_topk(lgt, tn):
    ne, t = lgt.shape
    kt = pl.BlockSpec((TOP_K, tn), lambda i: (0, i))
    return pl.pallas_call(
        _route_kernel,
        grid=(t // tn,),
        in_specs=[pl.BlockSpec((ne, tn), lambda i: (0, i))],
        out_specs=[kt, kt, kt, pl.BlockSpec((ne, LANES), lambda i: (0, 0))],
        out_shape=[jax.ShapeDtypeStruct((TOP_K, t), jnp.int32), jax.ShapeDtypeStruct((TOP_K, t), F32),
                   jax.ShapeDtypeStruct((TOP_K, t), jnp.int32), jax.ShapeDtypeStruct((ne, LANES), F32)],
        compiler_params=_cparams(("arbitrary",)),
        name="route",
    )(lgt)


GATHER_UNROLL = 8


def _gather_rows(src_hbm, idx_ref, n, buf_ref, slot, sem):
    def body(g, _):
        for u in range(GATHER_UNROLL):
            r = g * GATHER_UNROLL + u
            src = pl.multiple_of(idx_ref[0, 0, r], SUBLANES)
            dst = pl.multiple_of(r * SUBLANES, SUBLANES)
            pltpu.make_async_copy(src_hbm.at[pl.ds(src, SUBLANES)], buf_ref.at[slot, pl.ds(dst, SUBLANES)],
                                  sem.at[slot]).start(priority=u % 2)
        return 0
    lax.fori_loop(0, n // GATHER_UNROLL, body, 0)


def _wait_rows(src_hbm, n, buf_ref, slot, sem):
    pltpu.make_async_copy(src_hbm.at[pl.ds(0, n * SUBLANES)], buf_ref.at[slot], sem.at[slot]).wait()


def _moe_kernel(be_ref, nu_ref, tok_ref, tokn_ref, h_hbm, wgu_ref, bgu_ref, wdn_ref, bdn_ref, out_ref,
                buf_ref, wgu_bf, wdn_bf, sem):
    i = pl.program_id(0)
    n_used = nu_ref[0]
    slot = i % 2
    d_ff = wdn_bf.shape[0]

    @pl.when(i == 0)
    def _():
        _gather_rows(h_hbm, tok_ref, MOE_ROWS, buf_ref, 0, sem)

    @pl.when(i + 1 < n_used)
    def _():
        _gather_rows(h_hbm, tokn_ref, MOE_ROWS, buf_ref, 1 - slot, sem)

    @pl.when(i < n_used)
    def _():
        @pl.when((i == 0) | (be_ref[i] != be_ref[jnp.maximum(i - 1, 0)]))
        def _():
            wgu_bf[...] = wgu_ref[0].astype(BF16)
            wdn_bf[...] = wdn_ref[0].astype(BF16)

        _wait_rows(h_hbm, MOE_ROWS, buf_ref, slot, sem)

        xb = _tile_rows_load(buf_ref, slot, 0, MOE_ROWS).astype(BF16)
        gu = jnp.dot(xb, wgu_bf[...], preferred_element_type=F32) + bgu_ref[0]
        gate = jnp.minimum(gu[:, :d_ff], SWIGLU_LIMIT)
        up = jnp.clip(gu[:, d_ff:], -SWIGLU_LIMIT, SWIGLU_LIMIT)
        act = (up + 1.0) * gate * jax.nn.sigmoid(SWIGLU_ALPHA * gate)
        _tile_rows_store(out_ref, jnp.dot(act.astype(BF16), wdn_bf[...], preferred_element_type=F32) + bdn_ref[0])

    @pl.when(i >= n_used)
    def _():
        out_ref[...] = jnp.zeros_like(out_ref)


def _moe(h, tok, block_e, n_used, w_gu, b_gu, w_down, b_down):
    d = h.shape[1] * SUBLANES
    nb = tok.shape[0]
    ne, _, two_ff = w_gu.shape
    d_ff = two_ff // 2
    e_map = lambda i, be, nu: (be[i], 0, 0)
    grid_spec = pltpu.PrefetchScalarGridSpec(
        num_scalar_prefetch=2,
        grid=(nb,),
        in_specs=[
            pl.BlockSpec((1, 1, MOE_ROWS), lambda i, be, nu: (i, 0, 0), memory_space=pltpu.SMEM),
            pl.BlockSpec((1, 1, MOE_ROWS), lambda i, be, nu: (jnp.minimum(i + 1, nb - 1), 0, 0),
                         memory_space=pltpu.SMEM),
            pl.BlockSpec(memory_space=pl.ANY),
            pl.BlockSpec((1, d, two_ff), e_map),
            pl.BlockSpec((1, 1, two_ff), e_map),
            pl.BlockSpec((1, d_ff, d), e_map),
            pl.BlockSpec((1, 1, d), e_map),
        ],
        out_specs=pl.BlockSpec((MOE_ROWS * SUBLANES, LANES), lambda i, be, nu: (i, 0)),
        scratch_shapes=[pltpu.VMEM((2, MOE_ROWS * SUBLANES, LANES), F32),
                        pltpu.VMEM((d, two_ff), BF16),
                        pltpu.VMEM((d_ff, d), BF16),
                        pltpu.SemaphoreType.DMA((2,))],
    )
    return pl.pallas_call(
        _moe_kernel,
        grid_spec=grid_spec,
        out_shape=jax.ShapeDtypeStruct((nb * MOE_ROWS * SUBLANES, LANES), F32),
        compiler_params=_cparams(("arbitrary",)),
        name="moe",
    )(block_e, n_used, tok, tok, h, w_gu, b_gu.reshape(ne, 1, two_ff), w_down, b_down.reshape(ne, 1, d))


def _combine_kernel(dst_ref, dstn_ref, x_ref, gate_ref, gf_ref, rows_hbm, y_ref, buf_ref, sem, *, tm, nsteps):
    i = pl.program_id(0)
    slot = i % 2
    n = tm * TOP_K

    @pl.when(i == 0)
    def _():
        _gather_rows(rows_hbm, dst_ref, n, buf_ref, 0, sem)

    @pl.when(i + 1 < nsteps)
    def _():
        _gather_rows(rows_hbm, dstn_ref, n, buf_ref, 1 - slot, sem)

    _wait_rows(rows_hbm, n, buf_ref, slot, sem)

    y = x_ref[...]
    gate = gate_ref[...]
    for k in range(TOP_K):
        y = y + gate[:, k:k + 1] * _tile_rows_load(buf_ref, slot, k * tm, tm)
    y_ref[...] = y * lax.rsqrt(jnp.mean(y * y, axis=-1, keepdims=True) + EPS) * gf_ref[...]


def _combine(x1, gates, dest, rows, g_final, tm):
    t, d = x1.shape
    nsteps = t // tm
    n = tm * TOP_K
    return pl.pallas_call(
        functools.partial(_combine_kernel, tm=tm, nsteps=nsteps),
        grid=(nsteps,),
        in_specs=[pl.BlockSpec((1, 1, n), lambda i: (i, 0, 0), memory_space=pltpu.SMEM),
                  pl.BlockSpec((1, 1, n), lambda i: (jnp.minimum(i + 1, nsteps - 1), 0, 0),
                               memory_space=pltpu.SMEM),
                  pl.BlockSpec((tm, d), lambda i: (i, 0)),
                  pl.BlockSpec((tm, TOP_K), lambda i: (i, 0)),
                  pl.BlockSpec((1, d), lambda i: (0, 0)),
                  pl.BlockSpec(memory_space=pl.ANY)],
        out_specs=pl.BlockSpec((tm, d), lambda i: (i, 0)),
        out_shape=jax.ShapeDtypeStruct((t, d), F32),
        scratch_shapes=[pltpu.VMEM((2, n * SUBLANES, LANES), F32), pltpu.SemaphoreType.DMA((2,))],
        compiler_params=_cparams(("arbitrary",)),
        name="combine",
    )(dest, dest, x1, gates, g_final, rows)


def _route(lgt, tn):
    t = lgt.shape[1]
    n_assign = t * TOP_K
    e4, gates, rank, cnt = _route_topk(lgt, tn)
    counts = cnt[:, 0].astype(jnp.int32)
    padded = (counts + MOE_ROWS - 1) // MOE_ROWS * MOE_ROWS
    pend = jnp.cumsum(padded)
    pstart = pend - padded
    start = jnp.cumsum(counts) - counts
    e_ids = jnp.arange(N_EXPERTS, dtype=jnp.int32)
    dest = rank + jnp.sum(jnp.where(e4[None] == e_ids[:, None, None], pstart[:, None, None], 0), axis=0)
    nb = n_assign // MOE_ROWS + N_EXPERTS
    n_used = (pend[-1] // MOE_ROWS).astype(jnp.int32).reshape(1)
    blk0 = jnp.arange(nb, dtype=jnp.int32) * MOE_ROWS
    block_e = jnp.minimum(jnp.sum((pend[None, :] <= blk0[:, None]).astype(jnp.int32), axis=1), N_EXPERTS - 1)
    of_block = lambda table: jnp.sum(jnp.where(block_e[:, None] == e_ids[None, :], table[None, :], 0), axis=1)
    order = jnp.argsort(e4.T.reshape(-1))
    within = (blk0 - of_block(pstart))[:, None] + jnp.arange(MOE_ROWS, dtype=jnp.int32)[None, :]
    src = jnp.clip(of_block(start)[:, None] + jnp.minimum(within, of_block(counts)[:, None] - 1), 0, n_assign - 1)
    tok = (order[src] // TOP_K).astype(jnp.int32)
    return gates.T, dest * SUBLANES, (tok * SUBLANES).reshape(nb, 1, MOE_ROWS), block_e, n_used


def kernel(x_prompt, x_sample, cache_k, cache_v, state_ret, meta_tokens, g_mix, w_in,
           lam_q1, lam_k1, lam_q2, lam_k2, g_subln, w_ret_out, w_diff_out, w_out,
           g_ffn, w_router, b_router, w_gu, b_gu, w_down, b_down, g_final):
    b, s, d = x_prompt.shape
    db, ds, _ = x_sample.shape
    depth, _, pl_len = cache_k.shape[:3]
    assert depth == 1, "single-layer step only"
    nm = meta_tokens.shape[0]
    hw = DIFF_HEADS * HEAD_W

    lam_init = 0.8 - 0.6 * math.exp(-0.3 * 0)
    lam = (jnp.exp(jnp.sum(lam_q1[0] * lam_k1[0])) - jnp.exp(jnp.sum(lam_q2[0] * lam_k2[0]))
           + lam_init).reshape(1).astype(F32)
    coef = 1.0 - lam_init

    w_in_bf = w_in[0].astype(BF16)
    g_mix2 = g_mix[0].reshape(1, d)

    tm = 512 if (b * s) % 512 == 0 else s
    tab_p = _tables(N_META + jnp.arange(s, dtype=jnp.int32))
    rq, rk, rv, rg, dq, dk, dv, ga, gb = _proj(x_prompt.reshape(b * s, d), g_mix2, w_in_bf, tab_p, tm, s // tm)

    pos_small = jnp.concatenate([jnp.tile(N_META + pl_len + jnp.arange(ds, dtype=jnp.int32), db),
                                 jnp.arange(nm, dtype=jnp.int32)])
    x_small = jnp.concatenate([x_sample.reshape(db * ds, d), meta_tokens.astype(F32)], axis=0)
    small = _proj(x_small, g_mix2, w_in_bf, _tables(pos_small), x_small.shape[0], 1)
    ns = db * ds
    srq, srk, srv, srg, sdq, sdk, sdv, sga, sgb = [a[:ns] for a in small]
    mrq, mrk, mrv, mrg, _, mdk, mdv, _, _ = [a[ns:] for a in small]

    cb_small = 128

    def pad_rows(a, nb_, n_):
        a = a.reshape(nb_, n_, a.shape[-1])
        return jnp.pad(a, ((0, 0), (0, cb_small - n_), (0, 0)))

    zero_state = jnp.zeros((1, RET_HEADS, RET_DK, RET_DV), F32)
    _, s_meta = _retention(pad_rows(mrq, 1, nm), pad_rows(mrk, 1, nm), pad_rows(mrv, 1, nm),
                           pad_rows(mrg, 1, nm), zero_state, cb_small, nm)
    cb = 256
    r3 = lambda a: a.reshape(b, s, a.shape[-1])
    yr, ret_p = _retention(r3(rq), r3(rk), r3(rv), r3(rg), s_meta, cb, cb)
    syr, ret_s = _retention(pad_rows(srq, db, ds), pad_rows(srk, db, ds), pad_rows(srv, db, ds),
                            pad_rows(srg, db, ds), state_ret[0], cb_small, ds)
    syr = syr[:, :ds].reshape(ns, -1)

    g_sub = g_subln[0].reshape(1, HEAD_W)
    ext_rows = 128
    pad_ext = lambda a: jnp.pad(a, ((0, 0), (0, ext_rows - a.shape[1]), (0, 0)))
    od = _attention_prompt(r3(dq), r3(dk), r3(dv), pad_ext(mdk[None]), pad_ext(mdv[None]), nm,
                           lam, g_sub, coef, 256, 4)
    sdk3, sdv3 = sdk.reshape(db, ds, hw), sdv.reshape(db, ds, hw)
    ke_s = jnp.concatenate([jnp.broadcast_to(mdk[None], (db, nm, hw)), sdk3], axis=1)
    ve_s = jnp.concatenate([jnp.broadcast_to(mdv[None], (db, nm, hw)), sdv3], axis=1)
    sod = _attention_sample(sdq.reshape(db, ds, hw), cache_k[0], cache_v[0], pad_ext(ke_s), pad_ext(ve_s),
                            nm + ds, lam, g_sub, coef, 512)

    wr, wd, wo = w_ret_out[0].astype(BF16), w_diff_out[0].astype(BF16), w_out[0].astype(BF16)
    gf = g_ffn[0].reshape(1, d)
    wrt = w_router[0].T.astype(BF16)
    br = b_router[0].astype(F32).reshape(N_EXPERTS, 1)
    sx1, sh, slgt = _mix(x_sample.reshape(ns, d), syr, sod.reshape(ns, hw), sga, sgb, wr, wd, wo, gf, wrt, br, ns)
    x1, h, lgt = _mix(x_prompt.reshape(b * s, d), yr.reshape(b * s, -1), od.reshape(b * s, hw), ga, gb,
                      wr, wd, wo, gf, wrt, br, ns, (sh, slgt))

    t = b * s
    gates, dest, tok, block_e, n_used = _route(lgt, ns)
    rows = _moe(h, tok, block_e, n_used, w_gu[0], b_gu[0], w_down[0], b_down[0])
    gfin = g_final.reshape(1, d)

    def combine(x1g, lo, n):
        tc = min(256, n)
        dg = dest[:, lo:lo + n].reshape(TOP_K, n // tc, tc).swapaxes(0, 1).reshape(n // tc, 1, TOP_K * tc)
        return _combine(x1g, gates[lo:lo + n], dg, rows, gfin, tc)

    y_p = combine(x1, 0, t)
    y_s = combine(sx1, t, ns)

    k_p = jnp.concatenate([jnp.broadcast_to(mdk[None], (b, nm, hw)), r3(dk)], axis=1)
    v_p = jnp.concatenate([jnp.broadcast_to(mdv[None], (b, nm, hw)), r3(dv)], axis=1)
    shp = lambda a: a.reshape(1, a.shape[0], a.shape[1], DIFF_HEADS, HEAD_W)
    return (y_p.reshape(b, s, d), y_s.reshape(db, ds, d), ret_p[None], shp(k_p), shp(v_p),
            ret_s[None], shp(sdk3), shp(sdv3))
```

```python
import functools
import math

import jax
import jax.numpy as jnp
from jax import lax
from jax.experimental import pallas as pl
from jax.experimental.pallas import tpu as pltpu

F32 = jnp.float32
BF16 = jnp.bfloat16

EPS = 1e-6
N_META = 16
CHUNK = 64
RET_HEADS = 4
RET_DK = 128
RET_DV = 256
RET_THETA = 10000.0
DIFF_HEADS = 8
DIFF_DH = 64
ROT_DIM = DIFF_DH // 4
ROPE_THETA = 500000.0
N_EXPERTS = 32
TOP_K = 4
SWIGLU_LIMIT = 7.0
SWIGLU_ALPHA = 1.702

LANES = 128
SUBLANES = 8
HEAD_W = 2 * DIFF_DH
MOE_ROWS = 256
VMEM_LIMIT = 56 * 1024 * 1024
NEG = -1e30
Q_SCALE = DIFF_DH ** -0.5 * math.log2(math.e)


def _cparams(sem):
    return pltpu.CompilerParams(dimension_semantics=sem, vmem_limit_bytes=VMEM_LIMIT)


def _tile_rows_store(ref, val):
    n = val.shape[0]
    for c in range(SUBLANES):
        ref[pl.ds(c, n, stride=SUBLANES), :] = val[:, c * LANES:(c + 1) * LANES]


def _tile_rows_load(ref, slot, first, n):
    return jnp.concatenate([ref[slot, pl.ds(first * SUBLANES + c, n, stride=SUBLANES), :]
                            for c in range(SUBLANES)], axis=1)


def _ret_tables(pos):
    angle = RET_THETA ** (-jnp.linspace(0.0, 1.0, RET_DK // 2, dtype=F32))
    angle = jnp.repeat(angle, 2)
    ang = pos.astype(F32)[:, None] * angle[None, :]
    cos, sin = jnp.cos(ang), jnp.sin(ang)
    even = (jnp.arange(RET_DK) % 2 == 0)[None, :]
    return cos, jnp.where(even, -sin, 0.0), jnp.where(even, 0.0, sin)


def _rope_tables(pos):
    half = ROT_DIM // 2
    inv = ROPE_THETA ** (-jnp.arange(half, dtype=F32) * (2.0 / ROT_DIM))
    ang = pos.astype(F32)[:, None] * inv[None, :]
    cos, sin = jnp.cos(ang), jnp.sin(ang)
    jj = jnp.arange(HEAD_W) % DIFF_DH
    first = (jj < half)[None, :]
    second = ((jj >= half) & (jj < ROT_DIM))[None, :]
    cos_l = jnp.take(cos, jj % half, axis=1)
    sin_l = jnp.take(sin, jj % half, axis=1)
    c = jnp.where(first | second, cos_l, 1.0)
    sa = jnp.where(first, -sin_l, 0.0)
    sb = jnp.where(second, sin_l, 0.0)
    return c, sa, sb


def _tables(pos):
    return jnp.concatenate(_ret_tables(pos) + _rope_tables(pos), axis=1)


def _proj_kernel(x_ref, g_ref, w_ref, tab_ref, rq_ref, rk_ref, rv_ref, rg_ref,
                 dq_ref, dk_ref, dv_ref, ga_ref, gb_ref):
    x = x_ref[...]
    ms = jnp.mean(x * x, axis=-1, keepdims=True)
    h = (x * lax.rsqrt(ms + EPS) * g_ref[...]).astype(BF16)

    def mm(c0, width):
        return jnp.dot(h, w_ref[:, c0:c0 + width], preferred_element_type=F32)

    def rot(p, t0, near, far):
        c = tab_ref[:, t0:t0 + LANES]
        sa = tab_ref[:, t0 + LANES:t0 + 2 * LANES]
        sb = tab_ref[:, t0 + 2 * LANES:t0 + 3 * LANES]
        return p * c + pltpu.roll(p, LANES - near, 1) * sa + pltpu.roll(p, far, 1) * sb

    kw = RET_HEADS * RET_DK
    vw = RET_HEADS * RET_DV
    dw = DIFF_HEADS * HEAD_W
    c0 = 0
    p = mm(c0, kw)
    for hh in range(RET_HEADS):
        sl = slice(hh * LANES, (hh + 1) * LANES)
        rq_ref[:, sl] = rot(p[:, sl], 0, 1, 1).astype(BF16)
    c0 += kw
    p = mm(c0, kw)
    for hh in range(RET_HEADS):
        sl = slice(hh * LANES, (hh + 1) * LANES)
        rk_ref[:, sl] = (rot(p[:, sl], 0, 1, 1) * (RET_DK ** -0.5)).astype(BF16)
    c0 += kw
    rv_ref[...] = mm(c0, vw).astype(BF16)
    c0 += vw
    rg_ref[...] = mm(c0, vw).astype(BF16)
    c0 += vw
    half = ROT_DIM // 2
    p = mm(c0, dw)
    for hh in range(DIFF_HEADS):
        sl = slice(hh * LANES, (hh + 1) * LANES)
        dq_ref[:, sl] = (rot(p[:, sl], 3 * LANES, half, half) * Q_SCALE).astype(BF16)
    c0 += dw
    p = mm(c0, dw)
    for hh in range(DIFF_HEADS):
        sl = slice(hh * LANES, (hh + 1) * LANES)
        dk_ref[:, sl] = rot(p[:, sl], 3 * LANES, half, half)
    c0 += dw
    dv_ref[...] = mm(c0, dw)
    c0 += dw
    d = x.shape[1]
    ga_ref[...] = mm(c0, d).astype(BF16)
    c0 += d
    gb_ref[...] = mm(c0, d).astype(BF16)


def _proj(x, g, w_bf, tab, tm, tab_blocks):
    t, d = x.shape
    kw, vw, dw = RET_HEADS * RET_DK, RET_HEADS * RET_DV, DIFF_HEADS * HEAD_W
    widths = (kw, kw, vw, vw, dw, dw, dw, d, d)
    dtypes = (BF16, BF16, BF16, BF16, BF16, F32, F32, BF16, BF16)
    row = lambda i: (i, 0)
    const = lambda i: (0, 0)
    return pl.pallas_call(
        _proj_kernel,
        grid=(t // tm,),
        in_specs=[pl.BlockSpec((tm, d), row),
                  pl.BlockSpec((1, d), const),
                  pl.BlockSpec(w_bf.shape, const),
                  pl.BlockSpec((tm, tab.shape[1]), lambda i: (i % tab_blocks, 0))],
        out_specs=[pl.BlockSpec((tm, w), row) for w in widths],
        out_shape=[jax.ShapeDtypeStruct((t, w), dt) for w, dt in zip(widths, dtypes)],
        compiler_params=_cparams(("arbitrary",)),
        name="proj",
    )(x, g, w_bf, tab)


def _ret_log_gammas():
    return tuple(math.log(1.0 - 2.0 ** (-5.0 - hh)) for hh in range(RET_HEADS))


def _ret_kernel(q_ref, k_ref, v_ref, g_ref, s0_ref, y_ref, st_ref, decay_ref, *, cb, n_valid):
    @pl.when((pl.program_id(0) == 0) & (pl.program_id(1) == 0))
    def _():
        row = lax.broadcasted_iota(jnp.int32, (cb, cb), 0)
        col = lax.broadcasted_iota(jnp.int32, (cb, cb), 1)
        dist = (row - col).astype(F32)
        for hh, lg in enumerate(_ret_log_gammas()):
            decay_ref[hh] = jnp.where(dist >= 0, jnp.exp(lg * jnp.maximum(dist, 0.0)), 0.0)

    @pl.when(pl.program_id(1) == 0)
    def _():
        st_ref[...] = s0_ref[...]

    idx = lax.broadcasted_iota(jnp.int32, (cb, 1), 0).astype(F32)
    for hh, lg in enumerate(_ret_log_gammas()):
        q = q_ref[0, :, hh * RET_DK:(hh + 1) * RET_DK]
        k = k_ref[0, :, hh * RET_DK:(hh + 1) * RET_DK]
        v = v_ref[0, :, hh * RET_DV:(hh + 1) * RET_DV]
        g = g_ref[0, :, hh * RET_DV:(hh + 1) * RET_DV].astype(F32)
        st = st_ref[0, hh]
        s = lax.dot_general(q, k, (((1,), (1,)), ((), ())), preferred_element_type=F32) * decay_ref[hh]
        o = jnp.dot(s.astype(BF16), v, preferred_element_type=F32)
        o = o + jnp.dot(q, st.astype(BF16), preferred_element_type=F32) * jnp.exp(lg * (idx + 1.0))
        kw = (k.astype(F32) * jnp.exp(lg * (n_valid - 1.0 - idx))).astype(BF16)
        st_ref[0, hh] = math.exp(lg * n_valid) * st + lax.dot_general(
            kw, v, (((0,), (0,)), ((), ())), preferred_element_type=F32)
        on = o * lax.rsqrt(jnp.mean(o * o, axis=-1, keepdims=True) + EPS)
        y_ref[0, :, hh * RET_DV:(hh + 1) * RET_DV] = (g * jax.nn.sigmoid(g) * on).astype(BF16)


def _retention(q, k, v, g, s0, cb, n_valid):
    b, s, _ = q.shape
    s0_map = (lambda i, c: (i, 0, 0, 0)) if s0.shape[0] == b else (lambda i, c: (0, 0, 0, 0))
    blk = lambda w: pl.BlockSpec((1, cb, w), lambda i, c: (i, c, 0))
    st_block = (1, RET_HEADS, RET_DK, RET_DV)
    return pl.pallas_call(
        functools.partial(_ret_kernel, cb=cb, n_valid=n_valid),
        grid=(b, s // cb),
        in_specs=[blk(q.shape[2]), blk(k.shape[2]), blk(v.shape[2]), blk(g.shape[2]),
                  pl.BlockSpec(st_block, s0_map)],
        out_specs=[blk(v.shape[2]), pl.BlockSpec(st_block, lambda i, c: (i, 0, 0, 0))],
        out_shape=[jax.ShapeDtypeStruct(v.shape, BF16),
                   jax.ShapeDtypeStruct((b,) + st_block[1:], F32)],
        scratch_shapes=[pltpu.VMEM((RET_HEADS, cb, cb), F32)],
        compiler_params=_cparams(("arbitrary", "arbitrary")),
        name="retention",
    )(q, k, v, g, s0)


def _scores(qq, kk):
    return lax.dot_general(qq, kk, (((1,), (1,)), ((), ())), preferred_element_type=F32)


def _softmax_seed(scores, values):
    ms = [jnp.max(s, axis=-1, keepdims=True) for s in scores]
    ps = [jnp.exp2(s - m) for s, m in zip(scores, ms)]
    out = []
    for m, p, vv in zip(ms, ps, values):
        out += [m, jnp.sum(p, axis=-1, keepdims=True),
                jnp.dot(p.astype(BF16), vv, preferred_element_type=F32)]
    return out


def _softmax_step(carry, scores, values):
    m2s = [jnp.maximum(carry[3 * c], jnp.max(s, axis=-1, keepdims=True)) for c, s in enumerate(scores)]
    ps = [jnp.exp2(s - m2) for s, m2 in zip(scores, m2s)]
    out = []
    for c, (m2, p, vv) in enumerate(zip(m2s, ps, values)):
        m, l, acc = carry[3 * c:3 * c + 3]
        a = jnp.exp2(m - m2)
        out += [m2, a * l + jnp.sum(p, axis=-1, keepdims=True),
                a * acc + jnp.dot(p.astype(BF16), vv, preferred_element_type=F32)]
    return out


def _sub_norm(o1, o2, lam, gs, coef):
    o = o1 - lam * o2
    return o * lax.rsqrt(jnp.mean(o * o, axis=-1, keepdims=True) + EPS) * gs * coef


SUM_ROWS = 16
VT_ROWS = HEAD_W + SUM_ROWS


def _softmax_seed_t(scores_t, values_t):
    ms = [jnp.max(s, axis=0, keepdims=True) for s in scores_t]
    ps = [jnp.exp2(s - m) for s, m in zip(scores_t, ms)]
    out = []
    for m, p, vt in zip(ms, ps, values_t):
        out += [m, jnp.dot(vt, p.astype(BF16), preferred_element_type=F32)]
    return out


def _softmax_step_t(carry, scores_t, values_t):
    m2s = [jnp.maximum(carry[2 * c], jnp.max(s, axis=0, keepdims=True)) for c, s in enumerate(scores_t)]
    ps = [jnp.exp2(s - m2) for s, m2 in zip(scores_t, m2s)]
    out = []
    for c, (m2, p, vt) in enumerate(zip(m2s, ps, values_t)):
        m, acc = carry[2 * c:2 * c + 2]
        out += [m2, jnp.exp2(m - m2) * acc + jnp.dot(vt, p.astype(BF16), preferred_element_type=F32)]
    return out


def _attn_prompt_kernel(lam_ref, q_ref, km_ref, vm_ref, ke_ref, ve_ref, gst_ref, o_ref, kb_ref, vt_ref,
                        *, sq, tq, hp, n_ext, coef):
    heads = [slice(hh * HEAD_W, (hh + 1) * HEAD_W) for hh in range(hp)]
    vrows = [slice(hh * VT_ROWS, (hh + 1) * VT_ROWS) for hh in range(hp)]
    kb_ref[...] = km_ref[0].astype(BF16)
    ones = jnp.ones((SUM_ROWS, tq), BF16)
    for j in range(sq // tq):
        for hh, sl in enumerate(heads):
            vt_ref[j, hh * VT_ROWS:hh * VT_ROWS + HEAD_W, :] = vm_ref[0, j * tq:(j + 1) * tq, sl].T.astype(BF16)
            vt_ref[j, hh * VT_ROWS + HEAD_W:(hh + 1) * VT_ROWS, :] = ones
    ke = [ke_ref[0, :, sl].astype(BF16) for sl in heads]
    vet = [jnp.concatenate([ve_ref[0, :, sl].T.astype(BF16), jnp.ones((SUM_ROWS, ve_ref.shape[1]), BF16)], axis=0)
           for sl in heads]
    lam = lam_ref[0]
    lo = lax.broadcasted_iota(jnp.int32, (1, HEAD_W), 1) < DIFF_DH
    ext_ok = lax.broadcasted_iota(jnp.int32, (ke_ref.shape[1], 1), 0) < n_ext
    diag = (lax.broadcasted_iota(jnp.int32, (tq, tq), 0) // CHUNK
            <= lax.broadcasted_iota(jnp.int32, (tq, tq), 1) // CHUNK)

    def q_tile(qi, _):
        q0 = pl.multiple_of(qi * tq, tq)
        qs = []
        for sl in heads:
            q = q_ref[0, pl.ds(q0, tq), sl]
            zero = jnp.zeros_like(q)
            qs += [jnp.where(lo, q, zero), jnp.where(lo, zero, q)]

        ext_s = [jnp.where(ext_ok, _scores(ke[c // 2], qq), NEG) for c, qq in enumerate(qs)]
        carry = _softmax_seed_t(ext_s, [vet[c // 2] for c in range(len(qs))])

        def step(j, carry, mask=None):
            k0 = pl.multiple_of(j * tq, tq)
            ss = [_scores(kb_ref[pl.ds(k0, tq), heads[c // 2]], qq) for c, qq in enumerate(qs)]
            if mask is not None:
                ss = [jnp.where(mask, s, NEG) for s in ss]
            return tuple(_softmax_step_t(carry, ss, [vt_ref[j, vrows[c // 2], :] for c in range(len(qs))]))

        carry = lax.fori_loop(0, qi, step, tuple(carry))
        carry = step(qi, carry, diag)
        for hh, sl in enumerate(heads):
            _, a1, _, a2 = carry[4 * hh:4 * hh + 4]
            o = a1[:HEAD_W] / a1[HEAD_W:HEAD_W + 1] - lam * (a2[:HEAD_W] / a2[HEAD_W:HEAD_W + 1])
            on = o * lax.rsqrt(jnp.mean(o * o, axis=0, keepdims=True) + EPS) * gst_ref[...] * coef
            o_ref[0, pl.ds(q0, tq), sl] = on.T.astype(BF16)
        return 0

    lax.fori_loop(0, sq // tq, q_tile, 0)


def _attention_prompt(q, k_main, v_main, k_ext, v_ext, n_ext, lam, g_subln, coef, tq, hp):
    b, sq, hw = q.shape
    w = hp * HEAD_W
    bh = pl.BlockSpec((1, sq, w), lambda i, h: (i, 0, h))
    ext = pl.BlockSpec((1, k_ext.shape[1], w), lambda i, h: (0, 0, h))
    gst = jnp.broadcast_to(g_subln.reshape(HEAD_W, 1), (HEAD_W, tq))
    return pl.pallas_call(
        functools.partial(_attn_prompt_kernel, sq=sq, tq=tq, hp=hp, n_ext=n_ext, coef=coef),
        grid=(b, hw // w),
        in_specs=[pl.BlockSpec(memory_space=pltpu.SMEM), bh, bh, bh, ext, ext,
                  pl.BlockSpec((HEAD_W, tq), lambda i, h: (0, 0))],
        out_specs=bh,
        out_shape=jax.ShapeDtypeStruct(q.shape, BF16),
        scratch_shapes=[pltpu.VMEM((sq, w), BF16), pltpu.VMEM((sq // tq, hp * VT_ROWS, tq), BF16)],
        compiler_params=_cparams(("arbitrary", "arbitrary")),
        name="attention_prompt",
    )(lam, q, k_main, v_main, k_ext, v_ext, gst)


def _attn_sample_kernel(lam_ref, q_ref, kc_ref, vc_ref, ke_ref, ve_ref, gs_ref, o_ref, m_ref, l_ref, acc_ref,
                        *, ds, nh, n_ext, coef):
    t = pl.program_id(1)
    tkv = kc_ref.shape[1] // nh
    lo = lax.broadcasted_iota(jnp.int32, (1, HEAD_W), 1) < DIFF_DH
    ext_ok = lax.broadcasted_iota(jnp.int32, (1, ke_ref.shape[1]), 1) < n_ext
    heads = [slice(hh * HEAD_W, (hh + 1) * HEAD_W) for hh in range(nh)]

    def stacked_q(sl):
        q = q_ref[0, :, sl]
        zero = jnp.zeros_like(q)
        return jnp.concatenate([jnp.where(lo, q, zero), jnp.where(lo, zero, q)], axis=0)

    qs = [stacked_q(sl) for sl in heads]

    def write_stats(stats):
        for hh in range(nh):
            m_ref[hh], l_ref[hh], acc_ref[hh] = stats[3 * hh:3 * hh + 3]

    @pl.when(t == 0)
    def _():
        ext_s = [jnp.where(ext_ok, _scores(qq, ke_ref[0, :, sl].astype(BF16)), NEG) for qq, sl in zip(qs, heads)]
        write_stats(_softmax_seed(ext_s, [ve_ref[0, :, sl].astype(BF16) for sl in heads]))

    carry = []
    for hh in range(nh):
        carry += [m_ref[hh], l_ref[hh], acc_ref[hh]]
    ss = [_scores(qq, kc_ref[0, pl.ds(hh, tkv, stride=nh), :].astype(BF16)) for hh, qq in enumerate(qs)]
    write_stats(_softmax_step(carry, ss, [vc_ref[0, pl.ds(hh, tkv, stride=nh), :].astype(BF16) for hh in range(nh)]))

    @pl.when(t == pl.num_programs(1) - 1)
    def _():
        lam = lam_ref[0]
        for hh, sl in enumerate(heads):
            o = acc_ref[hh] / l_ref[hh]
            o_ref[0, :, sl] = _sub_norm(o[:ds], o[ds:], lam, gs_ref[...], coef).astype(BF16)


def _attention_sample(q, k_cache, v_cache, k_ext, v_ext, n_ext, lam, g_subln, coef, tkv):
    b, ds, hw = q.shape
    _, pl_len, nh, _ = k_cache.shape
    per_b = lambda rows: pl.BlockSpec((1, rows, hw), lambda i, t: (i, 0, 0))
    cache = pl.BlockSpec((1, tkv * nh, HEAD_W), lambda i, t: (i, t, 0))
    return pl.pallas_call(
        functools.partial(_attn_sample_kernel, ds=ds, nh=nh, n_ext=n_ext, coef=coef),
        grid=(b, pl_len // tkv),
        in_specs=[pl.BlockSpec(memory_space=pltpu.SMEM), per_b(ds), cache, cache,
                  per_b(k_ext.shape[1]), per_b(k_ext.shape[1]),
                  pl.BlockSpec((1, HEAD_W), lambda i, t: (0, 0))],
        out_specs=per_b(ds),
        out_shape=jax.ShapeDtypeStruct(q.shape, BF16),
        scratch_shapes=[pltpu.VMEM((nh, 2 * ds, 1), F32), pltpu.VMEM((nh, 2 * ds, 1), F32),
                        pltpu.VMEM((nh, 2 * ds, HEAD_W), F32)],
        compiler_params=_cparams(("arbitrary", "arbitrary")),
        name="attention_sample",
    )(lam, q, k_cache.reshape(b, pl_len * nh, HEAD_W), v_cache.reshape(b, pl_len * nh, HEAD_W),
      k_ext, v_ext, g_subln)


def _mix_kernel(x_ref, yr_ref, od_ref, ga_ref, gb_ref, wr_ref, wd_ref, wo_ref, gf_ref, wrt_ref, br_ref,
                x1_ref, h_ref, lgt_ref):
    ya = jnp.dot(yr_ref[...], wr_ref[...], preferred_element_type=F32)
    yb = jnp.dot(od_ref[...], wd_ref[...], preferred_element_type=F32)
    z = jax.nn.sigmoid(ga_ref[...].astype(F32)) * ya + jax.nn.sigmoid(gb_ref[...].astype(F32)) * yb
    x1 = x_ref[...] + jnp.dot(z.astype(BF16), wo_ref[...], preferred_element_type=F32)
    x1_ref[...] = x1
    h = x1 * lax.rsqrt(jnp.mean(x1 * x1, axis=-1, keepdims=True) + EPS) * gf_ref[...]
    _tile_rows_store(h_ref, h)
    lgt_ref[...] = _scores(wrt_ref[...], h.astype(BF16)) + br_ref[...]


def _mix_tail_kernel(*refs):
    th_ref, tlgt_ref = refs[11:13]
    h_ref, lgt_ref = refs[14:16]
    last = pl.num_programs(0) - 1

    @pl.when(pl.program_id(0) < last)
    def _():
        _mix_kernel(*refs[:11], *refs[13:])

    @pl.when(pl.program_id(0) == last)
    def _():
        h_ref[...] = th_ref[...]
        lgt_ref[...] = tlgt_ref[...]


def _mix(x, yr, od, ga, gb, wr, wd, wo, gf, wrt, br, tm, tail=None):
    t, d = x.shape
    ne = wrt.shape[0]
    nt = t // tm
    extra = 0 if tail is None else 1
    row = lambda w: pl.BlockSpec((tm, w), lambda i: (jnp.minimum(i, nt - 1), 0))
    full = lambda a: pl.BlockSpec(a.shape, lambda i: (0, 0))
    ops = (x, yr, od, ga, gb, wr, wd, wo, gf, wrt, br) + (() if tail is None else tuple(tail))
    return pl.pallas_call(
        _mix_kernel if tail is None else _mix_tail_kernel,
        grid=(nt + extra,),
        in_specs=[row(d), row(yr.shape[1]), row(od.shape[1]), row(d), row(d)] + [full(a) for a in ops[5:]],
        out_specs=[row(d), pl.BlockSpec((tm * SUBLANES, LANES), lambda i: (i, 0)),
                   pl.BlockSpec((ne, tm), lambda i: (0, i))],
        out_shape=[jax.ShapeDtypeStruct((t, d), F32),
                   jax.ShapeDtypeStruct(((t + extra * tm) * SUBLANES, LANES), F32),
                   jax.ShapeDtypeStruct((ne, t + extra * tm), F32)],
        compiler_params=_cparams(("arbitrary",)),
        name="mix",
    )(*ops)


def _route_kernel(lgt_ref, e_ref, gate_ref, rank_ref, cnt_ref):
    @pl.when(pl.program_id(0) == 0)
    def _():
        cnt_ref[...] = jnp.zeros_like(cnt_ref)

    v = lgt_ref[...]
    ne, tn = v.shape
    eid = lax.broadcasted_iota(jnp.int32, (ne, tn), 0)
    tops, sels = [], []
    for k in range(TOP_K):
        m = jnp.max(v, axis=0, keepdims=True)
        idx = jnp.min(jnp.where(v == m, eid, ne), axis=0, keepdims=True)
        sel = eid == idx
        e_ref[k:k + 1, :] = idx
        tops.append(m)
        sels.append(sel)
        v = jnp.where(sel, -jnp.inf, v)
    ex = [jnp.exp(m - tops[0]) for m in tops]
    den = ex[0] + ex[1] + ex[2] + ex[3]
    for k in range(TOP_K):
        gate_ref[k:k + 1, :] = ex[k] / den

    chosen = sels[0] | sels[1] | sels[2] | sels[3]
    before = (lax.broadcasted_iota(jnp.int32, (tn, tn), 0) < lax.broadcasted_iota(jnp.int32, (tn, tn), 1))
    prior = jnp.dot(chosen.astype(BF16), before.astype(BF16), preferred_element_type=F32) + cnt_ref[:, 0:1]
    for k in range(TOP_K):
        rank_ref[k:k + 1, :] = jnp.sum(jnp.where(sels[k], prior, 0.0), axis=0, keepdims=True).astype(jnp.int32)
    cnt_ref[...] = cnt_ref[...] + jnp.sum(chosen.astype(F32), axis=1, keepdims=True)


def _route_topk(lgt, tn):
    ne, t = lgt.shape
    kt = pl.BlockSpec((TOP_K, tn), lambda i: (0, i))
    return pl.pallas_call(
        _route_kernel,
        grid=(t // tn,),
        in_specs=[pl.BlockSpec((ne, tn), lambda i: (0, i))],
        out_specs=[kt, kt, kt, pl.BlockSpec((ne, LANES), lambda i: (0, 0))],
        out_shape=[jax.ShapeDtypeStruct((TOP_K, t), jnp.int32), jax.ShapeDtypeStruct((TOP_K, t), F32),
                   jax.ShapeDtypeStruct((TOP_K, t), jnp.int32), jax.ShapeDtypeStruct((ne, LANES), F32)],
        compiler_params=_cparams(("arbitrary",)),
        name="route",
    )(lgt)


GATHER_UNROLL = 8
MOE_GATHER_PRIORITIES = (1,)


def _gather_rows(src_hbm, idx_ref, n, buf_ref, slot, sem, priorities=(0, 1)):
    def body(g, _):
        for u in range(GATHER_UNROLL):
            r = g * GATHER_UNROLL + u
            src = pl.multiple_of(idx_ref[0, 0, r], SUBLANES)
            dst = pl.multiple_of(r * SUBLANES, SUBLANES)
            pltpu.make_async_copy(src_hbm.at[pl.ds(src, SUBLANES)], buf_ref.at[slot, pl.ds(dst, SUBLANES)],
                                  sem.at[slot]).start(priority=priorities[u % len(priorities)])
        return 0
    lax.fori_loop(0, n // GATHER_UNROLL, body, 0)


def _wait_rows(src_hbm, n, buf_ref, slot, sem):
    pltpu.make_async_copy(src_hbm.at[pl.ds(0, n * SUBLANES)], buf_ref.at[slot], sem.at[slot]).wait()


def _moe_kernel(be_ref, nu_ref, tok_ref, tokn_ref, h_hbm, wgu_ref, bgu_ref, wdn_ref, bdn_ref, out_ref,
                buf_ref, wgu_bf, wdn_bf, sem):
    i = pl.program_id(0)
    n_used = nu_ref[0]
    slot = i % 2
    d_ff = wdn_bf.shape[0]

    @pl.when(i == 0)
    def _():
        _gather_rows(h_hbm, tok_ref, MOE_ROWS, buf_ref, 0, sem, MOE_GATHER_PRIORITIES)

    @pl.when(i + 1 < n_used)
    def _():
        _gather_rows(h_hbm, tokn_ref, MOE_ROWS, buf_ref, 1 - slot, sem, MOE_GATHER_PRIORITIES)

    @pl.when(i < n_used)
    def _():
        @pl.when((i == 0) | (be_ref[i] != be_ref[jnp.maximum(i - 1, 0)]))
        def _():
            wgu_bf[...] = wgu_ref[0].astype(BF16)
            wdn_bf[...] = wdn_ref[0].astype(BF16)

        _wait_rows(h_hbm, MOE_ROWS, buf_ref, slot, sem)

        xb = _tile_rows_load(buf_ref, slot, 0, MOE_ROWS).astype(BF16)
        gu = jnp.dot(xb, wgu_bf[...], preferred_element_type=F32) + bgu_ref[0]
        gate = jnp.minimum(gu[:, :d_ff], SWIGLU_LIMIT)
        up = jnp.clip(gu[:, d_ff:], -SWIGLU_LIMIT, SWIGLU_LIMIT)
        act = (up + 1.0) * gate * jax.nn.sigmoid(SWIGLU_ALPHA * gate)
        _tile_rows_store(out_ref, jnp.dot(act.astype(BF16), wdn_bf[...], preferred_element_type=F32) + bdn_ref[0])

    @pl.when(i >= n_used)
    def _():
        out_ref[...] = jnp.zeros_like(out_ref)


def _moe(h, tok, block_e, n_used, w_gu, b_gu, w_down, b_down):
    d = h.shape[1] * SUBLANES
    nb = tok.shape[0]
    ne, _, two_ff = w_gu.shape
    d_ff = two_ff // 2
    e_map = lambda i, be, nu: (be[i], 0, 0)
    grid_spec = pltpu.PrefetchScalarGridSpec(
        num_scalar_prefetch=2,
        grid=(nb,),
        in_specs=[
            pl.BlockSpec((1, 1, MOE_ROWS), lambda i, be, nu: (i, 0, 0), memory_space=pltpu.SMEM),
            pl.BlockSpec((1, 1, MOE_ROWS), lambda i, be, nu: (jnp.minimum(i + 1, nb - 1), 0, 0),
                         memory_space=pltpu.SMEM),
            pl.BlockSpec(memory_space=pl.ANY),
            pl.BlockSpec((1, d, two_ff), e_map),
            pl.BlockSpec((1, 1, two_ff), e_map),
            pl.BlockSpec((1, d_ff, d), e_map),
            pl.BlockSpec((1, 1, d), e_map),
        ],
        out_specs=pl.BlockSpec((MOE_ROWS * SUBLANES, LANES), lambda i, be, nu: (i, 0)),
        scratch_shapes=[pltpu.VMEM((2, MOE_ROWS * SUBLANES, LANES), F32),
                        pltpu.VMEM((d, two_ff), BF16),
                        pltpu.VMEM((d_ff, d), BF16),
                        pltpu.SemaphoreType.DMA((2,))],
    )
    return pl.pallas_call(
        _moe_kernel,
        grid_spec=grid_spec,
        out_shape=jax.ShapeDtypeStruct((nb * MOE_ROWS * SUBLANES, LANES), F32),
        compiler_params=_cparams(("arbitrary",)),
        name="moe",
    )(block_e, n_used, tok, tok, h, w_gu, b_gu.reshape(ne, 1, two_ff), w_down, b_down.reshape(ne, 1, d))


def _combine_kernel(dst_ref, dstn_ref, x_ref, gate_ref, gf_ref, rows_hbm, y_ref, buf_ref, sem, *, tm, nsteps):
    i = pl.program_id(0)
    slot = i % 2
    n = tm * TOP_K

    @pl.when(i == 0)
    def _():
        _gather_rows(rows_hbm, dst_ref, n, buf_ref, 0, sem)

    @pl.when(i + 1 < nsteps)
    def _():
        _gather_rows(rows_hbm, dstn_ref, n, buf_ref, 1 - slot, sem)

    _wait_rows(rows_hbm, n, buf_ref, slot, sem)

    y = x_ref[...]
    gate = gate_ref[...]
    for k in range(TOP_K):
        y = y + gate[:, k:k + 1] * _tile_rows_load(buf_ref, slot, k * tm, tm)
    y_ref[...] = y * lax.rsqrt(jnp.mean(y * y, axis=-1, keepdims=True) + EPS) * gf_ref[...]


def _combine(x1, gates, dest, rows, g_final, tm):
    t, d = x1.shape
    nsteps = t // tm
    n = tm * TOP_K
    return pl.pallas_call(
        functools.partial(_combine_kernel, tm=tm, nsteps=nsteps),
        grid=(nsteps,),
        in_specs=[pl.BlockSpec((1, 1, n), lambda i: (i, 0, 0), memory_space=pltpu.SMEM),
                  pl.BlockSpec((1, 1, n), lambda i: (jnp.minimum(i + 1, nsteps - 1), 0, 0),
                               memory_space=pltpu.SMEM),
                  pl.BlockSpec((tm, d), lambda i: (i, 0)),
                  pl.BlockSpec((tm, TOP_K), lambda i: (i, 0)),
                  pl.BlockSpec((1, d), lambda i: (0, 0)),
                  pl.BlockSpec(memory_space=pl.ANY)],
        out_specs=pl.BlockSpec((tm, d), lambda i: (i, 0)),
        out_shape=jax.ShapeDtypeStruct((t, d), F32),
        scratch_shapes=[pltpu.VMEM((2, n * SUBLANES, LANES), F32), pltpu.SemaphoreType.DMA((2,))],
        compiler_params=_cparams(("arbitrary",)),
        name="combine",
    )(dest, dest, x1, gates, g_final, rows)


def _route(lgt, tn):
    t = lgt.shape[1]
    n_assign = t * TOP_K
    e4, gates, rank, cnt = _route_topk(lgt, tn)
    counts = cnt[:, 0].astype(jnp.int32)
    padded = (counts + MOE_ROWS - 1) // MOE_ROWS * MOE_ROWS
    pend = jnp.cumsum(padded)
    pstart = pend - padded
    start = jnp.cumsum(counts) - counts
    e_ids = jnp.arange(N_EXPERTS, dtype=jnp.int32)
    dest = rank + jnp.sum(jnp.where(e4[None] == e_ids[:, None, None], pstart[:, None, None], 0), axis=0)
    nb = n_assign // MOE_ROWS + N_EXPERTS
    n_used = (pend[-1] // MOE_ROWS).astype(jnp.int32).reshape(1)
    blk0 = jnp.arange(nb, dtype=jnp.int32) * MOE_ROWS
    block_e = jnp.minimum(jnp.sum((pend[None, :] <= blk0[:, None]).astype(jnp.int32), axis=1), N_EXPERTS - 1)
    of_block = lambda table: jnp.sum(jnp.where(block_e[:, None] == e_ids[None, :], table[None, :], 0), axis=1)
    order = jnp.argsort(e4.T.reshape(-1))
    within = (blk0 - of_block(pstart))[:, None] + jnp.arange(MOE_ROWS, dtype=jnp.int32)[None, :]
    src = jnp.clip(of_block(start)[:, None] + jnp.minimum(within, of_block(counts)[:, None] - 1), 0, n_assign - 1)
    tok = (order[src] // TOP_K).astype(jnp.int32)
    return gates.T, dest * SUBLANES, (tok * SUBLANES).reshape(nb, 1, MOE_ROWS), block_e, n_used


def kernel(x_prompt, x_sample, cache_k, cache_v, state_ret, meta_tokens, g_mix, w_in,
           lam_q1, lam_k1, lam_q2, lam_k2, g_subln, w_ret_out, w_diff_out, w_out,
           g_ffn, w_router, b_router, w_gu, b_gu, w_down, b_down, g_final):
    b, s, d = x_prompt.shape
    db, ds, _ = x_sample.shape
    depth, _, pl_len = cache_k.shape[:3]
    assert depth == 1, "single-layer step only"
    nm = meta_tokens.shape[0]
    hw = DIFF_HEADS * HEAD_W

    lam_init = 0.8 - 0.6 * math.exp(-0.3 * 0)
    lam = (jnp.exp(jnp.sum(lam_q1[0] * lam_k1[0])) - jnp.exp(jnp.sum(lam_q2[0] * lam_k2[0]))
           + lam_init).reshape(1).astype(F32)
    coef = 1.0 - lam_init

    w_in_bf = w_in[0].astype(BF16)
    g_mix2 = g_mix[0].reshape(1, d)

    tm = 512 if (b * s) % 512 == 0 else s
    tab_p = _tables(N_META + jnp.arange(s, dtype=jnp.int32))
    rq, rk, rv, rg, dq, dk, dv, ga, gb = _proj(x_prompt.reshape(b * s, d), g_mix2, w_in_bf, tab_p, tm, s // tm)

    pos_small = jnp.concatenate([jnp.tile(N_META + pl_len + jnp.arange(ds, dtype=jnp.int32), db),
                                 jnp.arange(nm, dtype=jnp.int32)])
    x_small = jnp.concatenate([x_sample.reshape(db * ds, d), meta_tokens.astype(F32)], axis=0)
    small = _proj(x_small, g_mix2, w_in_bf, _tables(pos_small), x_small.shape[0], 1)
    ns = db * ds
    srq, srk, srv, srg, sdq, sdk, sdv, sga, sgb = [a[:ns] for a in small]
    mrq, mrk, mrv, mrg, _, mdk, mdv, _, _ = [a[ns:] for a in small]

    cb_small = 128

    def pad_rows(a, nb_, n_):
        a = a.reshape(nb_, n_, a.shape[-1])
        return jnp.pad(a, ((0, 0), (0, cb_small - n_), (0, 0)))

    zero_state = jnp.zeros((1, RET_HEADS, RET_DK, RET_DV), F32)
    _, s_meta = _retention(pad_rows(mrq, 1, nm), pad_rows(mrk, 1, nm), pad_rows(mrv, 1, nm),
                           pad_rows(mrg, 1, nm), zero_state, cb_small, nm)
    cb = 256
    r3 = lambda a: a.reshape(b, s, a.shape[-1])
    yr, ret_p = _retention(r3(rq), r3(rk), r3(rv), r3(rg), s_meta, cb, cb)
    syr, ret_s = _retention(pad_rows(srq, db, ds), pad_rows(srk, db, ds), pad_rows(srv, db, ds),
                            pad_rows(srg, db, ds), state_ret[0], cb_small, ds)
    syr = syr[:, :ds].reshape(ns, -1)

    g_sub = g_subln[0].reshape(1, HEAD_W)
    ext_rows = 128
    pad_ext = lambda a: jnp.pad(a, ((0, 0), (0, ext_rows - a.shape[1]), (0, 0)))
    od = _attention_prompt(r3(dq), r3(dk), r3(dv), pad_ext(mdk[None]), pad_ext(mdv[None]), nm,
                           lam, g_sub, coef, 256, 4)
    sdk3, sdv3 = sdk.reshape(db, ds, hw), sdv.reshape(db, ds, hw)
    ke_s = jnp.concatenate([jnp.broadcast_to(mdk[None], (db, nm, hw)), sdk3], axis=1)
    ve_s = jnp.concatenate([jnp.broadcast_to(mdv[None], (db, nm, hw)), sdv3], axis=1)
    sod = _attention_sample(sdq.reshape(db, ds, hw), cache_k[0], cache_v[0], pad_ext(ke_s), pad_ext(ve_s),
                            nm + ds, lam, g_sub, coef, min(1024, pl_len))

    wr, wd, wo = w_ret_out[0].astype(BF16), w_diff_out[0].astype(BF16), w_out[0].astype(BF16)
    gf = g_ffn[0].reshape(1, d)
    wrt = w_router[0].T.astype(BF16)
    br = b_router[0].astype(F32).reshape(N_EXPERTS, 1)
    sx1, sh, slgt = _mix(x_sample.reshape(ns, d), syr, sod.reshape(ns, hw), sga, sgb, wr, wd, wo, gf, wrt, br, ns)
    x1, h, lgt = _mix(x_prompt.reshape(b * s, d), yr.reshape(b * s, -1), od.reshape(b * s, hw), ga, gb,
                      wr, wd, wo, gf, wrt, br, ns, (sh, slgt))

    t = b * s
    gates, dest, tok, block_e, n_used = _route(lgt, ns)
    rows = _moe(h, tok, block_e, n_used, w_gu[0], b_gu[0], w_down[0], b_down[0])
    gfin = g_final.reshape(1, d)

    def combine(x1g, lo, n):
        tc = min(256, n)
        dg = dest[:, lo:lo + n].reshape(TOP_K, n // tc, tc).swapaxes(0, 1).reshape(n // tc, 1, TOP_K * tc)
        return _combine(x1g, gates[lo:lo + n], dg, rows, gfin, tc)

    y_p = combine(x1, 0, t)
    y_s = combine(sx1, t, ns)

    k_p = jnp.concatenate([jnp.broadcast_to(mdk[None], (b, nm, hw)), r3(dk)], axis=1)
    v_p = jnp.concatenate([jnp.broadcast_to(mdv[None], (b, nm, hw)), r3(dv)], axis=1)
    shp = lambda a: a.reshape(1, a.shape[0], a.shape[1], DIFF_HEADS, HEAD_W)
    return (y_p.reshape(b, s, d), y_s.reshape(db, ds, d), ret_p[None], shp(k_p), shp(v_p),
            ret_s[None], shp(sdk3), shp(sdv3))
```

```python
import functools
import math

import jax
import jax.numpy as jnp
from jax import lax
from jax.experimental import pallas as pl
from jax.experimental.pallas import tpu as pltpu

F32 = jnp.float32
BF16 = jnp.bfloat16

EPS = 1e-6
N_META = 16
CHUNK = 64
RET_HEADS = 4
RET_DK = 128
RET_DV = 256
RET_THETA = 10000.0
DIFF_HEADS = 8
DIFF_DH = 64
ROT_DIM = DIFF_DH // 4
ROPE_THETA = 500000.0
N_EXPERTS = 32
TOP_K = 4
SWIGLU_LIMIT = 7.0
SWIGLU_ALPHA = 1.702

LANES = 128
SUBLANES = 8
HEAD_W = 2 * DIFF_DH
MOE_ROWS = 256
VMEM_LIMIT = 56 * 1024 * 1024
NEG = -1e30
Q_SCALE = DIFF_DH ** -0.5 * math.log2(math.e)


def _cparams(sem):
    return pltpu.CompilerParams(dimension_semantics=sem, vmem_limit_bytes=VMEM_LIMIT)


def _tile_rows_store(ref, val):
    n = val.shape[0]
    for c in range(SUBLANES):
        ref[pl.ds(c, n, stride=SUBLANES), :] = val[:, c * LANES:(c + 1) * LANES]


def _tile_rows_load(ref, slot, first, n):
    return jnp.concatenate([ref[slot, pl.ds(first * SUBLANES + c, n, stride=SUBLANES), :]
                            for c in range(SUBLANES)], axis=1)


def _ret_tables(pos):
    angle = RET_THETA ** (-jnp.linspace(0.0, 1.0, RET_DK // 2, dtype=F32))
    angle = jnp.repeat(angle, 2)
    ang = pos.astype(F32)[:, None] * angle[None, :]
    cos, sin = jnp.cos(ang), jnp.sin(ang)
    even = (jnp.arange(RET_DK) % 2 == 0)[None, :]
    return cos, jnp.where(even, -sin, 0.0), jnp.where(even, 0.0, sin)


def _rope_tables(pos):
    half = ROT_DIM // 2
    inv = ROPE_THETA ** (-jnp.arange(half, dtype=F32) * (2.0 / ROT_DIM))
    ang = pos.astype(F32)[:, None] * inv[None, :]
    cos, sin = jnp.cos(ang), jnp.sin(ang)
    jj = jnp.arange(HEAD_W) % DIFF_DH
    first = (jj < half)[None, :]
    second = ((jj >= half) & (jj < ROT_DIM))[None, :]
    cos_l = jnp.take(cos, jj % half, axis=1)
    sin_l = jnp.take(sin, jj % half, axis=1)
    c = jnp.where(first | second, cos_l, 1.0)
    sa = jnp.where(first, -sin_l, 0.0)
    sb = jnp.where(second, sin_l, 0.0)
    return c, sa, sb


def _tables(pos):
    return jnp.concatenate(_ret_tables(pos) + _rope_tables(pos), axis=1)


def _proj_kernel(x_ref, g_ref, w_ref, tab_ref, rq_ref, rk_ref, rv_ref, rg_ref,
                 dq_ref, dk_ref, dv_ref, ga_ref, gb_ref):
    x = x_ref[...]
    ms = jnp.mean(x * x, axis=-1, keepdims=True)
    h = (x * lax.rsqrt(ms + EPS) * g_ref[...]).astype(BF16)

    def mm(c0, width):
        return jnp.dot(h, w_ref[:, c0:c0 + width], preferred_element_type=F32)

    def rot(p, t0, near, far):
        c = tab_ref[:, t0:t0 + LANES]
        sa = tab_ref[:, t0 + LANES:t0 + 2 * LANES]
        sb = tab_ref[:, t0 + 2 * LANES:t0 + 3 * LANES]
        return p * c + pltpu.roll(p, LANES - near, 1) * sa + pltpu.roll(p, far, 1) * sb

    kw = RET_HEADS * RET_DK
    vw = RET_HEADS * RET_DV
    dw = DIFF_HEADS * HEAD_W
    c0 = 0
    p = mm(c0, kw)
    for hh in range(RET_HEADS):
        sl = slice(hh * LANES, (hh + 1) * LANES)
        rq_ref[:, sl] = rot(p[:, sl], 0, 1, 1).astype(BF16)
    c0 += kw
    p = mm(c0, kw)
    for hh in range(RET_HEADS):
        sl = slice(hh * LANES, (hh + 1) * LANES)
        rk_ref[:, sl] = (rot(p[:, sl], 0, 1, 1) * (RET_DK ** -0.5)).astype(BF16)
    c0 += kw
    rv_ref[...] = mm(c0, vw).astype(BF16)
    c0 += vw
    rg_ref[...] = mm(c0, vw).astype(BF16)
    c0 += vw
    half = ROT_DIM // 2
    p = mm(c0, dw)
    for hh in range(DIFF_HEADS):
        sl = slice(hh * LANES, (hh + 1) * LANES)
        dq_ref[:, sl] = (rot(p[:, sl], 3 * LANES, half, half) * Q_SCALE).astype(BF16)
    c0 += dw
    p = mm(c0, dw)
    for hh in range(DIFF_HEADS):
        sl = slice(hh * LANES, (hh + 1) * LANES)
        dk_ref[:, sl] = rot(p[:, sl], 3 * LANES, half, half)
    c0 += dw
    dv_ref[...] = mm(c0, dw)
    c0 += dw
    d = x.shape[1]
    ga_ref[...] = mm(c0, d).astype(BF16)
    c0 += d
    gb_ref[...] = mm(c0, d).astype(BF16)


def _proj(x, g, w_bf, tab, tm, tab_blocks):
    t, d = x.shape
    kw, vw, dw = RET_HEADS * RET_DK, RET_HEADS * RET_DV, DIFF_HEADS * HEAD_W
    widths = (kw, kw, vw, vw, dw, dw, dw, d, d)
    dtypes = (BF16, BF16, BF16, BF16, BF16, F32, F32, BF16, BF16)
    row = lambda i: (i, 0)
    const = lambda i: (0, 0)
    return pl.pallas_call(
        _proj_kernel,
        grid=(t // tm,),
        in_specs=[pl.BlockSpec((tm, d), row),
                  pl.BlockSpec((1, d), const),
                  pl.BlockSpec(w_bf.shape, const),
                  pl.BlockSpec((tm, tab.shape[1]), lambda i: (i % tab_blocks, 0))],
        out_specs=[pl.BlockSpec((tm, w), row) for w in widths],
        out_shape=[jax.ShapeDtypeStruct((t, w), dt) for w, dt in zip(widths, dtypes)],
        compiler_params=_cparams(("arbitrary",)),
        name="proj",
    )(x, g, w_bf, tab)


def _ret_log_gammas():
    return tuple(math.log(1.0 - 2.0 ** (-5.0 - hh)) for hh in range(RET_HEADS))


def _ret_kernel(q_ref, k_ref, v_ref, g_ref, s0_ref, y_ref, st_ref, decay_ref, *, cb, n_valid):
    @pl.when((pl.program_id(0) == 0) & (pl.program_id(1) == 0))
    def _():
        row = lax.broadcasted_iota(jnp.int32, (cb, cb), 0)
        col = lax.broadcasted_iota(jnp.int32, (cb, cb), 1)
        dist = (row - col).astype(F32)
        for hh, lg in enumerate(_ret_log_gammas()):
            decay_ref[hh] = jnp.where(dist >= 0, jnp.exp(lg * jnp.maximum(dist, 0.0)), 0.0)

    @pl.when(pl.program_id(1) == 0)
    def _():
        st_ref[...] = s0_ref[...]

    idx = lax.broadcasted_iota(jnp.int32, (cb, 1), 0).astype(F32)
    for hh, lg in enumerate(_ret_log_gammas()):
        q = q_ref[0, :, hh * RET_DK:(hh + 1) * RET_DK]
        k = k_ref[0, :, hh * RET_DK:(hh + 1) * RET_DK]
        v = v_ref[0, :, hh * RET_DV:(hh + 1) * RET_DV]
        g = g_ref[0, :, hh * RET_DV:(hh + 1) * RET_DV].astype(F32)
        st = st_ref[0, hh]
        s = lax.dot_general(q, k, (((1,), (1,)), ((), ())), preferred_element_type=F32) * decay_ref[hh]
        o = jnp.dot(s.astype(BF16), v, preferred_element_type=F32)
        o = o + jnp.dot(q, st.astype(BF16), preferred_element_type=F32) * jnp.exp(lg * (idx + 1.0))
        kw = (k.astype(F32) * jnp.exp(lg * (n_valid - 1.0 - idx))).astype(BF16)
        st_ref[0, hh] = math.exp(lg * n_valid) * st + lax.dot_general(
            kw, v, (((0,), (0,)), ((), ())), preferred_element_type=F32)
        on = o * lax.rsqrt(jnp.mean(o * o, axis=-1, keepdims=True) + EPS)
        y_ref[0, :, hh * RET_DV:(hh + 1) * RET_DV] = (g * jax.nn.sigmoid(g) * on).astype(BF16)


def _retention(q, k, v, g, s0, cb, n_valid):
    b, s, _ = q.shape
    s0_map = (lambda i, c: (i, 0, 0, 0)) if s0.shape[0] == b else (lambda i, c: (0, 0, 0, 0))
    blk = lambda w: pl.BlockSpec((1, cb, w), lambda i, c: (i, c, 0))
    st_block = (1, RET_HEADS, RET_DK, RET_DV)
    return pl.pallas_call(
        functools.partial(_ret_kernel, cb=cb, n_valid=n_valid),
        grid=(b, s // cb),
        in_specs=[blk(q.shape[2]), blk(k.shape[2]), blk(v.shape[2]), blk(g.shape[2]),
                  pl.BlockSpec(st_block, s0_map)],
        out_specs=[blk(v.shape[2]), pl.BlockSpec(st_block, lambda i, c: (i, 0, 0, 0))],
        out_shape=[jax.ShapeDtypeStruct(v.shape, BF16),
                   jax.ShapeDtypeStruct((b,) + st_block[1:], F32)],
        scratch_shapes=[pltpu.VMEM((RET_HEADS, cb, cb), F32)],
        compiler_params=_cparams(("arbitrary", "arbitrary")),
        name="retention",
    )(q, k, v, g, s0)


def _scores(qq, kk):
    return lax.dot_general(qq, kk, (((1,), (1,)), ((), ())), preferred_element_type=F32)


def _softmax_seed(scores, values):
    ms = [jnp.max(s, axis=-1, keepdims=True) for s in scores]
    ps = [jnp.exp2(s - m) for s, m in zip(scores, ms)]
    out = []
    for m, p, vv in zip(ms, ps, values):
        out += [m, jnp.sum(p, axis=-1, keepdims=True),
                jnp.dot(p.astype(BF16), vv, preferred_element_type=F32)]
    return out


def _softmax_step(carry, scores, values):
    m2s = [jnp.maximum(carry[3 * c], jnp.max(s, axis=-1, keepdims=True)) for c, s in enumerate(scores)]
    ps = [jnp.exp2(s - m2) for s, m2 in zip(scores, m2s)]
    out = []
    for c, (m2, p, vv) in enumerate(zip(m2s, ps, values)):
        m, l, acc = carry[3 * c:3 * c + 3]
        a = jnp.exp2(m - m2)
        out += [m2, a * l + jnp.sum(p, axis=-1, keepdims=True),
                a * acc + jnp.dot(p.astype(BF16), vv, preferred_element_type=F32)]
    return out


def _sub_norm(o1, o2, lam, gs, coef):
    o = o1 - lam * o2
    return o * lax.rsqrt(jnp.mean(o * o, axis=-1, keepdims=True) + EPS) * gs * coef


SUM_ROWS = 16
VT_ROWS = HEAD_W + SUM_ROWS


def _softmax_seed_t(scores_t, values_t):
    ms = [jnp.max(s, axis=0, keepdims=True) for s in scores_t]
    ps = [jnp.exp2(s - m) for s, m in zip(scores_t, ms)]
    out = []
    for m, p, vt in zip(ms, ps, values_t):
        out += [m, jnp.dot(vt, p.astype(BF16), preferred_element_type=F32)]
    return out


def _softmax_step_t(carry, scores_t, values_t):
    m2s = [jnp.maximum(carry[2 * c], jnp.max(s, axis=0, keepdims=True)) for c, s in enumerate(scores_t)]
    ps = [jnp.exp2(s - m2) for s, m2 in zip(scores_t, m2s)]
    out = []
    for c, (m2, p, vt) in enumerate(zip(m2s, ps, values_t)):
        m, acc = carry[2 * c:2 * c + 2]
        out += [m2, jnp.exp2(m - m2) * acc + jnp.dot(vt, p.astype(BF16), preferred_element_type=F32)]
    return out


def _attn_prompt_kernel(lam_ref, q_ref, km_ref, vm_ref, ke_ref, ve_ref, gst_ref, o_ref, kb_ref, vt_ref,
                        *, sq, tq, hp, n_ext, coef):
    heads = [slice(hh * HEAD_W, (hh + 1) * HEAD_W) for hh in range(hp)]
    vrows = [slice(hh * VT_ROWS, (hh + 1) * VT_ROWS) for hh in range(hp)]
    kb_ref[...] = km_ref[0].astype(BF16)
    ones = jnp.ones((SUM_ROWS, tq), BF16)
    for j in range(sq // tq):
        for hh, sl in enumerate(heads):
            vt_ref[j, hh * VT_ROWS:hh * VT_ROWS + HEAD_W, :] = vm_ref[0, j * tq:(j + 1) * tq, sl].T.astype(BF16)
            vt_ref[j, hh * VT_ROWS + HEAD_W:(hh + 1) * VT_ROWS, :] = ones
    ke = [ke_ref[0, :, sl].astype(BF16) for sl in heads]
    vet = [jnp.concatenate([ve_ref[0, :, sl].T.astype(BF16), jnp.ones((SUM_ROWS, ve_ref.shape[1]), BF16)], axis=0)
           for sl in heads]
    lam = lam_ref[0]
    lo = lax.broadcasted_iota(jnp.int32, (1, HEAD_W), 1) < DIFF_DH
    ext_ok = lax.broadcasted_iota(jnp.int32, (ke_ref.shape[1], 1), 0) < n_ext
    diag = (lax.broadcasted_iota(jnp.int32, (tq, tq), 0) // CHUNK
            <= lax.broadcasted_iota(jnp.int32, (tq, tq), 1) // CHUNK)

    def q_tile(qi, _):
        q0 = pl.multiple_of(qi * tq, tq)
        qs = []
        for sl in heads:
            q = q_ref[0, pl.ds(q0, tq), sl]
            zero = jnp.zeros_like(q)
            qs += [jnp.where(lo, q, zero), jnp.where(lo, zero, q)]

        ext_s = [jnp.where(ext_ok, _scores(ke[c // 2], qq), NEG) for c, qq in enumerate(qs)]
        carry = _softmax_seed_t(ext_s, [vet[c // 2] for c in range(len(qs))])

        def step(j, carry, mask=None):
            k0 = pl.multiple_of(j * tq, tq)
            ss = [_scores(kb_ref[pl.ds(k0, tq), heads[c // 2]], qq) for c, qq in enumerate(qs)]
            if mask is not None:
                ss = [jnp.where(mask, s, NEG) for s in ss]
            return tuple(_softmax_step_t(carry, ss, [vt_ref[j, vrows[c // 2], :] for c in range(len(qs))]))

        def step2(jj, carry):
            return step(2 * jj + 1, step(2 * jj, carry))

        carry = lax.fori_loop(0, qi // 2, step2, tuple(carry))
        carry = lax.fori_loop(2 * (qi // 2), qi, step, carry)
        carry = step(qi, carry, diag)
        for hh, sl in enumerate(heads):
            _, a1, _, a2 = carry[4 * hh:4 * hh + 4]
            o = a1[:HEAD_W] / a1[HEAD_W:HEAD_W + 1] - lam * (a2[:HEAD_W] / a2[HEAD_W:HEAD_W + 1])
            on = o * lax.rsqrt(jnp.mean(o * o, axis=0, keepdims=True) + EPS) * gst_ref[...] * coef
            o_ref[0, pl.ds(q0, tq), sl] = on.T.astype(BF16)
        return 0

    lax.fori_loop(0, sq // tq, q_tile, 0)


def _attention_prompt(q, k_main, v_main, k_ext, v_ext, n_ext, lam, g_subln, coef, tq, hp):
    b, sq, hw = q.shape
    w = hp * HEAD_W
    bh = pl.BlockSpec((1, sq, w), lambda i, h: (i, 0, h))
    ext = pl.BlockSpec((1, k_ext.shape[1], w), lambda i, h: (0, 0, h))
    gst = jnp.broadcast_to(g_subln.reshape(HEAD_W, 1), (HEAD_W, tq))
    return pl.pallas_call(
        functools.partial(_attn_prompt_kernel, sq=sq, tq=tq, hp=hp, n_ext=n_ext, coef=coef),
        grid=(b, hw // w),
        in_specs=[pl.BlockSpec(memory_space=pltpu.SMEM), bh, bh, bh, ext, ext,
                  pl.BlockSpec((HEAD_W, tq), lambda i, h: (0, 0))],
        out_specs=bh,
        out_shape=jax.ShapeDtypeStruct(q.shape, BF16),
        scratch_shapes=[pltpu.VMEM((sq, w), BF16), pltpu.VMEM((sq // tq, hp * VT_ROWS, tq), BF16)],
        compiler_params=_cparams(("arbitrary", "arbitrary")),
        name="attention_prompt",
    )(lam, q, k_main, v_main, k_ext, v_ext, gst)


def _attn_sample_kernel(lam_ref, q_ref, kc_ref, vc_ref, ke_ref, ve_ref, gs_ref, o_ref, m_ref, l_ref, acc_ref,
                        *, ds, nh, n_ext, coef):
    t = pl.program_id(1)
    tkv = kc_ref.shape[1] // nh
    lo = lax.broadcasted_iota(jnp.int32, (1, HEAD_W), 1) < DIFF_DH
    ext_ok = lax.broadcasted_iota(jnp.int32, (1, ke_ref.shape[1]), 1) < n_ext
    heads = [slice(hh * HEAD_W, (hh + 1) * HEAD_W) for hh in range(nh)]

    def stacked_q(sl):
        q = q_ref[0, :, sl]
        zero = jnp.zeros_like(q)
        return jnp.concatenate([jnp.where(lo, q, zero), jnp.where(lo, zero, q)], axis=0)

    qs = [stacked_q(sl) for sl in heads]

    def write_stats(stats):
        for hh in range(nh):
            m_ref[hh], l_ref[hh], acc_ref[hh] = stats[3 * hh:3 * hh + 3]

    @pl.when(t == 0)
    def _():
        ext_s = [jnp.where(ext_ok, _scores(qq, ke_ref[0, :, sl].astype(BF16)), NEG) for qq, sl in zip(qs, heads)]
        write_stats(_softmax_seed(ext_s, [ve_ref[0, :, sl].astype(BF16) for sl in heads]))

    carry = []
    for hh in range(nh):
        carry += [m_ref[hh], l_ref[hh], acc_ref[hh]]
    ss = [_scores(qq, kc_ref[0, pl.ds(hh, tkv, stride=nh), :].astype(BF16)) for hh, qq in enumerate(qs)]
    write_stats(_softmax_step(carry, ss, [vc_ref[0, pl.ds(hh, tkv, stride=nh), :].astype(BF16) for hh in range(nh)]))

    @pl.when(t == pl.num_programs(1) - 1)
    def _():
        lam = lam_ref[0]
        for hh, sl in enumerate(heads):
            o = acc_ref[hh] / l_ref[hh]
            o_ref[0, :, sl] = _sub_norm(o[:ds], o[ds:], lam, gs_ref[...], coef).astype(BF16)


def _attention_sample(q, k_cache, v_cache, k_ext, v_ext, n_ext, lam, g_subln, coef, tkv):
    b, ds, hw = q.shape
    _, pl_len, nh, _ = k_cache.shape
    per_b = lambda rows: pl.BlockSpec((1, rows, hw), lambda i, t: (i, 0, 0))
    cache = pl.BlockSpec((1, tkv * nh, HEAD_W), lambda i, t: (i, t, 0))
    return pl.pallas_call(
        functools.partial(_attn_sample_kernel, ds=ds, nh=nh, n_ext=n_ext, coef=coef),
        grid=(b, pl_len // tkv),
        in_specs=[pl.BlockSpec(memory_space=pltpu.SMEM), per_b(ds), cache, cache,
                  per_b(k_ext.shape[1]), per_b(k_ext.shape[1]),
                  pl.BlockSpec((1, HEAD_W), lambda i, t: (0, 0))],
        out_specs=per_b(ds),
        out_shape=jax.ShapeDtypeStruct(q.shape, BF16),
        scratch_shapes=[pltpu.VMEM((nh, 2 * ds, 1), F32), pltpu.VMEM((nh, 2 * ds, 1), F32),
                        pltpu.VMEM((nh, 2 * ds, HEAD_W), F32)],
        compiler_params=_cparams(("arbitrary", "arbitrary")),
        name="attention_sample",
    )(lam, q, k_cache.reshape(b, pl_len * nh, HEAD_W), v_cache.reshape(b, pl_len * nh, HEAD_W),
      k_ext, v_ext, g_subln)


def _mix_kernel(x_ref, yr_ref, od_ref, ga_ref, gb_ref, wr_ref, wd_ref, wo_ref, gf_ref, wrt_ref, br_ref,
                x1_ref, h_ref, lgt_ref):
    ya = jnp.dot(yr_ref[...], wr_ref[...], preferred_element_type=F32)
    yb = jnp.dot(od_ref[...], wd_ref[...], preferred_element_type=F32)
    z = jax.nn.sigmoid(ga_ref[...].astype(F32)) * ya + jax.nn.sigmoid(gb_ref[...].astype(F32)) * yb
    x1 = x_ref[...] + jnp.dot(z.astype(BF16), wo_ref[...], preferred_element_type=F32)
    x1_ref[...] = x1
    h = x1 * lax.rsqrt(jnp.mean(x1 * x1, axis=-1, keepdims=True) + EPS) * gf_ref[...]
    _tile_rows_store(h_ref, h)
    lgt_ref[...] = _scores(wrt_ref[...], h.astype(BF16)) + br_ref[...]


def _mix_tail_kernel(*refs):
    th_ref, tlgt_ref = refs[11:13]
    h_ref, lgt_ref = refs[14:16]
    last = pl.num_programs(0) - 1

    @pl.when(pl.program_id(0) < last)
    def _():
        _mix_kernel(*refs[:11], *refs[13:])

    @pl.when(pl.program_id(0) == last)
    def _():
        h_ref[...] = th_ref[...]
        lgt_ref[...] = tlgt_ref[...]


def _mix(x, yr, od, ga, gb, wr, wd, wo, gf, wrt, br, tm, tail=None):
    t, d = x.shape
    ne = wrt.shape[0]
    nt = t // tm
    extra = 0 if tail is None else 1
    row = lambda w: pl.BlockSpec((tm, w), lambda i: (jnp.minimum(i, nt - 1), 0))
    full = lambda a: pl.BlockSpec(a.shape, lambda i: (0, 0))
    ops = (x, yr, od, ga, gb, wr, wd, wo, gf, wrt, br) + (() if tail is None else tuple(tail))
    return pl.pallas_call(
        _mix_kernel if tail is None else _mix_tail_kernel,
        grid=(nt + extra,),
        in_specs=[row(d), row(yr.shape[1]), row(od.shape[1]), row(d), row(d)] + [full(a) for a in ops[5:]],
        out_specs=[row(d), pl.BlockSpec((tm * SUBLANES, LANES), lambda i: (i, 0)),
                   pl.BlockSpec((ne, tm), lambda i: (0, i))],
        out_shape=[jax.ShapeDtypeStruct((t, d), F32),
                   jax.ShapeDtypeStruct(((t + extra * tm) * SUBLANES, LANES), F32),
                   jax.ShapeDtypeStruct((ne, t + extra * tm), F32)],
        compiler_params=_cparams(("arbitrary",)),
        name="mix",
    )(*ops)


def _route_kernel(lgt_ref, e_ref, gate_ref, rank_ref, cnt_ref):
    @pl.when(pl.program_id(0) == 0)
    def _():
        cnt_ref[...] = jnp.zeros_like(cnt_ref)

    v = lgt_ref[...]
    ne, tn = v.shape
    eid = lax.broadcasted_iota(jnp.int32, (ne, tn), 0)
    tops, sels = [], []
    for k in range(TOP_K):
        m = jnp.max(v, axis=0, keepdims=True)
        idx = jnp.min(jnp.where(v == m, eid, ne), axis=0, keepdims=True)
        sel = eid == idx
        e_ref[k:k + 1, :] = idx
        tops.append(m)
        sels.append(sel)
        v = jnp.where(sel, -jnp.inf, v)
    ex = [jnp.exp(m - tops[0]) for m in tops]
    den = ex[0] + ex[1] + ex[2] + ex[3]
    for k in range(TOP_K):
        gate_ref[k:k + 1, :] = ex[k] / den

    chosen = sels[0] | sels[1] | sels[2] | sels[3]
    before = (lax.broadcasted_iota(jnp.int32, (tn, tn), 0) < lax.broadcasted_iota(jnp.int32, (tn, tn), 1))
    prior = jnp.dot(chosen.astype(BF16), before.astype(BF16), preferred_element_type=F32) + cnt_ref[:, 0:1]
    for k in range(TOP_K):
        rank_ref[k:k + 1, :] = jnp.sum(jnp.where(sels[k], prior, 0.0), axis=0, keepdims=True).astype(jnp.int32)
    cnt_ref[...] = cnt_ref[...] + jnp.sum(chosen.astype(F32), axis=1, keepdims=True)


def _route_topk(lgt, tn):
    ne, t = lgt.shape
    kt = pl.BlockSpec((TOP_K, tn), lambda i: (0, i))
    return pl.pallas_call(
        _route_kernel,
        grid=(t // tn,),
        in_specs=[pl.BlockSpec((ne, tn), lambda i: (0, i))],
        out_specs=[kt, kt, kt, pl.BlockSpec((ne, LANES), lambda i: (0, 0))],
        out_shape=[jax.ShapeDtypeStruct((TOP_K, t), jnp.int32), jax.ShapeDtypeStruct((TOP_K, t), F32),
                   jax.ShapeDtypeStruct((TOP_K, t), jnp.int32), jax.ShapeDtypeStruct((ne, LANES), F32)],
        compiler_params=_cparams(("arbitrary",)),
        name="route",
    )(lgt)


GATHER_UNROLL = 8
MOE_GATHER_PRIORITIES = (1,)


def _gather_rows(src_hbm, idx_ref, n, buf_ref, slot, sem, priorities=(0, 1)):
    def body(g, _):
        for u in range(GATHER_UNROLL):
            r = g * GATHER_UNROLL + u
            src = pl.multiple_of(idx_ref[0, 0, r], SUBLANES)
            dst = pl.multiple_of(r * SUBLANES, SUBLANES)
            pltpu.make_async_copy(src_hbm.at[pl.ds(src, SUBLANES)], buf_ref.at[slot, pl.ds(dst, SUBLANES)],
                                  sem.at[slot]).start(priority=priorities[u % len(priorities)])
        return 0
    lax.fori_loop(0, n // GATHER_UNROLL, body, 0)


def _wait_rows(src_hbm, n, buf_ref, slot, sem):
    pltpu.make_async_copy(src_hbm.at[pl.ds(0, n * SUBLANES)], buf_ref.at[slot], sem.at[slot]).wait()


MOE_SLOTS = 3


def _moe_kernel(be_ref, nu_ref, tok_ref, tok1_ref, tok2_ref, h_hbm, wgu_ref, bgu_ref, wdn_ref, bdn_ref, out_ref,
                buf_ref, wgu_bf, wdn_bf, sem):
    i = pl.program_id(0)
    n_used = nu_ref[0]
    slot = i % MOE_SLOTS
    d_ff = wdn_bf.shape[0]

    @pl.when(i == 0)
    def _():
        _gather_rows(h_hbm, tok_ref, MOE_ROWS, buf_ref, 0, sem, MOE_GATHER_PRIORITIES)

        @pl.when(1 < n_used)
        def _():
            _gather_rows(h_hbm, tok1_ref, MOE_ROWS, buf_ref, 1, sem, MOE_GATHER_PRIORITIES)

    @pl.when(i + 2 < n_used)
    def _():
        _gather_rows(h_hbm, tok2_ref, MOE_ROWS, buf_ref, (i + 2) % MOE_SLOTS, sem, MOE_GATHER_PRIORITIES)

    @pl.when(i < n_used)
    def _():
        @pl.when((i == 0) | (be_ref[i] != be_ref[jnp.maximum(i - 1, 0)]))
        def _():
            wgu_bf[...] = wgu_ref[0].astype(BF16)
            wdn_bf[...] = wdn_ref[0].astype(BF16)

        _wait_rows(h_hbm, MOE_ROWS, buf_ref, slot, sem)

        xb = _tile_rows_load(buf_ref, slot, 0, MOE_ROWS).astype(BF16)
        gu = jnp.dot(xb, wgu_bf[...], preferred_element_type=F32) + bgu_ref[0]
        gate = jnp.minimum(gu[:, :d_ff], SWIGLU_LIMIT)
        up = jnp.clip(gu[:, d_ff:], -SWIGLU_LIMIT, SWIGLU_LIMIT)
        act = (up + 1.0) * gate * jax.nn.sigmoid(SWIGLU_ALPHA * gate)
        _tile_rows_store(out_ref, jnp.dot(act.astype(BF16), wdn_bf[...], preferred_element_type=F32) + bdn_ref[0])

    @pl.when(i >= n_used)
    def _():
        out_ref[...] = jnp.zeros_like(out_ref)


def _moe(h, tok, block_e, n_used, w_gu, b_gu, w_down, b_down):
    d = h.shape[1] * SUBLANES
    nb = tok.shape[0]
    ne, _, two_ff = w_gu.shape
    d_ff = two_ff // 2
    e_map = lambda i, be, nu: (be[i], 0, 0)
    grid_spec = pltpu.PrefetchScalarGridSpec(
        num_scalar_prefetch=2,
        grid=(nb,),
        in_specs=[
            pl.BlockSpec((1, 1, MOE_ROWS), lambda i, be, nu: (i, 0, 0), memory_space=pltpu.SMEM),
            pl.BlockSpec((1, 1, MOE_ROWS), lambda i, be, nu: (jnp.minimum(i + 1, nb - 1), 0, 0),
                         memory_space=pltpu.SMEM),
            pl.BlockSpec((1, 1, MOE_ROWS), lambda i, be, nu: (jnp.minimum(i + 2, nb - 1), 0, 0),
                         memory_space=pltpu.SMEM),
            pl.BlockSpec(memory_space=pl.ANY),
            pl.BlockSpec((1, d, two_ff), e_map),
            pl.BlockSpec((1, 1, two_ff), e_map),
            pl.BlockSpec((1, d_ff, d), e_map),
            pl.BlockSpec((1, 1, d), e_map),
        ],
        out_specs=pl.BlockSpec((MOE_ROWS * SUBLANES, LANES), lambda i, be, nu: (i, 0)),
        scratch_shapes=[pltpu.VMEM((MOE_SLOTS, MOE_ROWS * SUBLANES, LANES), F32),
                        pltpu.VMEM((d, two_ff), BF16),
                        pltpu.VMEM((d_ff, d), BF16),
                        pltpu.SemaphoreType.DMA((MOE_SLOTS,))],
    )
    return pl.pallas_call(
        _moe_kernel,
        grid_spec=grid_spec,
        out_shape=jax.ShapeDtypeStruct((nb * MOE_ROWS * SUBLANES, LANES), F32),
        compiler_params=_cparams(("arbitrary",)),
        name="moe",
    )(block_e, n_used, tok, tok, tok, h, w_gu, b_gu.reshape(ne, 1, two_ff), w_down, b_down.reshape(ne, 1, d))


def _combine_kernel(dst_ref, dstn_ref, x_ref, gate_ref, gf_ref, rows_hbm, y_ref, buf_ref, sem, *, tm, nsteps):
    i = pl.program_id(0)
    slot = i % 2
    n = tm * TOP_K

    @pl.when(i == 0)
    def _():
        _gather_rows(rows_hbm, dst_ref, n, buf_ref, 0, sem)

    @pl.when(i + 1 < nsteps)
    def _():
        _gather_rows(rows_hbm, dstn_ref, n, buf_ref, 1 - slot, sem)

    _wait_rows(rows_hbm, n, buf_ref, slot, sem)

    y = x_ref[...]
    gate = gate_ref[...]
    for k in range(TOP_K):
        y = y + gate[:, k:k + 1] * _tile_rows_load(buf_ref, slot, k * tm, tm)
    y_ref[...] = y * lax.rsqrt(jnp.mean(y * y, axis=-1, keepdims=True) + EPS) * gf_ref[...]


def _combine(x1, gates, dest, rows, g_final, tm):
    t, d = x1.shape
    nsteps = t // tm
    n = tm * TOP_K
    return pl.pallas_call(
        functools.partial(_combine_kernel, tm=tm, nsteps=nsteps),
        grid=(nsteps,),
        in_specs=[pl.BlockSpec((1, 1, n), lambda i: (i, 0, 0), memory_space=pltpu.SMEM),
                  pl.BlockSpec((1, 1, n), lambda i: (jnp.minimum(i + 1, nsteps - 1), 0, 0),
                               memory_space=pltpu.SMEM),
                  pl.BlockSpec((tm, d), lambda i: (i, 0)),
                  pl.BlockSpec((tm, TOP_K), lambda i: (i, 0)),
                  pl.BlockSpec((1, d), lambda i: (0, 0)),
                  pl.BlockSpec(memory_space=pl.ANY)],
        out_specs=pl.BlockSpec((tm, d), lambda i: (i, 0)),
        out_shape=jax.ShapeDtypeStruct((t, d), F32),
        scratch_shapes=[pltpu.VMEM((2, n * SUBLANES, LANES), F32), pltpu.SemaphoreType.DMA((2,))],
        compiler_params=_cparams(("arbitrary",)),
        name="combine",
    )(dest, dest, x1, gates, g_final, rows)


def _route(lgt, tn):
    t = lgt.shape[1]
    n_assign = t * TOP_K
    e4, gates, rank, cnt = _route_topk(lgt, tn)
    counts = cnt[:, 0].astype(jnp.int32)
    padded = (counts + MOE_ROWS - 1) // MOE_ROWS * MOE_ROWS
    pend = jnp.cumsum(padded)
    pstart = pend - padded
    start = jnp.cumsum(counts) - counts
    e_ids = jnp.arange(N_EXPERTS, dtype=jnp.int32)
    dest = rank + jnp.sum(jnp.where(e4[None] == e_ids[:, None, None], pstart[:, None, None], 0), axis=0)
    nb = n_assign // MOE_ROWS + N_EXPERTS
    n_used = (pend[-1] // MOE_ROWS).astype(jnp.int32).reshape(1)
    blk0 = jnp.arange(nb, dtype=jnp.int32) * MOE_ROWS
    block_e = jnp.minimum(jnp.sum((pend[None, :] <= blk0[:, None]).astype(jnp.int32), axis=1), N_EXPERTS - 1)
    of_block = lambda table: jnp.sum(jnp.where(block_e[:, None] == e_ids[None, :], table[None, :], 0), axis=1)
    order = jnp.argsort(e4.T.reshape(-1))
    within = (blk0 - of_block(pstart))[:, None] + jnp.arange(MOE_ROWS, dtype=jnp.int32)[None, :]
    src = jnp.clip(of_block(start)[:, None] + jnp.minimum(within, of_block(counts)[:, None] - 1), 0, n_assign - 1)
    tok = (order[src] // TOP_K).astype(jnp.int32)
    return gates.T, dest * SUBLANES, (tok * SUBLANES).reshape(nb, 1, MOE_ROWS), block_e, n_used


def kernel(x_prompt, x_sample, cache_k, cache_v, state_ret, meta_tokens, g_mix, w_in,
           lam_q1, lam_k1, lam_q2, lam_k2, g_subln, w_ret_out, w_diff_out, w_out,
           g_ffn, w_router, b_router, w_gu, b_gu, w_down, b_down, g_final):
    b, s, d = x_prompt.shape
    db, ds, _ = x_sample.shape
    depth, _, pl_len = cache_k.shape[:3]
    assert depth == 1, "single-layer step only"
    nm = meta_tokens.shape[0]
    hw = DIFF_HEADS * HEAD_W

    lam_init = 0.8 - 0.6 * math.exp(-0.3 * 0)
    lam = (jnp.exp(jnp.sum(lam_q1[0] * lam_k1[0])) - jnp.exp(jnp.sum(lam_q2[0] * lam_k2[0]))
           + lam_init).reshape(1).astype(F32)
    coef = 1.0 - lam_init

    w_in_bf = w_in[0].astype(BF16)
    g_mix2 = g_mix[0].reshape(1, d)

    tm = 512 if (b * s) % 512 == 0 else s
    tab_p = _tables(N_META + jnp.arange(s, dtype=jnp.int32))
    rq, rk, rv, rg, dq, dk, dv, ga, gb = _proj(x_prompt.reshape(b * s, d), g_mix2, w_in_bf, tab_p, tm, s // tm)

    pos_small = jnp.concatenate([jnp.tile(N_META + pl_len + jnp.arange(ds, dtype=jnp.int32), db),
                                 jnp.arange(nm, dtype=jnp.int32)])
    x_small = jnp.concatenate([x_sample.reshape(db * ds, d), meta_tokens.astype(F32)], axis=0)
    small = _proj(x_small, g_mix2, w_in_bf, _tables(pos_small), x_small.shape[0], 1)
    ns = db * ds
    srq, srk, srv, srg, sdq, sdk, sdv, sga, sgb = [a[:ns] for a in small]
    mrq, mrk, mrv, mrg, _, mdk, mdv, _, _ = [a[ns:] for a in small]

    cb_small = 128

    def pad_rows(a, nb_, n_):
        a = a.reshape(nb_, n_, a.shape[-1])
        return jnp.pad(a, ((0, 0), (0, cb_small - n_), (0, 0)))

    zero_state = jnp.zeros((1, RET_HEADS, RET_DK, RET_DV), F32)
    _, s_meta = _retention(pad_rows(mrq, 1, nm), pad_rows(mrk, 1, nm), pad_rows(mrv, 1, nm),
                           pad_rows(mrg, 1, nm), zero_state, cb_small, nm)
    cb = 256
    r3 = lambda a: a.reshape(b, s, a.shape[-1])
    yr, ret_p = _retention(r3(rq), r3(rk), r3(rv), r3(rg), s_meta, cb, cb)
    syr, ret_s = _retention(pad_rows(srq, db, ds), pad_rows(srk, db, ds), pad_rows(srv, db, ds),
                            pad_rows(srg, db, ds), state_ret[0], cb_small, ds)
    syr = syr[:, :ds].reshape(ns, -1)

    g_sub = g_subln[0].reshape(1, HEAD_W)
    ext_rows = 128
    pad_ext = lambda a: jnp.pad(a, ((0, 0), (0, ext_rows - a.shape[1]), (0, 0)))
    od = _attention_prompt(r3(dq), r3(dk), r3(dv), pad_ext(mdk[None]), pad_ext(mdv[None]), nm,
                           lam, g_sub, coef, 256, 4)
    sdk3, sdv3 = sdk.reshape(db, ds, hw), sdv.reshape(db, ds, hw)
    ke_s = jnp.concatenate([jnp.broadcast_to(mdk[None], (db, nm, hw)), sdk3], axis=1)
    ve_s = jnp.concatenate([jnp.broadcast_to(mdv[None], (db, nm, hw)), sdv3], axis=1)
    sod = _attention_sample(sdq.reshape(db, ds, hw), cache_k[0], cache_v[0], pad_ext(ke_s), pad_ext(ve_s),
                            nm + ds, lam, g_sub, coef, min(2048, pl_len))

    wr, wd, wo = w_ret_out[0].astype(BF16), w_diff_out[0].astype(BF16), w_out[0].astype(BF16)
    gf = g_ffn[0].reshape(1, d)
    wrt = w_router[0].T.astype(BF16)
    br = b_router[0].astype(F32).reshape(N_EXPERTS, 1)
    sx1, sh, slgt = _mix(x_sample.reshape(ns, d), syr, sod.reshape(ns, hw), sga, sgb, wr, wd, wo, gf, wrt, br, ns)
    x1, h, lgt = _mix(x_prompt.reshape(b * s, d), yr.reshape(b * s, -1), od.reshape(b * s, hw), ga, gb,
                      wr, wd, wo, gf, wrt, br, ns, (sh, slgt))

    t = b * s
    gates, dest, tok, block_e, n_used = _route(lgt, ns)
    rows = _moe(h, tok, block_e, n_used, w_gu[0], b_gu[0], w_down[0], b_down[0])
    gfin = g_final.reshape(1, d)

    def combine(x1g, lo, n):
        tc = min(256, n)
        dg = dest[:, lo:lo + n].reshape(TOP_K, n // tc, tc).swapaxes(0, 1).reshape(n // tc, 1, TOP_K * tc)
        return _combine(x1g, gates[lo:lo + n], dg, rows, gfin, tc)

    y_p = combine(x1, 0, t)
    y_s = combine(sx1, t, ns)

    k_p = jnp.concatenate([jnp.broadcast_to(mdk[None], (b, nm, hw)), r3(dk)], axis=1)
    v_p = jnp.concatenate([jnp.broadcast_to(mdv[None], (b, nm, hw)), r3(dv)], axis=1)
    shp = lambda a: a.reshape(1, a.shape[0], a.shape[1], DIFF_HEADS, HEAD_W)
    return (y_p.reshape(b, s, d), y_s.reshape(db, ds, d), ret_p[None], shp(k_p), shp(v_p),
            ret_s[None], shp(sdk3), shp(sdv3))
```

```python
import functools
import math

import jax
import jax.numpy as jnp
from jax import lax
from jax.experimental import pallas as pl
from jax.experimental.pallas import tpu as pltpu

F32 = jnp.float32
BF16 = jnp.bfloat16

EPS = 1e-6
N_META = 16
CHUNK = 64
RET_HEADS = 4
RET_DK = 128
RET_DV = 256
RET_THETA = 10000.0
DIFF_HEADS = 8
DIFF_DH = 64
ROT_DIM = DIFF_DH // 4
ROPE_THETA = 500000.0
N_EXPERTS = 32
TOP_K = 4
SWIGLU_LIMIT = 7.0
SWIGLU_ALPHA = 1.702

LANES = 128
SUBLANES = 8
HEAD_W = 2 * DIFF_DH
MOE_ROWS = 256
VMEM_LIMIT = 56 * 1024 * 1024
NEG = -1e30
Q_SCALE = DIFF_DH ** -0.5 * math.log2(math.e)


def _cparams(sem):
    return pltpu.CompilerParams(dimension_semantics=sem, vmem_limit_bytes=VMEM_LIMIT)


def _tile_rows_store(ref, val):
    n = val.shape[0]
    for c in range(SUBLANES):
        ref[pl.ds(c, n, stride=SUBLANES), :] = val[:, c * LANES:(c + 1) * LANES]


def _tile_rows_load(ref, slot, first, n):
    return jnp.concatenate([ref[slot, pl.ds(first * SUBLANES + c, n, stride=SUBLANES), :]
                            for c in range(SUBLANES)], axis=1)


def _ret_tables(pos):
    angle = RET_THETA ** (-jnp.linspace(0.0, 1.0, RET_DK // 2, dtype=F32))
    angle = jnp.repeat(angle, 2)
    ang = pos.astype(F32)[:, None] * angle[None, :]
    cos, sin = jnp.cos(ang), jnp.sin(ang)
    even = (jnp.arange(RET_DK) % 2 == 0)[None, :]
    return cos, jnp.where(even, -sin, 0.0), jnp.where(even, 0.0, sin)


def _rope_tables(pos):
    half = ROT_DIM // 2
    inv = ROPE_THETA ** (-jnp.arange(half, dtype=F32) * (2.0 / ROT_DIM))
    ang = pos.astype(F32)[:, None] * inv[None, :]
    cos, sin = jnp.cos(ang), jnp.sin(ang)
    jj = jnp.arange(HEAD_W) % DIFF_DH
    first = (jj < half)[None, :]
    second = ((jj >= half) & (jj < ROT_DIM))[None, :]
    cos_l = jnp.take(cos, jj % half, axis=1)
    sin_l = jnp.take(sin, jj % half, axis=1)
    c = jnp.where(first | second, cos_l, 1.0)
    sa = jnp.where(first, -sin_l, 0.0)
    sb = jnp.where(second, sin_l, 0.0)
    return c, sa, sb


def _tables(pos):
    return jnp.concatenate(_ret_tables(pos) + _rope_tables(pos), axis=1)


def _proj_kernel(x_ref, g_ref, w_ref, tab_ref, rq_ref, rk_ref, rv_ref, rg_ref,
                 dq_ref, dk_ref, dv_ref, ga_ref, gb_ref):
    x = x_ref[...]
    ms = jnp.mean(x * x, axis=-1, keepdims=True)
    h = (x * lax.rsqrt(ms + EPS) * g_ref[...]).astype(BF16)

    def mm(c0, width):
        return jnp.dot(h, w_ref[:, c0:c0 + width], preferred_element_type=F32)

    def rot(p, t0, near, far):
        c = tab_ref[:, t0:t0 + LANES]
        sa = tab_ref[:, t0 + LANES:t0 + 2 * LANES]
        sb = tab_ref[:, t0 + 2 * LANES:t0 + 3 * LANES]
        return p * c + pltpu.roll(p, LANES - near, 1) * sa + pltpu.roll(p, far, 1) * sb

    kw = RET_HEADS * RET_DK
    vw = RET_HEADS * RET_DV
    dw = DIFF_HEADS * HEAD_W
    c0 = 0
    p = mm(c0, kw)
    for hh in range(RET_HEADS):
        sl = slice(hh * LANES, (hh + 1) * LANES)
        rq_ref[:, sl] = rot(p[:, sl], 0, 1, 1).astype(BF16)
    c0 += kw
    p = mm(c0, kw)
    for hh in range(RET_HEADS):
        sl = slice(hh * LANES, (hh + 1) * LANES)
        rk_ref[:, sl] = (rot(p[:, sl], 0, 1, 1) * (RET_DK ** -0.5)).astype(BF16)
    c0 += kw
    rv_ref[...] = mm(c0, vw).astype(BF16)
    c0 += vw
    rg_ref[...] = mm(c0, vw).astype(BF16)
    c0 += vw
    half = ROT_DIM // 2
    p = mm(c0, dw)
    for hh in range(DIFF_HEADS):
        sl = slice(hh * LANES, (hh + 1) * LANES)
        dq_ref[:, sl] = (rot(p[:, sl], 3 * LANES, half, half) * Q_SCALE).astype(BF16)
    c0 += dw
    p = mm(c0, dw)
    for hh in range(DIFF_HEADS):
        sl = slice(hh * LANES, (hh + 1) * LANES)
        dk_ref[:, sl] = rot(p[:, sl], 3 * LANES, half, half)
    c0 += dw
    dv_ref[...] = mm(c0, dw)
    c0 += dw
    d = x.shape[1]
    ga_ref[...] = mm(c0, d).astype(BF16)
    c0 += d
    gb_ref[...] = mm(c0, d).astype(BF16)


def _proj(x, g, w_bf, tab, tm, tab_blocks):
    t, d = x.shape
    kw, vw, dw = RET_HEADS * RET_DK, RET_HEADS * RET_DV, DIFF_HEADS * HEAD_W
    widths = (kw, kw, vw, vw, dw, dw, dw, d, d)
    dtypes = (BF16, BF16, BF16, BF16, BF16, F32, F32, BF16, BF16)
    row = lambda i: (i, 0)
    const = lambda i: (0, 0)
    return pl.pallas_call(
        _proj_kernel,
        grid=(t // tm,),
        in_specs=[pl.BlockSpec((tm, d), row),
                  pl.BlockSpec((1, d), const),
                  pl.BlockSpec(w_bf.shape, const),
                  pl.BlockSpec((tm, tab.shape[1]), lambda i: (i % tab_blocks, 0))],
        out_specs=[pl.BlockSpec((tm, w), row) for w in widths],
        out_shape=[jax.ShapeDtypeStruct((t, w), dt) for w, dt in zip(widths, dtypes)],
        compiler_params=_cparams(("arbitrary",)),
        name="proj",
    )(x, g, w_bf, tab)


def _ret_log_gammas():
    return tuple(math.log(1.0 - 2.0 ** (-5.0 - hh)) for hh in range(RET_HEADS))


def _ret_kernel(q_ref, k_ref, v_ref, g_ref, s0_ref, y_ref, st_ref, decay_ref, *, cb, n_valid):
    @pl.when((pl.program_id(0) == 0) & (pl.program_id(1) == 0))
    def _():
        row = lax.broadcasted_iota(jnp.int32, (cb, cb), 0)
        col = lax.broadcasted_iota(jnp.int32, (cb, cb), 1)
        dist = (row - col).astype(F32)
        for hh, lg in enumerate(_ret_log_gammas()):
            decay_ref[hh] = jnp.where(dist >= 0, jnp.exp(lg * jnp.maximum(dist, 0.0)), 0.0)

    @pl.when(pl.program_id(1) == 0)
    def _():
        st_ref[...] = s0_ref[...]

    idx = lax.broadcasted_iota(jnp.int32, (cb, 1), 0).astype(F32)
    for hh, lg in enumerate(_ret_log_gammas()):
        q = q_ref[0, :, hh * RET_DK:(hh + 1) * RET_DK]
        k = k_ref[0, :, hh * RET_DK:(hh + 1) * RET_DK]
        v = v_ref[0, :, hh * RET_DV:(hh + 1) * RET_DV]
        g = g_ref[0, :, hh * RET_DV:(hh + 1) * RET_DV].astype(F32)
        st = st_ref[0, hh]
        s = lax.dot_general(q, k, (((1,), (1,)), ((), ())), preferred_element_type=F32) * decay_ref[hh]
        o = jnp.dot(s.astype(BF16), v, preferred_element_type=F32)
        o = o + jnp.dot(q, st.astype(BF16), preferred_element_type=F32) * jnp.exp(lg * (idx + 1.0))
        kw = (k.astype(F32) * jnp.exp(lg * (n_valid - 1.0 - idx))).astype(BF16)
        st_ref[0, hh] = math.exp(lg * n_valid) * st + lax.dot_general(
            kw, v, (((0,), (0,)), ((), ())), preferred_element_type=F32)
        on = o * lax.rsqrt(jnp.mean(o * o, axis=-1, keepdims=True) + EPS)
        y_ref[0, :, hh * RET_DV:(hh + 1) * RET_DV] = (g * jax.nn.sigmoid(g) * on).astype(BF16)


def _retention(q, k, v, g, s0, cb, n_valid):
    b, s, _ = q.shape
    s0_map = (lambda i, c: (i, 0, 0, 0)) if s0.shape[0] == b else (lambda i, c: (0, 0, 0, 0))
    blk = lambda w: pl.BlockSpec((1, cb, w), lambda i, c: (i, c, 0))
    st_block = (1, RET_HEADS, RET_DK, RET_DV)
    return pl.pallas_call(
        functools.partial(_ret_kernel, cb=cb, n_valid=n_valid),
        grid=(b, s // cb),
        in_specs=[blk(q.shape[2]), blk(k.shape[2]), blk(v.shape[2]), blk(g.shape[2]),
                  pl.BlockSpec(st_block, s0_map)],
        out_specs=[blk(v.shape[2]), pl.BlockSpec(st_block, lambda i, c: (i, 0, 0, 0))],
        out_shape=[jax.ShapeDtypeStruct(v.shape, BF16),
                   jax.ShapeDtypeStruct((b,) + st_block[1:], F32)],
        scratch_shapes=[pltpu.VMEM((RET_HEADS, cb, cb), F32)],
        compiler_params=_cparams(("arbitrary", "arbitrary")),
        name="retention",
    )(q, k, v, g, s0)


def _scores(qq, kk):
    return lax.dot_general(qq, kk, (((1,), (1,)), ((), ())), preferred_element_type=F32)


def _softmax_seed(scores, values):
    ms = [jnp.max(s, axis=-1, keepdims=True) for s in scores]
    ps = [jnp.exp2(s - m) for s, m in zip(scores, ms)]
    out = []
    for m, p, vv in zip(ms, ps, values):
        out += [m, jnp.sum(p, axis=-1, keepdims=True),
                jnp.dot(p.astype(BF16), vv, preferred_element_type=F32)]
    return out


def _softmax_step(carry, scores, values):
    m2s = [jnp.maximum(carry[3 * c], jnp.max(s, axis=-1, keepdims=True)) for c, s in enumerate(scores)]
    ps = [jnp.exp2(s - m2) for s, m2 in zip(scores, m2s)]
    out = []
    for c, (m2, p, vv) in enumerate(zip(m2s, ps, values)):
        m, l, acc = carry[3 * c:3 * c + 3]
        a = jnp.exp2(m - m2)
        out += [m2, a * l + jnp.sum(p, axis=-1, keepdims=True),
                a * acc + jnp.dot(p.astype(BF16), vv, preferred_element_type=F32)]
    return out


def _sub_norm(o1, o2, lam, gs, coef):
    o = o1 - lam * o2
    return o * lax.rsqrt(jnp.mean(o * o, axis=-1, keepdims=True) + EPS) * gs * coef


SUM_ROWS = 16
VT_ROWS = HEAD_W + SUM_ROWS


def _softmax_seed_t(scores_t, values_t):
    ms = [jnp.max(s, axis=0, keepdims=True) for s in scores_t]
    ps = [jnp.exp2(s - m) for s, m in zip(scores_t, ms)]
    out = []
    for m, p, vt in zip(ms, ps, values_t):
        out += [m, jnp.dot(vt, p.astype(BF16), preferred_element_type=F32)]
    return out


def _softmax_step_t(carry, scores_t, values_t):
    m2s = [jnp.maximum(carry[2 * c], jnp.max(s, axis=0, keepdims=True)) for c, s in enumerate(scores_t)]
    ps = [jnp.exp2(s - m2) for s, m2 in zip(scores_t, m2s)]
    out = []
    for c, (m2, p, vt) in enumerate(zip(m2s, ps, values_t)):
        m, acc = carry[2 * c:2 * c + 2]
        out += [m2, jnp.exp2(m - m2) * acc + jnp.dot(vt, p.astype(BF16), preferred_element_type=F32)]
    return out


def _attn_prompt_kernel(lam_ref, q_ref, km_ref, vm_ref, ke_ref, ve_ref, gst_ref, o_ref, kb_ref, vt_ref,
                        *, sq, tq, hp, n_ext, coef):
    heads = [slice(hh * HEAD_W, (hh + 1) * HEAD_W) for hh in range(hp)]
    vrows = [slice(hh * VT_ROWS, (hh + 1) * VT_ROWS) for hh in range(hp)]
    kb_ref[...] = km_ref[0].astype(BF16)
    ones = jnp.ones((SUM_ROWS, tq), BF16)
    for j in range(sq // tq):
        for hh, sl in enumerate(heads):
            vt_ref[j, hh * VT_ROWS:hh * VT_ROWS + HEAD_W, :] = vm_ref[0, j * tq:(j + 1) * tq, sl].T.astype(BF16)
            vt_ref[j, hh * VT_ROWS + HEAD_W:(hh + 1) * VT_ROWS, :] = ones
    ke = [ke_ref[0, :, sl].astype(BF16) for sl in heads]
    vet = [jnp.concatenate([ve_ref[0, :, sl].T.astype(BF16), jnp.ones((SUM_ROWS, ve_ref.shape[1]), BF16)], axis=0)
           for sl in heads]
    lam = lam_ref[0]
    lo = lax.broadcasted_iota(jnp.int32, (1, HEAD_W), 1) < DIFF_DH
    ext_ok = lax.broadcasted_iota(jnp.int32, (ke_ref.shape[1], 1), 0) < n_ext
    diag = (lax.broadcasted_iota(jnp.int32, (tq, tq), 0) // CHUNK
            <= lax.broadcasted_iota(jnp.int32, (tq, tq), 1) // CHUNK)

    def q_tile(qi):
        q0 = qi * tq
        qs = []
        for sl in heads:
            q = q_ref[0, pl.ds(q0, tq), sl]
            zero = jnp.zeros_like(q)
            qs += [jnp.where(lo, q, zero), jnp.where(lo, zero, q)]

        ext_s = [jnp.where(ext_ok, _scores(ke[c // 2], qq), NEG) for c, qq in enumerate(qs)]
        carry = _softmax_seed_t(ext_s, [vet[c // 2] for c in range(len(qs))])

        def step(j, carry, mask=None):
            ss = [_scores(kb_ref[j * tq:(j + 1) * tq, heads[c // 2]], qq) for c, qq in enumerate(qs)]
            if mask is not None:
                ss = [jnp.where(mask, s, NEG) for s in ss]
            return tuple(_softmax_step_t(carry, ss, [vt_ref[j, vrows[c // 2], :] for c in range(len(qs))]))

        carry = tuple(carry)
        for j in range(qi):
            carry = step(j, carry)
        carry = step(qi, carry, diag)
        for hh, sl in enumerate(heads):
            _, a1, _, a2 = carry[4 * hh:4 * hh + 4]
            o = a1[:HEAD_W] / a1[HEAD_W:HEAD_W + 1] - lam * (a2[:HEAD_W] / a2[HEAD_W:HEAD_W + 1])
            on = o * lax.rsqrt(jnp.mean(o * o, axis=0, keepdims=True) + EPS) * gst_ref[...] * coef
            o_ref[0, q0:q0 + tq, sl] = on.T.astype(BF16)

    for qi in range(sq // tq):
        pl.when(pl.program_id(0) >= 0)(functools.partial(q_tile, qi))


def _attention_prompt(q, k_main, v_main, k_ext, v_ext, n_ext, lam, g_subln, coef, tq, hp):
    b, sq, hw = q.shape
    w = hp * HEAD_W
    bh = pl.BlockSpec((1, sq, w), lambda i, h: (i, 0, h))
    ext = pl.BlockSpec((1, k_ext.shape[1], w), lambda i, h: (0, 0, h))
    gst = jnp.broadcast_to(g_subln.reshape(HEAD_W, 1), (HEAD_W, tq))
    return pl.pallas_call(
        functools.partial(_attn_prompt_kernel, sq=sq, tq=tq, hp=hp, n_ext=n_ext, coef=coef),
        grid=(b, hw // w),
        in_specs=[pl.BlockSpec(memory_space=pltpu.SMEM), bh, bh, bh, ext, ext,
                  pl.BlockSpec((HEAD_W, tq), lambda i, h: (0, 0))],
        out_specs=bh,
        out_shape=jax.ShapeDtypeStruct(q.shape, BF16),
        scratch_shapes=[pltpu.VMEM((sq, w), BF16), pltpu.VMEM((sq // tq, hp * VT_ROWS, tq), BF16)],
        compiler_params=_cparams(("arbitrary", "arbitrary")),
        name="attention_prompt",
    )(lam, q, k_main, v_main, k_ext, v_ext, gst)


def _attn_sample_kernel(lam_ref, q_ref, kc_ref, vc_ref, ke_ref, ve_ref, gs_ref, o_ref, m_ref, l_ref, acc_ref,
                        *, ds, nh, n_ext, coef):
    t = pl.program_id(1)
    tkv = kc_ref.shape[1] // nh
    lo = lax.broadcasted_iota(jnp.int32, (1, HEAD_W), 1) < DIFF_DH
    ext_ok = lax.broadcasted_iota(jnp.int32, (1, ke_ref.shape[1]), 1) < n_ext
    heads = [slice(hh * HEAD_W, (hh + 1) * HEAD_W) for hh in range(nh)]

    def stacked_q(sl):
        q = q_ref[0, :, sl]
        zero = jnp.zeros_like(q)
        return jnp.concatenate([jnp.where(lo, q, zero), jnp.where(lo, zero, q)], axis=0)

    qs = [stacked_q(sl) for sl in heads]

    def write_stats(stats):
        for hh in range(nh):
            m_ref[hh], l_ref[hh], acc_ref[hh] = stats[3 * hh:3 * hh + 3]

    @pl.when(t == 0)
    def _():
        ext_s = [jnp.where(ext_ok, _scores(qq, ke_ref[0, :, sl].astype(BF16)), NEG) for qq, sl in zip(qs, heads)]
        write_stats(_softmax_seed(ext_s, [ve_ref[0, :, sl].astype(BF16) for sl in heads]))

    carry = []
    for hh in range(nh):
        carry += [m_ref[hh], l_ref[hh], acc_ref[hh]]
    ss = [_scores(qq, kc_ref[0, pl.ds(hh, tkv, stride=nh), :].astype(BF16)) for hh, qq in enumerate(qs)]
    write_stats(_softmax_step(carry, ss, [vc_ref[0, pl.ds(hh, tkv, stride=nh), :].astype(BF16) for hh in range(nh)]))

    @pl.when(t == pl.num_programs(1) - 1)
    def _():
        lam = lam_ref[0]
        for hh, sl in enumerate(heads):
            o = acc_ref[hh] / l_ref[hh]
            o_ref[0, :, sl] = _sub_norm(o[:ds], o[ds:], lam, gs_ref[...], coef).astype(BF16)


def _attention_sample(q, k_cache, v_cache, k_ext, v_ext, n_ext, lam, g_subln, coef, tkv):
    b, ds, hw = q.shape
    _, pl_len, nh, _ = k_cache.shape
    per_b = lambda rows: pl.BlockSpec((1, rows, hw), lambda i, t: (i, 0, 0))
    cache = pl.BlockSpec((1, tkv * nh, HEAD_W), lambda i, t: (i, t, 0))
    return pl.pallas_call(
        functools.partial(_attn_sample_kernel, ds=ds, nh=nh, n_ext=n_ext, coef=coef),
        grid=(b, pl_len // tkv),
        in_specs=[pl.BlockSpec(memory_space=pltpu.SMEM), per_b(ds), cache, cache,
                  per_b(k_ext.shape[1]), per_b(k_ext.shape[1]),
                  pl.BlockSpec((1, HEAD_W), lambda i, t: (0, 0))],
        out_specs=per_b(ds),
        out_shape=jax.ShapeDtypeStruct(q.shape, BF16),
        scratch_shapes=[pltpu.VMEM((nh, 2 * ds, 1), F32), pltpu.VMEM((nh, 2 * ds, 1), F32),
                        pltpu.VMEM((nh, 2 * ds, HEAD_W), F32)],
        compiler_params=_cparams(("arbitrary", "arbitrary")),
        name="attention_sample",
    )(lam, q, k_cache.reshape(b, pl_len * nh, HEAD_W), v_cache.reshape(b, pl_len * nh, HEAD_W),
      k_ext, v_ext, g_subln)


def _mix_kernel(x_ref, yr_ref, od_ref, ga_ref, gb_ref, wr_ref, wd_ref, wo_ref, gf_ref, wrt_ref, br_ref,
                x1_ref, h_ref, lgt_ref):
    ya = jnp.dot(yr_ref[...], wr_ref[...], preferred_element_type=F32)
    yb = jnp.dot(od_ref[...], wd_ref[...], preferred_element_type=F32)
    z = jax.nn.sigmoid(ga_ref[...].astype(F32)) * ya + jax.nn.sigmoid(gb_ref[...].astype(F32)) * yb
    x1 = x_ref[...] + jnp.dot(z.astype(BF16), wo_ref[...], preferred_element_type=F32)
    x1_ref[...] = x1
    h = x1 * lax.rsqrt(jnp.mean(x1 * x1, axis=-1, keepdims=True) + EPS) * gf_ref[...]
    _tile_rows_store(h_ref, h)
    lgt_ref[...] = _scores(wrt_ref[...], h.astype(BF16)) + br_ref[...]


def _mix_tail_kernel(*refs):
    th_ref, tlgt_ref = refs[11:13]
    h_ref, lgt_ref = refs[14:16]
    last = pl.num_programs(0) - 1

    @pl.when(pl.program_id(0) < last)
    def _():
        _mix_kernel(*refs[:11], *refs[13:])

    @pl.when(pl.program_id(0) == last)
    def _():
        h_ref[...] = th_ref[...]
        lgt_ref[...] = tlgt_ref[...]


def _mix(x, yr, od, ga, gb, wr, wd, wo, gf, wrt, br, tm, tail=None):
    t, d = x.shape
    ne = wrt.shape[0]
    nt = t // tm
    extra = 0 if tail is None else 1
    row = lambda w: pl.BlockSpec((tm, w), lambda i: (jnp.minimum(i, nt - 1), 0))
    full = lambda a: pl.BlockSpec(a.shape, lambda i: (0, 0))
    ops = (x, yr, od, ga, gb, wr, wd, wo, gf, wrt, br) + (() if tail is None else tuple(tail))
    return pl.pallas_call(
        _mix_kernel if tail is None else _mix_tail_kernel,
        grid=(nt + extra,),
        in_specs=[row(d), row(yr.shape[1]), row(od.shape[1]), row(d), row(d)] + [full(a) for a in ops[5:]],
        out_specs=[row(d), pl.BlockSpec((tm * SUBLANES, LANES), lambda i: (i, 0)),
                   pl.BlockSpec((ne, tm), lambda i: (0, i))],
        out_shape=[jax.ShapeDtypeStruct((t, d), F32),
                   jax.ShapeDtypeStruct(((t + extra * tm) * SUBLANES, LANES), F32),
                   jax.ShapeDtypeStruct((ne, t + extra * tm), F32)],
        compiler_params=_cparams(("arbitrary",)),
        name="mix",
    )(*ops)


def _route_kernel(lgt_ref, e_ref, gate_ref, rank_ref, cnt_ref):
    @pl.when(pl.program_id(0) == 0)
    def _():
        cnt_ref[...] = jnp.zeros_like(cnt_ref)

    v = lgt_ref[...]
    ne, tn = v.shape
    eid = lax.broadcasted_iota(jnp.int32, (ne, tn), 0)
    tops, sels = [], []
    for k in range(TOP_K):
        m = jnp.max(v, axis=0, keepdims=True)
        idx = jnp.min(jnp.where(v == m, eid, ne), axis=0, keepdims=True)
        sel = eid == idx
        e_ref[k:k + 1, :] = idx
        tops.append(m)
        sels.append(sel)
        v = jnp.where(sel, -jnp.inf, v)
    ex = [jnp.exp(m - tops[0]) for m in tops]
    den = ex[0] + ex[1] + ex[2] + ex[3]
    for k in range(TOP_K):
        gate_ref[k:k + 1, :] = ex[k] / den

    chosen = sels[0] | sels[1] | sels[2] | sels[3]
    before = (lax.broadcasted_iota(jnp.int32, (tn, tn), 0) < lax.broadcasted_iota(jnp.int32, (tn, tn), 1))
    prior = jnp.dot(chosen.astype(BF16), before.astype(BF16), preferred_element_type=F32) + cnt_ref[:, 0:1]
    for k in range(TOP_K):
        rank_ref[k:k + 1, :] = jnp.sum(jnp.where(sels[k], prior, 0.0), axis=0, keepdims=True).astype(jnp.int32)
    cnt_ref[...] = cnt_ref[...] + jnp.sum(chosen.astype(F32), axis=1, keepdims=True)


def _route_topk(lgt, tn):
    ne, t = lgt.shape
    kt = pl.BlockSpec((TOP_K, tn), lambda i: (0, i))
    return pl.pallas_call(
        _route_kernel,
        grid=(t // tn,),
        in_specs=[pl.BlockSpec((ne, tn), lambda i: (0, i))],
        out_specs=[kt, kt, kt, pl.BlockSpec((ne, LANES), lambda i: (0, 0))],
        out_shape=[jax.ShapeDtypeStruct((TOP_K, t), jnp.int32), jax.ShapeDtypeStruct((TOP_K, t), F32),
                   jax.ShapeDtypeStruct((TOP_K, t), jnp.int32), jax.ShapeDtypeStruct((ne, LANES), F32)],
        compiler_params=_cparams(("arbitrary",)),
        name="route",
    )(lgt)


GATHER_UNROLL = 8
MOE_GATHER_PRIORITIES = (1,)


def _gather_rows(src_hbm, idx_ref, n, buf_ref, slot, sem, priorities=(0, 1)):
    def body(g, _):
        for u in range(GATHER_UNROLL):
            r = g * GATHER_UNROLL + u
            src = pl.multiple_of(idx_ref[0, 0, r], SUBLANES)
            dst = pl.multiple_of(r * SUBLANES, SUBLANES)
            pltpu.make_async_copy(src_hbm.at[pl.ds(src, SUBLANES)], buf_ref.at[slot, pl.ds(dst, SUBLANES)],
                                  sem.at[slot]).start(priority=priorities[u % len(priorities)])
        return 0
    lax.fori_loop(0, n // GATHER_UNROLL, body, 0)


def _wait_rows(src_hbm, n, buf_ref, slot, sem):
    pltpu.make_async_copy(src_hbm.at[pl.ds(0, n * SUBLANES)], buf_ref.at[slot], sem.at[slot]).wait()


MOE_SLOTS = 3


def _moe_kernel(be_ref, nu_ref, tok_ref, tok1_ref, tok2_ref, h_hbm, wgu_ref, bgu_ref, wdn_ref, bdn_ref, out_ref,
                buf_ref, wgu_bf, wdn_bf, sem):
    i = pl.program_id(0)
    n_used = nu_ref[0]
    slot = i % MOE_SLOTS
    d_ff = wdn_bf.shape[0]

    @pl.when(i == 0)
    def _():
        _gather_rows(h_hbm, tok_ref, MOE_ROWS, buf_ref, 0, sem, MOE_GATHER_PRIORITIES)

        @pl.when(1 < n_used)
        def _():
            _gather_rows(h_hbm, tok1_ref, MOE_ROWS, buf_ref, 1, sem, MOE_GATHER_PRIORITIES)

    @pl.when(i + 2 < n_used)
    def _():
        _gather_rows(h_hbm, tok2_ref, MOE_ROWS, buf_ref, (i + 2) % MOE_SLOTS, sem, MOE_GATHER_PRIORITIES)

    @pl.when(i < n_used)
    def _():
        @pl.when((i == 0) | (be_ref[i] != be_ref[jnp.maximum(i - 1, 0)]))
        def _():
            wgu_bf[...] = wgu_ref[0].astype(BF16)
            wdn_bf[...] = wdn_ref[0].astype(BF16)

        _wait_rows(h_hbm, MOE_ROWS, buf_ref, slot, sem)

        xb = _tile_rows_load(buf_ref, slot, 0, MOE_ROWS).astype(BF16)
        gu = jnp.dot(xb, wgu_bf[...], preferred_element_type=F32) + bgu_ref[0]
        gate = jnp.minimum(gu[:, :d_ff], SWIGLU_LIMIT)
        up = jnp.clip(gu[:, d_ff:], -SWIGLU_LIMIT, SWIGLU_LIMIT)
        act = (up + 1.0) * gate * jax.nn.sigmoid(SWIGLU_ALPHA * gate)
        _tile_rows_store(out_ref, jnp.dot(act.astype(BF16), wdn_bf[...], preferred_element_type=F32) + bdn_ref[0])

    @pl.when(i >= n_used)
    def _():
        out_ref[...] = jnp.zeros_like(out_ref)


def _moe(h, tok, block_e, n_used, w_gu, b_gu, w_down, b_down):
    d = h.shape[1] * SUBLANES
    nb = tok.shape[0]
    ne, _, two_ff = w_gu.shape
    d_ff = two_ff // 2
    e_map = lambda i, be, nu: (be[i], 0, 0)
    grid_spec = pltpu.PrefetchScalarGridSpec(
        num_scalar_prefetch=2,
        grid=(nb,),
        in_specs=[
            pl.BlockSpec((1, 1, MOE_ROWS), lambda i, be, nu: (i, 0, 0), memory_space=pltpu.SMEM),
            pl.BlockSpec((1, 1, MOE_ROWS), lambda i, be, nu: (jnp.minimum(i + 1, nb - 1), 0, 0),
                         memory_space=pltpu.SMEM),
            pl.BlockSpec((1, 1, MOE_ROWS), lambda i, be, nu: (jnp.minimum(i + 2, nb - 1), 0, 0),
                         memory_space=pltpu.SMEM),
            pl.BlockSpec(memory_space=pl.ANY),
            pl.BlockSpec((1, d, two_ff), e_map),
            pl.BlockSpec((1, 1, two_ff), e_map),
            pl.BlockSpec((1, d_ff, d), e_map),
            pl.BlockSpec((1, 1, d), e_map),
        ],
        out_specs=pl.BlockSpec((MOE_ROWS * SUBLANES, LANES), lambda i, be, nu: (i, 0)),
        scratch_shapes=[pltpu.VMEM((MOE_SLOTS, MOE_ROWS * SUBLANES, LANES), F32),
                        pltpu.VMEM((d, two_ff), BF16),
                        pltpu.VMEM((d_ff, d), BF16),
                        pltpu.SemaphoreType.DMA((MOE_SLOTS,))],
    )
    return pl.pallas_call(
        _moe_kernel,
        grid_spec=grid_spec,
        out_shape=jax.ShapeDtypeStruct((nb * MOE_ROWS * SUBLANES, LANES), F32),
        compiler_params=_cparams(("arbitrary",)),
        name="moe",
    )(block_e, n_used, tok, tok, tok, h, w_gu, b_gu.reshape(ne, 1, two_ff), w_down, b_down.reshape(ne, 1, d))


def _combine_kernel(dst_ref, dstn_ref, x_ref, gate_ref, gf_ref, rows_hbm, y_ref, buf_ref, sem, *, tm, nsteps):
    i = pl.program_id(0)
    slot = i % 2
    n = tm * TOP_K

    @pl.when(i == 0)
    def _():
        _gather_rows(rows_hbm, dst_ref, n, buf_ref, 0, sem)

    @pl.when(i + 1 < nsteps)
    def _():
        _gather_rows(rows_hbm, dstn_ref, n, buf_ref, 1 - slot, sem)

    _wait_rows(rows_hbm, n, buf_ref, slot, sem)

    y = x_ref[...]
    gate = gate_ref[...]
    for k in range(TOP_K):
        y = y + gate[:, k:k + 1] * _tile_rows_load(buf_ref, slot, k * tm, tm)
    y_ref[...] = y * lax.rsqrt(jnp.mean(y * y, axis=-1, keepdims=True) + EPS) * gf_ref[...]


def _combine(x1, gates, dest, rows, g_final, tm):
    t, d = x1.shape
    nsteps = t // tm
    n = tm * TOP_K
    return pl.pallas_call(
        functools.partial(_combine_kernel, tm=tm, nsteps=nsteps),
        grid=(nsteps,),
        in_specs=[pl.BlockSpec((1, 1, n), lambda i: (i, 0, 0), memory_space=pltpu.SMEM),
                  pl.BlockSpec((1, 1, n), lambda i: (jnp.minimum(i + 1, nsteps - 1), 0, 0),
                               memory_space=pltpu.SMEM),
                  pl.BlockSpec((tm, d), lambda i: (i, 0)),
                  pl.BlockSpec((tm, TOP_K), lambda i: (i, 0)),
                  pl.BlockSpec((1, d), lambda i: (0, 0)),
                  pl.BlockSpec(memory_space=pl.ANY)],
        out_specs=pl.BlockSpec((tm, d), lambda i: (i, 0)),
        out_shape=jax.ShapeDtypeStruct((t, d), F32),
        scratch_shapes=[pltpu.VMEM((2, n * SUBLANES, LANES), F32), pltpu.SemaphoreType.DMA((2,))],
        compiler_params=_cparams(("arbitrary",)),
        name="combine",
    )(dest, dest, x1, gates, g_final, rows)


def _route(lgt, tn):
    t = lgt.shape[1]
    n_assign = t * TOP_K
    e4, gates, rank, cnt = _route_topk(lgt, tn)
    counts = cnt[:, 0].astype(jnp.int32)
    padded = (counts + MOE_ROWS - 1) // MOE_ROWS * MOE_ROWS
    pend = jnp.cumsum(padded)
    pstart = pend - padded
    start = jnp.cumsum(counts) - counts
    e_ids = jnp.arange(N_EXPERTS, dtype=jnp.int32)
    dest = rank + jnp.sum(jnp.where(e4[None] == e_ids[:, None, None], pstart[:, None, None], 0), axis=0)
    nb = n_assign // MOE_ROWS + N_EXPERTS
    n_used = (pend[-1] // MOE_ROWS).astype(jnp.int32).reshape(1)
    blk0 = jnp.arange(nb, dtype=jnp.int32) * MOE_ROWS
    block_e = jnp.minimum(jnp.sum((pend[None, :] <= blk0[:, None]).astype(jnp.int32), axis=1), N_EXPERTS - 1)
    of_block = lambda table: jnp.sum(jnp.where(block_e[:, None] == e_ids[None, :], table[None, :], 0), axis=1)
    order = jnp.argsort(e4.T.reshape(-1))
    within = (blk0 - of_block(pstart))[:, None] + jnp.arange(MOE_ROWS, dtype=jnp.int32)[None, :]
    src = jnp.clip(of_block(start)[:, None] + jnp.minimum(within, of_block(counts)[:, None] - 1), 0, n_assign - 1)
    tok = (order[src] // TOP_K).astype(jnp.int32)
    return gates.T, dest * SUBLANES, (tok * SUBLANES).reshape(nb, 1, MOE_ROWS), block_e, n_used


def kernel(x_prompt, x_sample, cache_k, cache_v, state_ret, meta_tokens, g_mix, w_in,
           lam_q1, lam_k1, lam_q2, lam_k2, g_subln, w_ret_out, w_diff_out, w_out,
           g_ffn, w_router, b_router, w_gu, b_gu, w_down, b_down, g_final):
    b, s, d = x_prompt.shape
    db, ds, _ = x_sample.shape
    depth, _, pl_len = cache_k.shape[:3]
    assert depth == 1, "single-layer step only"
    nm = meta_tokens.shape[0]
    hw = DIFF_HEADS * HEAD_W

    lam_init = 0.8 - 0.6 * math.exp(-0.3 * 0)
    lam = (jnp.exp(jnp.sum(lam_q1[0] * lam_k1[0])) - jnp.exp(jnp.sum(lam_q2[0] * lam_k2[0]))
           + lam_init).reshape(1).astype(F32)
    coef = 1.0 - lam_init

    w_in_bf = w_in[0].astype(BF16)
    g_mix2 = g_mix[0].reshape(1, d)

    tm = 512 if (b * s) % 512 == 0 else s
    tab_p = _tables(N_META + jnp.arange(s, dtype=jnp.int32))
    rq, rk, rv, rg, dq, dk, dv, ga, gb = _proj(x_prompt.reshape(b * s, d), g_mix2, w_in_bf, tab_p, tm, s // tm)

    pos_small = jnp.concatenate([jnp.tile(N_META + pl_len + jnp.arange(ds, dtype=jnp.int32), db),
                                 jnp.arange(nm, dtype=jnp.int32)])
    x_small = jnp.concatenate([x_sample.reshape(db * ds, d), meta_tokens.astype(F32)], axis=0)
    small = _proj(x_small, g_mix2, w_in_bf, _tables(pos_small), x_small.shape[0], 1)
    ns = db * ds
    srq, srk, srv, srg, sdq, sdk, sdv, sga, sgb = [a[:ns] for a in small]
    mrq, mrk, mrv, mrg, _, mdk, mdv, _, _ = [a[ns:] for a in small]

    cb_small = 128

    def pad_rows(a, nb_, n_):
        a = a.reshape(nb_, n_, a.shape[-1])
        return jnp.pad(a, ((0, 0), (0, cb_small - n_), (0, 0)))

    zero_state = jnp.zeros((1, RET_HEADS, RET_DK, RET_DV), F32)
    _, s_meta = _retention(pad_rows(mrq, 1, nm), pad_rows(mrk, 1, nm), pad_rows(mrv, 1, nm),
                           pad_rows(mrg, 1, nm), zero_state, cb_small, nm)
    cb = 256
    r3 = lambda a: a.reshape(b, s, a.shape[-1])
    yr, ret_p = _retention(r3(rq), r3(rk), r3(rv), r3(rg), s_meta, cb, cb)
    syr, ret_s = _retention(pad_rows(srq, db, ds), pad_rows(srk, db, ds), pad_rows(srv, db, ds),
                            pad_rows(srg, db, ds), state_ret[0], cb_small, ds)
    syr = syr[:, :ds].reshape(ns, -1)

    g_sub = g_subln[0].reshape(1, HEAD_W)
    ext_rows = 128
    pad_ext = lambda a: jnp.pad(a, ((0, 0), (0, ext_rows - a.shape[1]), (0, 0)))
    od = _attention_prompt(r3(dq), r3(dk), r3(dv), pad_ext(mdk[None]), pad_ext(mdv[None]), nm,
                           lam, g_sub, coef, 256, 4)
    sdk3, sdv3 = sdk.reshape(db, ds, hw), sdv.reshape(db, ds, hw)
    ke_s = jnp.concatenate([jnp.broadcast_to(mdk[None], (db, nm, hw)), sdk3], axis=1)
    ve_s = jnp.concatenate([jnp.broadcast_to(mdv[None], (db, nm, hw)), sdv3], axis=1)
    sod = _attention_sample(sdq.reshape(db, ds, hw), cache_k[0], cache_v[0], pad_ext(ke_s), pad_ext(ve_s),
                            nm + ds, lam, g_sub, coef, min(2048, pl_len))

    wr, wd, wo = w_ret_out[0].astype(BF16), w_diff_out[0].astype(BF16), w_out[0].astype(BF16)
    gf = g_ffn[0].reshape(1, d)
    wrt = w_router[0].T.astype(BF16)
    br = b_router[0].astype(F32).reshape(N_EXPERTS, 1)
    sx1, sh, slgt = _mix(x_sample.reshape(ns, d), syr, sod.reshape(ns, hw), sga, sgb, wr, wd, wo, gf, wrt, br, ns)
    x1, h, lgt = _mix(x_prompt.reshape(b * s, d), yr.reshape(b * s, -1), od.reshape(b * s, hw), ga, gb,
                      wr, wd, wo, gf, wrt, br, ns, (sh, slgt))

    t = b * s
    gates, dest, tok, block_e, n_used = _route(lgt, ns)
    rows = _moe(h, tok, block_e, n_used, w_gu[0], b_gu[0], w_down[0], b_down[0])
    gfin = g_final.reshape(1, d)

    def combine(x1g, lo, n):
        tc = min(256, n)
        dg = dest[:, lo:lo + n].reshape(TOP_K, n // tc, tc).swapaxes(0, 1).reshape(n // tc, 1, TOP_K * tc)
        return _combine(x1g, gates[lo:lo + n], dg, rows, gfin, tc)

    y_p = combine(x1, 0, t)
    y_s = combine(sx1, t, ns)

    k_p = jnp.concatenate([jnp.broadcast_to(mdk[None], (b, nm, hw)), r3(dk)], axis=1)
    v_p = jnp.concatenate([jnp.broadcast_to(mdv[None], (b, nm, hw)), r3(dv)], axis=1)
    shp = lambda a: a.reshape(1, a.shape[0], a.shape[1], DIFF_HEADS, HEAD_W)
    return (y_p.reshape(b, s, d), y_s.reshape(db, ds, d), ret_p[None], shp(k_p), shp(v_p),
            ret_s[None], shp(sdk3), shp(sdv3))
```

```python
import functools
import math

import jax
import jax.numpy as jnp
from jax import lax
from jax.experimental import pallas as pl
from jax.experimental.pallas import tpu as pltpu

F32 = jnp.float32
BF16 = jnp.bfloat16

EPS = 1e-6
N_META = 16
CHUNK = 64
RET_HEADS = 4
RET_DK = 128
RET_DV = 256
RET_THETA = 10000.0
DIFF_HEADS = 8
DIFF_DH = 64
ROT_DIM = DIFF_DH // 4
ROPE_THETA = 500000.0
N_EXPERTS = 32
TOP_K = 4
SWIGLU_LIMIT = 7.0
SWIGLU_ALPHA = 1.702

LANES = 128
SUBLANES = 8
HEAD_W = 2 * DIFF_DH
MOE_ROWS = 512
VMEM_LIMIT = 56 * 1024 * 1024
NEG = -1e30
Q_SCALE = DIFF_DH ** -0.5 * math.log2(math.e)


def _cparams(sem):
    return pltpu.CompilerParams(dimension_semantics=sem, vmem_limit_bytes=VMEM_LIMIT)


def _tile_rows_store(ref, val):
    n = val.shape[0]
    for c in range(SUBLANES):
        ref[pl.ds(c, n, stride=SUBLANES), :] = val[:, c * LANES:(c + 1) * LANES]


def _tile_rows_load(ref, slot, first, n):
    return jnp.concatenate([ref[slot, pl.ds(first * SUBLANES + c, n, stride=SUBLANES), :]
                            for c in range(SUBLANES)], axis=1)


def _ret_tables(pos):
    angle = RET_THETA ** (-jnp.linspace(0.0, 1.0, RET_DK // 2, dtype=F32))
    angle = jnp.repeat(angle, 2)
    ang = pos.astype(F32)[:, None] * angle[None, :]
    cos, sin = jnp.cos(ang), jnp.sin(ang)
    even = (jnp.arange(RET_DK) % 2 == 0)[None, :]
    return cos, jnp.where(even, -sin, 0.0), jnp.where(even, 0.0, sin)


def _rope_tables(pos):
    half = ROT_DIM // 2
    inv = ROPE_THETA ** (-jnp.arange(half, dtype=F32) * (2.0 / ROT_DIM))
    ang = pos.astype(F32)[:, None] * inv[None, :]
    cos, sin = jnp.cos(ang), jnp.sin(ang)
    jj = jnp.arange(HEAD_W) % DIFF_DH
    first = (jj < half)[None, :]
    second = ((jj >= half) & (jj < ROT_DIM))[None, :]
    cos_l = jnp.take(cos, jj % half, axis=1)
    sin_l = jnp.take(sin, jj % half, axis=1)
    c = jnp.where(first | second, cos_l, 1.0)
    sa = jnp.where(first, -sin_l, 0.0)
    sb = jnp.where(second, sin_l, 0.0)
    return c, sa, sb


def _tables(pos):
    return jnp.concatenate(_ret_tables(pos) + _rope_tables(pos), axis=1)


def _proj_kernel(x_ref, g_ref, w_ref, tab_ref, rq_ref, rk_ref, rv_ref, rg_ref,
                 dq_ref, dk_ref, dv_ref, ga_ref, gb_ref):
    x = x_ref[...]
    ms = jnp.mean(x * x, axis=-1, keepdims=True)
    h = (x * lax.rsqrt(ms + EPS) * g_ref[...]).astype(BF16)

    def mm(c0, width):
        return jnp.dot(h, w_ref[:, c0:c0 + width], preferred_element_type=F32)

    def rot(p, t0, near, far):
        c = tab_ref[:, t0:t0 + LANES]
        sa = tab_ref[:, t0 + LANES:t0 + 2 * LANES]
        sb = tab_ref[:, t0 + 2 * LANES:t0 + 3 * LANES]
        return p * c + pltpu.roll(p, LANES - near, 1) * sa + pltpu.roll(p, far, 1) * sb

    kw = RET_HEADS * RET_DK
    vw = RET_HEADS * RET_DV
    dw = DIFF_HEADS * HEAD_W
    c0 = 0
    p = mm(c0, kw)
    for hh in range(RET_HEADS):
        sl = slice(hh * LANES, (hh + 1) * LANES)
        rq_ref[:, sl] = rot(p[:, sl], 0, 1, 1).astype(BF16)
    c0 += kw
    p = mm(c0, kw)
    for hh in range(RET_HEADS):
        sl = slice(hh * LANES, (hh + 1) * LANES)
        rk_ref[:, sl] = (rot(p[:, sl], 0, 1, 1) * (RET_DK ** -0.5)).astype(BF16)
    c0 += kw
    rv_ref[...] = mm(c0, vw).astype(BF16)
    c0 += vw
    rg_ref[...] = mm(c0, vw).astype(BF16)
    c0 += vw
    half = ROT_DIM // 2
    p = mm(c0, dw)
    for hh in range(DIFF_HEADS):
        sl = slice(hh * LANES, (hh + 1) * LANES)
        dq_ref[:, sl] = (rot(p[:, sl], 3 * LANES, half, half) * Q_SCALE).astype(BF16)
    c0 += dw
    p = mm(c0, dw)
    for hh in range(DIFF_HEADS):
        sl = slice(hh * LANES, (hh + 1) * LANES)
        dk_ref[:, sl] = rot(p[:, sl], 3 * LANES, half, half)
    c0 += dw
    dv_ref[...] = mm(c0, dw)
    c0 += dw
    d = x.shape[1]
    ga_ref[...] = mm(c0, d).astype(BF16)
    c0 += d
    gb_ref[...] = mm(c0, d).astype(BF16)


def _proj(x, g, w_bf, tab, tm, tab_blocks):
    t, d = x.shape
    kw, vw, dw = RET_HEADS * RET_DK, RET_HEADS * RET_DV, DIFF_HEADS * HEAD_W
    widths = (kw, kw, vw, vw, dw, dw, dw, d, d)
    dtypes = (BF16, BF16, BF16, BF16, BF16, F32, F32, BF16, BF16)
    row = lambda i: (i, 0)
    const = lambda i: (0, 0)
    return pl.pallas_call(
        _proj_kernel,
        grid=(t // tm,),
        in_specs=[pl.BlockSpec((tm, d), row),
                  pl.BlockSpec((1, d), const),
                  pl.BlockSpec(w_bf.shape, const),
                  pl.BlockSpec((tm, tab.shape[1]), lambda i: (i % tab_blocks, 0))],
        out_specs=[pl.BlockSpec((tm, w), row) for w in widths],
        out_shape=[jax.ShapeDtypeStruct((t, w), dt) for w, dt in zip(widths, dtypes)],
        compiler_params=_cparams(("arbitrary",)),
        name="proj",
    )(x, g, w_bf, tab)


def _ret_log_gammas():
    return tuple(math.log(1.0 - 2.0 ** (-5.0 - hh)) for hh in range(RET_HEADS))


def _ret_kernel(q_ref, k_ref, v_ref, g_ref, s0_ref, y_ref, st_ref, decay_ref, *, cb, n_valid):
    @pl.when((pl.program_id(0) == 0) & (pl.program_id(1) == 0))
    def _():
        row = lax.broadcasted_iota(jnp.int32, (cb, cb), 0)
        col = lax.broadcasted_iota(jnp.int32, (cb, cb), 1)
        dist = (row - col).astype(F32)
        for hh, lg in enumerate(_ret_log_gammas()):
            decay_ref[hh] = jnp.where(dist >= 0, jnp.exp(lg * jnp.maximum(dist, 0.0)), 0.0)

    @pl.when(pl.program_id(1) == 0)
    def _():
        st_ref[...] = s0_ref[...]

    idx = lax.broadcasted_iota(jnp.int32, (cb, 1), 0).astype(F32)
    for hh, lg in enumerate(_ret_log_gammas()):
        q = q_ref[0, :, hh * RET_DK:(hh + 1) * RET_DK]
        k = k_ref[0, :, hh * RET_DK:(hh + 1) * RET_DK]
        v = v_ref[0, :, hh * RET_DV:(hh + 1) * RET_DV]
        g = g_ref[0, :, hh * RET_DV:(hh + 1) * RET_DV].astype(F32)
        st = st_ref[0, hh]
        s = lax.dot_general(q, k, (((1,), (1,)), ((), ())), preferred_element_type=F32) * decay_ref[hh]
        o = jnp.dot(s.astype(BF16), v, preferred_element_type=F32)
        o = o + jnp.dot(q, st.astype(BF16), preferred_element_type=F32) * jnp.exp(lg * (idx + 1.0))
        kw = (k.astype(F32) * jnp.exp(lg * (n_valid - 1.0 - idx))).astype(BF16)
        st_ref[0, hh] = math.exp(lg * n_valid) * st + lax.dot_general(
            kw, v, (((0,), (0,)), ((), ())), preferred_element_type=F32)
        on = o * lax.rsqrt(jnp.mean(o * o, axis=-1, keepdims=True) + EPS)
        y_ref[0, :, hh * RET_DV:(hh + 1) * RET_DV] = (g * jax.nn.sigmoid(g) * on).astype(BF16)


def _retention(q, k, v, g, s0, cb, n_valid):
    b, s, _ = q.shape
    s0_map = (lambda i, c: (i, 0, 0, 0)) if s0.shape[0] == b else (lambda i, c: (0, 0, 0, 0))
    blk = lambda w: pl.BlockSpec((1, cb, w), lambda i, c: (i, c, 0))
    st_block = (1, RET_HEADS, RET_DK, RET_DV)
    return pl.pallas_call(
        functools.partial(_ret_kernel, cb=cb, n_valid=n_valid),
        grid=(b, s // cb),
        in_specs=[blk(q.shape[2]), blk(k.shape[2]), blk(v.shape[2]), blk(g.shape[2]),
                  pl.BlockSpec(st_block, s0_map)],
        out_specs=[blk(v.shape[2]), pl.BlockSpec(st_block, lambda i, c: (i, 0, 0, 0))],
        out_shape=[jax.ShapeDtypeStruct(v.shape, BF16),
                   jax.ShapeDtypeStruct((b,) + st_block[1:], F32)],
        scratch_shapes=[pltpu.VMEM((RET_HEADS, cb, cb), F32)],
        compiler_params=_cparams(("arbitrary", "arbitrary")),
        name="retention",
    )(q, k, v, g, s0)


def _scores(qq, kk):
    return lax.dot_general(qq, kk, (((1,), (1,)), ((), ())), preferred_element_type=F32)


def _softmax_seed(scores, values):
    ms = [jnp.max(s, axis=-1, keepdims=True) for s in scores]
    ps = [jnp.exp2(s - m) for s, m in zip(scores, ms)]
    out = []
    for m, p, vv in zip(ms, ps, values):
        out += [m, jnp.sum(p, axis=-1, keepdims=True),
                jnp.dot(p.astype(BF16), vv, preferred_element_type=F32)]
    return out


def _softmax_step(carry, scores, values):
    m2s = [jnp.maximum(carry[3 * c], jnp.max(s, axis=-1, keepdims=True)) for c, s in enumerate(scores)]
    ps = [jnp.exp2(s - m2) for s, m2 in zip(scores, m2s)]
    out = []
    for c, (m2, p, vv) in enumerate(zip(m2s, ps, values)):
        m, l, acc = carry[3 * c:3 * c + 3]
        a = jnp.exp2(m - m2)
        out += [m2, a * l + jnp.sum(p, axis=-1, keepdims=True),
                a * acc + jnp.dot(p.astype(BF16), vv, preferred_element_type=F32)]
    return out


def _sub_norm(o1, o2, lam, gs, coef):
    o = o1 - lam * o2
    return o * lax.rsqrt(jnp.mean(o * o, axis=-1, keepdims=True) + EPS) * gs * coef


SUM_ROWS = 16
VT_ROWS = HEAD_W + SUM_ROWS


def _softmax_seed_t(scores_t, values_t):
    ms = [jnp.max(s, axis=0, keepdims=True) for s in scores_t]
    ps = [jnp.exp2(s - m) for s, m in zip(scores_t, ms)]
    out = []
    for m, p, vt in zip(ms, ps, values_t):
        out += [m, jnp.dot(vt, p.astype(BF16), preferred_element_type=F32)]
    return out


def _softmax_step_t(carry, scores_t, values_t):
    m2s = [jnp.maximum(carry[2 * c], jnp.max(s, axis=0, keepdims=True)) for c, s in enumerate(scores_t)]
    ps = [jnp.exp2(s - m2) for s, m2 in zip(scores_t, m2s)]
    out = []
    for c, (m2, p, vt) in enumerate(zip(m2s, ps, values_t)):
        m, acc = carry[2 * c:2 * c + 2]
        out += [m2, jnp.exp2(m - m2) * acc + jnp.dot(vt, p.astype(BF16), preferred_element_type=F32)]
    return out


def _attn_prompt_kernel(lam_ref, q_ref, km_ref, vm_ref, ke_ref, ve_ref, gst_ref, o_ref, kb_ref, vt_ref,
                        *, sq, tq, hp, n_ext, coef):
    heads = [slice(hh * HEAD_W, (hh + 1) * HEAD_W) for hh in range(hp)]
    vrows = [slice(hh * VT_ROWS, (hh + 1) * VT_ROWS) for hh in range(hp)]
    kb_ref[...] = km_ref[0].astype(BF16)
    ones = jnp.ones((SUM_ROWS, tq), BF16)
    for j in range(sq // tq):
        for hh, sl in enumerate(heads):
            vt_ref[j, hh * VT_ROWS:hh * VT_ROWS + HEAD_W, :] = vm_ref[0, j * tq:(j + 1) * tq, sl].T.astype(BF16)
            vt_ref[j, hh * VT_ROWS + HEAD_W:(hh + 1) * VT_ROWS, :] = ones
    ke = [ke_ref[0, :, sl].astype(BF16) for sl in heads]
    vet = [jnp.concatenate([ve_ref[0, :, sl].T.astype(BF16), jnp.ones((SUM_ROWS, ve_ref.shape[1]), BF16)], axis=0)
           for sl in heads]
    lam = lam_ref[0]
    lo = lax.broadcasted_iota(jnp.int32, (1, HEAD_W), 1) < DIFF_DH
    ext_ok = lax.broadcasted_iota(jnp.int32, (ke_ref.shape[1], 1), 0) < n_ext
    diag = (lax.broadcasted_iota(jnp.int32, (tq, tq), 0) // CHUNK
            <= lax.broadcasted_iota(jnp.int32, (tq, tq), 1) // CHUNK)

    def q_tile(qi):
        q0 = qi * tq
        qs = []
        for sl in heads:
            q = q_ref[0, pl.ds(q0, tq), sl]
            zero = jnp.zeros_like(q)
            qs += [jnp.where(lo, q, zero), jnp.where(lo, zero, q)]

        ext_s = [jnp.where(ext_ok, _scores(ke[c // 2], qq), NEG) for c, qq in enumerate(qs)]
        carry = _softmax_seed_t(ext_s, [vet[c // 2] for c in range(len(qs))])

        def step(j, carry, mask=None):
            ss = [_scores(kb_ref[j * tq:(j + 1) * tq, heads[c // 2]], qq) for c, qq in enumerate(qs)]
            if mask is not None:
                ss = [jnp.where(mask, s, NEG) for s in ss]
            return tuple(_softmax_step_t(carry, ss, [vt_ref[j, vrows[c // 2], :] for c in range(len(qs))]))

        carry = tuple(carry)
        for j in range(qi):
            carry = step(j, carry)
        carry = step(qi, carry, diag)
        for hh, sl in enumerate(heads):
            _, a1, _, a2 = carry[4 * hh:4 * hh + 4]
            o = a1[:HEAD_W] / a1[HEAD_W:HEAD_W + 1] - lam * (a2[:HEAD_W] / a2[HEAD_W:HEAD_W + 1])
            on = o * lax.rsqrt(jnp.mean(o * o, axis=0, keepdims=True) + EPS) * gst_ref[...] * coef
            o_ref[0, q0:q0 + tq, sl] = on.T.astype(BF16)

    for qi in range(sq // tq):
        pl.when(pl.program_id(0) >= 0)(functools.partial(q_tile, qi))


def _attention_prompt(q, k_main, v_main, k_ext, v_ext, n_ext, lam, g_subln, coef, tq, hp):
    b, sq, hw = q.shape
    w = hp * HEAD_W
    bh = pl.BlockSpec((1, sq, w), lambda i, h: (i, 0, h))
    ext = pl.BlockSpec((1, k_ext.shape[1], w), lambda i, h: (0, 0, h))
    gst = jnp.broadcast_to(g_subln.reshape(HEAD_W, 1), (HEAD_W, tq))
    return pl.pallas_call(
        functools.partial(_attn_prompt_kernel, sq=sq, tq=tq, hp=hp, n_ext=n_ext, coef=coef),
        grid=(b, hw // w),
        in_specs=[pl.BlockSpec(memory_space=pltpu.SMEM), bh, bh, bh, ext, ext,
                  pl.BlockSpec((HEAD_W, tq), lambda i, h: (0, 0))],
        out_specs=bh,
        out_shape=jax.ShapeDtypeStruct(q.shape, BF16),
        scratch_shapes=[pltpu.VMEM((sq, w), BF16), pltpu.VMEM((sq // tq, hp * VT_ROWS, tq), BF16)],
        compiler_params=_cparams(("arbitrary", "arbitrary")),
        name="attention_prompt",
    )(lam, q, k_main, v_main, k_ext, v_ext, gst)


def _attn_sample_kernel(lam_ref, q_ref, kc_ref, vc_ref, ke_ref, ve_ref, gs_ref, o_ref, m_ref, l_ref, acc_ref,
                        *, ds, nh, n_ext, coef):
    t = pl.program_id(1)
    tkv = kc_ref.shape[1] // nh
    lo = lax.broadcasted_iota(jnp.int32, (1, HEAD_W), 1) < DIFF_DH
    ext_ok = lax.broadcasted_iota(jnp.int32, (1, ke_ref.shape[1]), 1) < n_ext
    heads = [slice(hh * HEAD_W, (hh + 1) * HEAD_W) for hh in range(nh)]

    def stacked_q(sl):
        q = q_ref[0, :, sl]
        zero = jnp.zeros_like(q)
        return jnp.concatenate([jnp.where(lo, q, zero), jnp.where(lo, zero, q)], axis=0)

    qs = [stacked_q(sl) for sl in heads]

    def write_stats(stats):
        for hh in range(nh):
            m_ref[hh], l_ref[hh], acc_ref[hh] = stats[3 * hh:3 * hh + 3]

    @pl.when(t == 0)
    def _():
        ext_s = [jnp.where(ext_ok, _scores(qq, ke_ref[0, :, sl].astype(BF16)), NEG) for qq, sl in zip(qs, heads)]
        write_stats(_softmax_seed(ext_s, [ve_ref[0, :, sl].astype(BF16) for sl in heads]))

    carry = []
    for hh in range(nh):
        carry += [m_ref[hh], l_ref[hh], acc_ref[hh]]
    ss = [_scores(qq, kc_ref[0, pl.ds(hh, tkv, stride=nh), :].astype(BF16)) for hh, qq in enumerate(qs)]
    write_stats(_softmax_step(carry, ss, [vc_ref[0, pl.ds(hh, tkv, stride=nh), :].astype(BF16) for hh in range(nh)]))

    @pl.when(t == pl.num_programs(1) - 1)
    def _():
        lam = lam_ref[0]
        for hh, sl in enumerate(heads):
            o = acc_ref[hh] / l_ref[hh]
            o_ref[0, :, sl] = _sub_norm(o[:ds], o[ds:], lam, gs_ref[...], coef).astype(BF16)


def _attention_sample(q, k_cache, v_cache, k_ext, v_ext, n_ext, lam, g_subln, coef, tkv):
    b, ds, hw = q.shape
    _, pl_len, nh, _ = k_cache.shape
    per_b = lambda rows: pl.BlockSpec((1, rows, hw), lambda i, t: (i, 0, 0))
    cache = pl.BlockSpec((1, tkv * nh, HEAD_W), lambda i, t: (i, t, 0))
    return pl.pallas_call(
        functools.partial(_attn_sample_kernel, ds=ds, nh=nh, n_ext=n_ext, coef=coef),
        grid=(b, pl_len // tkv),
        in_specs=[pl.BlockSpec(memory_space=pltpu.SMEM), per_b(ds), cache, cache,
                  per_b(k_ext.shape[1]), per_b(k_ext.shape[1]),
                  pl.BlockSpec((1, HEAD_W), lambda i, t: (0, 0))],
        out_specs=per_b(ds),
        out_shape=jax.ShapeDtypeStruct(q.shape, BF16),
        scratch_shapes=[pltpu.VMEM((nh, 2 * ds, 1), F32), pltpu.VMEM((nh, 2 * ds, 1), F32),
                        pltpu.VMEM((nh, 2 * ds, HEAD_W), F32)],
        compiler_params=_cparams(("arbitrary", "arbitrary")),
        name="attention_sample",
    )(lam, q, k_cache.reshape(b, pl_len * nh, HEAD_W), v_cache.reshape(b, pl_len * nh, HEAD_W),
      k_ext, v_ext, g_subln)


def _mix_kernel(x_ref, yr_ref, od_ref, ga_ref, gb_ref, wr_ref, wd_ref, wo_ref, gf_ref, wrt_ref, br_ref,
                x1_ref, h_ref, lgt_ref):
    ya = jnp.dot(yr_ref[...], wr_ref[...], preferred_element_type=F32)
    yb = jnp.dot(od_ref[...], wd_ref[...], preferred_element_type=F32)
    z = jax.nn.sigmoid(ga_ref[...].astype(F32)) * ya + jax.nn.sigmoid(gb_ref[...].astype(F32)) * yb
    x1 = x_ref[...] + jnp.dot(z.astype(BF16), wo_ref[...], preferred_element_type=F32)
    x1_ref[...] = x1
    h = x1 * lax.rsqrt(jnp.mean(x1 * x1, axis=-1, keepdims=True) + EPS) * gf_ref[...]
    _tile_rows_store(h_ref, h)
    lgt_ref[...] = _scores(wrt_ref[...], h.astype(BF16)) + br_ref[...]


def _mix_tail_kernel(*refs):
    th_ref, tlgt_ref = refs[11:13]
    h_ref, lgt_ref = refs[14:16]
    last = pl.num_programs(0) - 1

    @pl.when(pl.program_id(0) < last)
    def _():
        _mix_kernel(*refs[:11], *refs[13:])

    @pl.when(pl.program_id(0) == last)
    def _():
        h_ref[...] = th_ref[...]
        lgt_ref[...] = tlgt_ref[...]


def _mix(x, yr, od, ga, gb, wr, wd, wo, gf, wrt, br, tm, tail=None):
    t, d = x.shape
    ne = wrt.shape[0]
    nt = t // tm
    extra = 0 if tail is None else 1
    row = lambda w: pl.BlockSpec((tm, w), lambda i: (jnp.minimum(i, nt - 1), 0))
    full = lambda a: pl.BlockSpec(a.shape, lambda i: (0, 0))
    ops = (x, yr, od, ga, gb, wr, wd, wo, gf, wrt, br) + (() if tail is None else tuple(tail))
    return pl.pallas_call(
        _mix_kernel if tail is None else _mix_tail_kernel,
        grid=(nt + extra,),
        in_specs=[row(d), row(yr.shape[1]), row(od.shape[1]), row(d), row(d)] + [full(a) for a in ops[5:]],
        out_specs=[row(d), pl.BlockSpec((tm * SUBLANES, LANES), lambda i: (i, 0)),
                   pl.BlockSpec((ne, tm), lambda i: (0, i))],
        out_shape=[jax.ShapeDtypeStruct((t, d), F32),
                   jax.ShapeDtypeStruct(((t + extra * tm) * SUBLANES, LANES), F32),
                   jax.ShapeDtypeStruct((ne, t + extra * tm), F32)],
        compiler_params=_cparams(("arbitrary",)),
        name="mix",
    )(*ops)


def _route_kernel(lgt_ref, e_ref, gate_ref, rank_ref, cnt_ref):
    @pl.when(pl.program_id(0) == 0)
    def _():
        cnt_ref[...] = jnp.zeros_like(cnt_ref)

    v = lgt_ref[...]
    ne, tn = v.shape
    eid = lax.broadcasted_iota(jnp.int32, (ne, tn), 0)
    tops, sels = [], []
    for k in range(TOP_K):
        m = jnp.max(v, axis=0, keepdims=True)
        idx = jnp.min(jnp.where(v == m, eid, ne), axis=0, keepdims=True)
        sel = eid == idx
        e_ref[k:k + 1, :] = idx
        tops.append(m)
        sels.append(sel)
        v = jnp.where(sel, -jnp.inf, v)
    ex = [jnp.exp(m - tops[0]) for m in tops]
    den = ex[0] + ex[1] + ex[2] + ex[3]
    for k in range(TOP_K):
        gate_ref[k:k + 1, :] = ex[k] / den

    chosen = sels[0] | sels[1] | sels[2] | sels[3]
    before = (lax.broadcasted_iota(jnp.int32, (tn, tn), 0) < lax.broadcasted_iota(jnp.int32, (tn, tn), 1))
    prior = jnp.dot(chosen.astype(BF16), before.astype(BF16), preferred_element_type=F32) + cnt_ref[:, 0:1]
    for k in range(TOP_K):
        rank_ref[k:k + 1, :] = jnp.sum(jnp.where(sels[k], prior, 0.0), axis=0, keepdims=True).astype(jnp.int32)
    cnt_ref[...] = cnt_ref[...] + jnp.sum(chosen.astype(F32), axis=1, keepdims=True)


def _route_topk(lgt, tn):
    ne, t = lgt.shape
    kt = pl.BlockSpec((TOP_K, tn), lambda i: (0, i))
    return pl.pallas_call(
        _route_kernel,
        grid=(t // tn,),
        in_specs=[pl.BlockSpec((ne, tn), lambda i: (0, i))],
        out_specs=[kt, kt, kt, pl.BlockSpec((ne, LANES), lambda i: (0, 0))],
        out_shape=[jax.ShapeDtypeStruct((TOP_K, t), jnp.int32), jax.ShapeDtypeStruct((TOP_K, t), F32),
                   jax.ShapeDtypeStruct((TOP_K, t), jnp.int32), jax.ShapeDtypeStruct((ne, LANES), F32)],
        compiler_params=_cparams(("arbitrary",)),
        name="route",
    )(lgt)


GATHER_UNROLL = 8
MOE_GATHER_PRIORITIES = (1,)


def _gather_rows(src_hbm, idx_ref, n, buf_ref, slot, sem, priorities=(0, 1)):
    def body(g, _):
        for u in range(GATHER_UNROLL):
            r = g * GATHER_UNROLL + u
            src = pl.multiple_of(idx_ref[0, 0, r], SUBLANES)
            dst = pl.multiple_of(r * SUBLANES, SUBLANES)
            pltpu.make_async_copy(src_hbm.at[pl.ds(src, SUBLANES)], buf_ref.at[slot, pl.ds(dst, SUBLANES)],
                                  sem.at[slot]).start(priority=priorities[u % len(priorities)])
        return 0
    lax.fori_loop(0, n // GATHER_UNROLL, body, 0)


def _wait_rows(src_hbm, n, buf_ref, slot, sem):
    pltpu.make_async_copy(src_hbm.at[pl.ds(0, n * SUBLANES)], buf_ref.at[slot], sem.at[slot]).wait()


MOE_SLOTS = 3


def _moe_kernel(be_ref, nu_ref, tok_ref, tok1_ref, tok2_ref, h_hbm, wgu_ref, bgu_ref, wdn_ref, bdn_ref, out_ref,
                buf_ref, wgu_bf, wdn_bf, sem):
    i = pl.program_id(0)
    n_used = nu_ref[0]
    slot = i % MOE_SLOTS
    d_ff = wdn_bf.shape[0]

    @pl.when(i == 0)
    def _():
        _gather_rows(h_hbm, tok_ref, MOE_ROWS, buf_ref, 0, sem, MOE_GATHER_PRIORITIES)

        @pl.when(1 < n_used)
        def _():
            _gather_rows(h_hbm, tok1_ref, MOE_ROWS, buf_ref, 1, sem, MOE_GATHER_PRIORITIES)

    @pl.when(i + 2 < n_used)
    def _():
        _gather_rows(h_hbm, tok2_ref, MOE_ROWS, buf_ref, (i + 2) % MOE_SLOTS, sem, MOE_GATHER_PRIORITIES)

    @pl.when(i < n_used)
    def _():
        @pl.when((i == 0) | (be_ref[i] != be_ref[jnp.maximum(i - 1, 0)]))
        def _():
            wgu_bf[...] = wgu_ref[0].astype(BF16)
            wdn_bf[...] = wdn_ref[0].astype(BF16)

        _wait_rows(h_hbm, MOE_ROWS, buf_ref, slot, sem)

        xb = _tile_rows_load(buf_ref, slot, 0, MOE_ROWS).astype(BF16)
        gu = jnp.dot(xb, wgu_bf[...], preferred_element_type=F32) + bgu_ref[0]
        gate = jnp.minimum(gu[:, :d_ff], SWIGLU_LIMIT)
        up = jnp.clip(gu[:, d_ff:], -SWIGLU_LIMIT, SWIGLU_LIMIT)
        act = (up + 1.0) * gate * jax.nn.sigmoid(SWIGLU_ALPHA * gate)
        _tile_rows_store(out_ref, jnp.dot(act.astype(BF16), wdn_bf[...], preferred_element_type=F32) + bdn_ref[0])

    @pl.when(i >= n_used)
    def _():
        out_ref[...] = jnp.zeros_like(out_ref)


def _moe(h, tok, block_e, n_used, w_gu, b_gu, w_down, b_down):
    d = h.shape[1] * SUBLANES
    nb = tok.shape[0]
    ne, _, two_ff = w_gu.shape
    d_ff = two_ff // 2
    e_map = lambda i, be, nu: (be[i], 0, 0)
    grid_spec = pltpu.PrefetchScalarGridSpec(
        num_scalar_prefetch=2,
        grid=(nb,),
        in_specs=[
            pl.BlockSpec((1, 1, MOE_ROWS), lambda i, be, nu: (i, 0, 0), memory_space=pltpu.SMEM),
            pl.BlockSpec((1, 1, MOE_ROWS), lambda i, be, nu: (jnp.minimum(i + 1, nb - 1), 0, 0),
                         memory_space=pltpu.SMEM),
            pl.BlockSpec((1, 1, MOE_ROWS), lambda i, be, nu: (jnp.minimum(i + 2, nb - 1), 0, 0),
                         memory_space=pltpu.SMEM),
            pl.BlockSpec(memory_space=pl.ANY),
            pl.BlockSpec((1, d, two_ff), e_map),
            pl.BlockSpec((1, 1, two_ff), e_map),
            pl.BlockSpec((1, d_ff, d), e_map),
            pl.BlockSpec((1, 1, d), e_map),
        ],
        out_specs=pl.BlockSpec((MOE_ROWS * SUBLANES, LANES), lambda i, be, nu: (i, 0)),
        scratch_shapes=[pltpu.VMEM((MOE_SLOTS, MOE_ROWS * SUBLANES, LANES), F32),
                        pltpu.VMEM((d, two_ff), BF16),
                        pltpu.VMEM((d_ff, d), BF16),
                        pltpu.SemaphoreType.DMA((MOE_SLOTS,))],
    )
    return pl.pallas_call(
        _moe_kernel,
        grid_spec=grid_spec,
        out_shape=jax.ShapeDtypeStruct((nb * MOE_ROWS * SUBLANES, LANES), F32),
        compiler_params=_cparams(("arbitrary",)),
        name="moe",
    )(block_e, n_used, tok, tok, tok, h, w_gu, b_gu.reshape(ne, 1, two_ff), w_down, b_down.reshape(ne, 1, d))


def _combine_kernel(dst_ref, dstn_ref, x_ref, gate_ref, gf_ref, rows_hbm, y_ref, buf_ref, sem, *, tm, nsteps):
    i = pl.program_id(0)
    slot = i % 2
    n = tm * TOP_K

    @pl.when(i == 0)
    def _():
        _gather_rows(rows_hbm, dst_ref, n, buf_ref, 0, sem)

    @pl.when(i + 1 < nsteps)
    def _():
        _gather_rows(rows_hbm, dstn_ref, n, buf_ref, 1 - slot, sem)

    _wait_rows(rows_hbm, n, buf_ref, slot, sem)

    y = x_ref[...]
    gate = gate_ref[...]
    for k in range(TOP_K):
        y = y + gate[:, k:k + 1] * _tile_rows_load(buf_ref, slot, k * tm, tm)
    y_ref[...] = y * lax.rsqrt(jnp.mean(y * y, axis=-1, keepdims=True) + EPS) * gf_ref[...]


def _combine(x1, gates, dest, rows, g_final, tm):
    t, d = x1.shape
    nsteps = t // tm
    n = tm * TOP_K
    return pl.pallas_call(
        functools.partial(_combine_kernel, tm=tm, nsteps=nsteps),
        grid=(nsteps,),
        in_specs=[pl.BlockSpec((1, 1, n), lambda i: (i, 0, 0), memory_space=pltpu.SMEM),
                  pl.BlockSpec((1, 1, n), lambda i: (jnp.minimum(i + 1, nsteps - 1), 0, 0),
                               memory_space=pltpu.SMEM),
                  pl.BlockSpec((tm, d), lambda i: (i, 0)),
                  pl.BlockSpec((tm, TOP_K), lambda i: (i, 0)),
                  pl.BlockSpec((1, d), lambda i: (0, 0)),
                  pl.BlockSpec(memory_space=pl.ANY)],
        out_specs=pl.BlockSpec((tm, d), lambda i: (i, 0)),
        out_shape=jax.ShapeDtypeStruct((t, d), F32),
        scratch_shapes=[pltpu.VMEM((2, n * SUBLANES, LANES), F32), pltpu.SemaphoreType.DMA((2,))],
        compiler_params=_cparams(("arbitrary",)),
        name="combine",
    )(dest, dest, x1, gates, g_final, rows)


def _route(lgt, tn):
    t = lgt.shape[1]
    n_assign = t * TOP_K
    e4, gates, rank, cnt = _route_topk(lgt, tn)
    counts = cnt[:, 0].astype(jnp.int32)
    padded = (counts + MOE_ROWS - 1) // MOE_ROWS * MOE_ROWS
    pend = jnp.cumsum(padded)
    pstart = pend - padded
    start = jnp.cumsum(counts) - counts
    e_ids = jnp.arange(N_EXPERTS, dtype=jnp.int32)
    dest = rank + jnp.sum(jnp.where(e4[None] == e_ids[:, None, None], pstart[:, None, None], 0), axis=0)
    nb = n_assign // MOE_ROWS + N_EXPERTS
    n_used = (pend[-1] // MOE_ROWS).astype(jnp.int32).reshape(1)
    blk0 = jnp.arange(nb, dtype=jnp.int32) * MOE_ROWS
    block_e = jnp.minimum(jnp.sum((pend[None, :] <= blk0[:, None]).astype(jnp.int32), axis=1), N_EXPERTS - 1)
    of_block = lambda table: jnp.sum(jnp.where(block_e[:, None] == e_ids[None, :], table[None, :], 0), axis=1)
    order = jnp.argsort(e4.T.reshape(-1))
    within = (blk0 - of_block(pstart))[:, None] + jnp.arange(MOE_ROWS, dtype=jnp.int32)[None, :]
    src = jnp.clip(of_block(start)[:, None] + jnp.minimum(within, of_block(counts)[:, None] - 1), 0, n_assign - 1)
    tok = (order[src] // TOP_K).astype(jnp.int32)
    return gates.T, dest * SUBLANES, (tok * SUBLANES).reshape(nb, 1, MOE_ROWS), block_e, n_used


def kernel(x_prompt, x_sample, cache_k, cache_v, state_ret, meta_tokens, g_mix, w_in,
           lam_q1, lam_k1, lam_q2, lam_k2, g_subln, w_ret_out, w_diff_out, w_out,
           g_ffn, w_router, b_router, w_gu, b_gu, w_down, b_down, g_final):
    b, s, d = x_prompt.shape
    db, ds, _ = x_sample.shape
    depth, _, pl_len = cache_k.shape[:3]
    assert depth == 1, "single-layer step only"
    nm = meta_tokens.shape[0]
    hw = DIFF_HEADS * HEAD_W

    lam_init = 0.8 - 0.6 * math.exp(-0.3 * 0)
    lam = (jnp.exp(jnp.sum(lam_q1[0] * lam_k1[0])) - jnp.exp(jnp.sum(lam_q2[0] * lam_k2[0]))
           + lam_init).reshape(1).astype(F32)
    coef = 1.0 - lam_init

    w_in_bf = w_in[0].astype(BF16)
    g_mix2 = g_mix[0].reshape(1, d)

    tm = 512 if (b * s) % 512 == 0 else s
    tab_p = _tables(N_META + jnp.arange(s, dtype=jnp.int32))
    rq, rk, rv, rg, dq, dk, dv, ga, gb = _proj(x_prompt.reshape(b * s, d), g_mix2, w_in_bf, tab_p, tm, s // tm)

    pos_small = jnp.concatenate([jnp.tile(N_META + pl_len + jnp.arange(ds, dtype=jnp.int32), db),
                                 jnp.arange(nm, dtype=jnp.int32)])
    x_small = jnp.concatenate([x_sample.reshape(db * ds, d), meta_tokens.astype(F32)], axis=0)
    small = _proj(x_small, g_mix2, w_in_bf, _tables(pos_small), x_small.shape[0], 1)
    ns = db * ds
    srq, srk, srv, srg, sdq, sdk, sdv, sga, sgb = [a[:ns] for a in small]
    mrq, mrk, mrv, mrg, _, mdk, mdv, _, _ = [a[ns:] for a in small]

    cb_small = 128

    def pad_rows(a, nb_, n_):
        a = a.reshape(nb_, n_, a.shape[-1])
        return jnp.pad(a, ((0, 0), (0, cb_small - n_), (0, 0)))

    zero_state = jnp.zeros((1, RET_HEADS, RET_DK, RET_DV), F32)
    _, s_meta = _retention(pad_rows(mrq, 1, nm), pad_rows(mrk, 1, nm), pad_rows(mrv, 1, nm),
                           pad_rows(mrg, 1, nm), zero_state, cb_small, nm)
    cb = 256
    r3 = lambda a: a.reshape(b, s, a.shape[-1])
    yr, ret_p = _retention(r3(rq), r3(rk), r3(rv), r3(rg), s_meta, cb, cb)
    syr, ret_s = _retention(pad_rows(srq, db, ds), pad_rows(srk, db, ds), pad_rows(srv, db, ds),
                            pad_rows(srg, db, ds), state_ret[0], cb_small, ds)
    syr = syr[:, :ds].reshape(ns, -1)

    g_sub = g_subln[0].reshape(1, HEAD_W)
    ext_rows = 128
    pad_ext = lambda a: jnp.pad(a, ((0, 0), (0, ext_rows - a.shape[1]), (0, 0)))
    od = _attention_prompt(r3(dq), r3(dk), r3(dv), pad_ext(mdk[None]), pad_ext(mdv[None]), nm,
                           lam, g_sub, coef, 256, 4)
    sdk3, sdv3 = sdk.reshape(db, ds, hw), sdv.reshape(db, ds, hw)
    ke_s = jnp.concatenate([jnp.broadcast_to(mdk[None], (db, nm, hw)), sdk3], axis=1)
    ve_s = jnp.concatenate([jnp.broadcast_to(mdv[None], (db, nm, hw)), sdv3], axis=1)
    sod = _attention_sample(sdq.reshape(db, ds, hw), cache_k[0], cache_v[0], pad_ext(ke_s), pad_ext(ve_s),
                            nm + ds, lam, g_sub, coef, min(2048, pl_len))

    wr, wd, wo = w_ret_out[0].astype(BF16), w_diff_out[0].astype(BF16), w_out[0].astype(BF16)
    gf = g_ffn[0].reshape(1, d)
    wrt = w_router[0].T.astype(BF16)
    br = b_router[0].astype(F32).reshape(N_EXPERTS, 1)
    sx1, sh, slgt = _mix(x_sample.reshape(ns, d), syr, sod.reshape(ns, hw), sga, sgb, wr, wd, wo, gf, wrt, br, ns)
    x1, h, lgt = _mix(x_prompt.reshape(b * s, d), yr.reshape(b * s, -1), od.reshape(b * s, hw), ga, gb,
                      wr, wd, wo, gf, wrt, br, ns, (sh, slgt))

    t = b * s
    gates, dest, tok, block_e, n_used = _route(lgt, ns)
    rows = _moe(h, tok, block_e, n_used, w_gu[0], b_gu[0], w_down[0], b_down[0])
    gfin = g_final.reshape(1, d)

    def combine(x1g, lo, n):
        tc = min(512, n)
        dg = dest[:, lo:lo + n].reshape(TOP_K, n // tc, tc).swapaxes(0, 1).reshape(n // tc, 1, TOP_K * tc)
        return _combine(x1g, gates[lo:lo + n], dg, rows, gfin, tc)

    y_p = combine(x1, 0, t)
    y_s = combine(sx1, t, ns)

    k_p = jnp.concatenate([jnp.broadcast_to(mdk[None], (b, nm, hw)), r3(dk)], axis=1)
    v_p = jnp.concatenate([jnp.broadcast_to(mdv[None], (b, nm, hw)), r3(dv)], axis=1)
    shp = lambda a: a.reshape(1, a.shape[0], a.shape[1], DIFF_HEADS, HEAD_W)
    return (y_p.reshape(b, s, d), y_s.reshape(db, ds, d), ret_p[None], shp(k_p), shp(v_p),
            ret_s[None], shp(sdk3), shp(sdv3))
```

```python
import functools
import math

import jax
import jax.numpy as jnp
from jax import lax
from jax.experimental import pallas as pl
from jax.experimental.pallas import tpu as pltpu

F32 = jnp.float32
BF16 = jnp.bfloat16

EPS = 1e-6
N_META = 16
CHUNK = 64
RET_HEADS = 4
RET_DK = 128
RET_DV = 256
RET_THETA = 10000.0
DIFF_HEADS = 8
DIFF_DH = 64
ROT_DIM = DIFF_DH // 4
ROPE_THETA = 500000.0
N_EXPERTS = 32
TOP_K = 4
SWIGLU_LIMIT = 7.0
SWIGLU_ALPHA = 1.702

LANES = 128
SUBLANES = 8
HEAD_W = 2 * DIFF_DH
MOE_ROWS = 512
VMEM_LIMIT = 56 * 1024 * 1024
NEG = -1e30
Q_SCALE = DIFF_DH ** -0.5 * math.log2(math.e)


def _cparams(sem):
    return pltpu.CompilerParams(dimension_semantics=sem, vmem_limit_bytes=VMEM_LIMIT)


def _tile_rows_store(ref, val):
    n = val.shape[0]
    for c in range(SUBLANES):
        ref[pl.ds(c, n, stride=SUBLANES), :] = val[:, c * LANES:(c + 1) * LANES]


def _tile_rows_load(ref, slot, first, n):
    return jnp.concatenate([ref[slot, pl.ds(first * SUBLANES + c, n, stride=SUBLANES), :]
                            for c in range(SUBLANES)], axis=1)


def _ret_tables(pos):
    angle = RET_THETA ** (-jnp.linspace(0.0, 1.0, RET_DK // 2, dtype=F32))
    angle = jnp.repeat(angle, 2)
    ang = pos.astype(F32)[:, None] * angle[None, :]
    cos, sin = jnp.cos(ang), jnp.sin(ang)
    even = (jnp.arange(RET_DK) % 2 == 0)[None, :]
    return cos, jnp.where(even, -sin, 0.0), jnp.where(even, 0.0, sin)


def _rope_tables(pos):
    half = ROT_DIM // 2
    inv = ROPE_THETA ** (-jnp.arange(half, dtype=F32) * (2.0 / ROT_DIM))
    ang = pos.astype(F32)[:, None] * inv[None, :]
    cos, sin = jnp.cos(ang), jnp.sin(ang)
    jj = jnp.arange(HEAD_W) % DIFF_DH
    first = (jj < half)[None, :]
    second = ((jj >= half) & (jj < ROT_DIM))[None, :]
    cos_l = jnp.take(cos, jj % half, axis=1)
    sin_l = jnp.take(sin, jj % half, axis=1)
    c = jnp.where(first | second, cos_l, 1.0)
    sa = jnp.where(first, -sin_l, 0.0)
    sb = jnp.where(second, sin_l, 0.0)
    return c, sa, sb


def _tables(pos):
    return jnp.concatenate(_ret_tables(pos) + _rope_tables(pos), axis=1)


def _proj_kernel(x_ref, g_ref, w_ref, tab_ref, rq_ref, rk_ref, rv_ref, rg_ref,
                 dq_ref, dk_ref, dv_ref, ga_ref, gb_ref):
    x = x_ref[...]
    ms = jnp.mean(x * x, axis=-1, keepdims=True)
    h = (x * lax.rsqrt(ms + EPS) * g_ref[...]).astype(BF16)

    def mm(c0, width):
        return jnp.dot(h, w_ref[:, c0:c0 + width], preferred_element_type=F32)

    def rot(p, t0, near, far):
        c = tab_ref[:, t0:t0 + LANES]
        sa = tab_ref[:, t0 + LANES:t0 + 2 * LANES]
        sb = tab_ref[:, t0 + 2 * LANES:t0 + 3 * LANES]
        return p * c + pltpu.roll(p, LANES - near, 1) * sa + pltpu.roll(p, far, 1) * sb

    kw = RET_HEADS * RET_DK
    vw = RET_HEADS * RET_DV
    dw = DIFF_HEADS * HEAD_W
    c0 = 0
    p = mm(c0, kw)
    for hh in range(RET_HEADS):
        sl = slice(hh * LANES, (hh + 1) * LANES)
        rq_ref[:, sl] = rot(p[:, sl], 0, 1, 1).astype(BF16)
    c0 += kw
    p = mm(c0, kw)
    for hh in range(RET_HEADS):
        sl = slice(hh * LANES, (hh + 1) * LANES)
        rk_ref[:, sl] = (rot(p[:, sl], 0, 1, 1) * (RET_DK ** -0.5)).astype(BF16)
    c0 += kw
    rv_ref[...] = mm(c0, vw).astype(BF16)
    c0 += vw
    rg_ref[...] = mm(c0, vw).astype(BF16)
    c0 += vw
    half = ROT_DIM // 2
    p = mm(c0, dw)
    for hh in range(DIFF_HEADS):
        sl = slice(hh * LANES, (hh + 1) * LANES)
        dq_ref[:, sl] = (rot(p[:, sl], 3 * LANES, half, half) * Q_SCALE).astype(BF16)
    c0 += dw
    p = mm(c0, dw)
    for hh in range(DIFF_HEADS):
        sl = slice(hh * LANES, (hh + 1) * LANES)
        dk_ref[:, sl] = rot(p[:, sl], 3 * LANES, half, half)
    c0 += dw
    dv_ref[...] = mm(c0, dw)
    c0 += dw
    d = x.shape[1]
    ga_ref[...] = mm(c0, d).astype(BF16)
    c0 += d
    gb_ref[...] = mm(c0, d).astype(BF16)


def _proj(x, g, w_bf, tab, tm, tab_blocks):
    t, d = x.shape
    kw, vw, dw = RET_HEADS * RET_DK, RET_HEADS * RET_DV, DIFF_HEADS * HEAD_W
    widths = (kw, kw, vw, vw, dw, dw, dw, d, d)
    dtypes = (BF16, BF16, BF16, BF16, BF16, F32, F32, BF16, BF16)
    row = lambda i: (i, 0)
    const = lambda i: (0, 0)
    return pl.pallas_call(
        _proj_kernel,
        grid=(t // tm,),
        in_specs=[pl.BlockSpec((tm, d), row),
                  pl.BlockSpec((1, d), const),
                  pl.BlockSpec(w_bf.shape, const),
                  pl.BlockSpec((tm, tab.shape[1]), lambda i: (i % tab_blocks, 0))],
        out_specs=[pl.BlockSpec((tm, w), row) for w in widths],
        out_shape=[jax.ShapeDtypeStruct((t, w), dt) for w, dt in zip(widths, dtypes)],
        compiler_params=_cparams(("arbitrary",)),
        name="proj",
    )(x, g, w_bf, tab)


def _ret_log_gammas():
    return tuple(math.log(1.0 - 2.0 ** (-5.0 - hh)) for hh in range(RET_HEADS))


def _fill_decay(decay_ref, cb):
    row = lax.broadcasted_iota(jnp.int32, (cb, cb), 0)
    col = lax.broadcasted_iota(jnp.int32, (cb, cb), 1)
    dist = (row - col).astype(F32)
    for hh, lg in enumerate(_ret_log_gammas()):
        decay_ref[hh] = jnp.where(dist >= 0, jnp.exp(lg * jnp.maximum(dist, 0.0)), 0.0)


def _ret_block(load, store, st_ref, decay_ref, cb, n_valid):
    idx = lax.broadcasted_iota(jnp.int32, (cb, 1), 0).astype(F32)
    for hh, lg in enumerate(_ret_log_gammas()):
        kcols = slice(hh * RET_DK, (hh + 1) * RET_DK)
        vcols = slice(hh * RET_DV, (hh + 1) * RET_DV)
        q, k, v = load("q", kcols), load("k", kcols), load("v", vcols)
        g = load("g", vcols).astype(F32)
        st = st_ref[0, hh]
        s = lax.dot_general(q, k, (((1,), (1,)), ((), ())), preferred_element_type=F32) * decay_ref[hh]
        o = jnp.dot(s.astype(BF16), v, preferred_element_type=F32)
        o = o + jnp.dot(q, st.astype(BF16), preferred_element_type=F32) * jnp.exp(lg * (idx + 1.0))
        kw = (k.astype(F32) * jnp.exp(lg * (n_valid - 1.0 - idx))).astype(BF16)
        st_ref[0, hh] = math.exp(lg * n_valid) * st + lax.dot_general(
            kw, v, (((0,), (0,)), ((), ())), preferred_element_type=F32)
        on = o * lax.rsqrt(jnp.mean(o * o, axis=-1, keepdims=True) + EPS)
        store(vcols, (g * jax.nn.sigmoid(g) * on).astype(BF16))


def _ret_kernel(q_ref, k_ref, v_ref, g_ref, s0_ref, y_ref, st_ref, decay_ref, *, cb, n_valid):
    @pl.when((pl.program_id(0) == 0) & (pl.program_id(1) == 0))
    def _():
        _fill_decay(decay_ref, cb)

    @pl.when(pl.program_id(1) == 0)
    def _():
        st_ref[...] = s0_ref[...]

    refs = {"q": q_ref, "k": k_ref, "v": v_ref, "g": g_ref}

    def store(cols, y):
        y_ref[0, :, cols] = y

    _ret_block(lambda name, cols: refs[name][0, :, cols], store, st_ref, decay_ref, cb, n_valid)


def _retention(q, k, v, g, s0, cb, n_valid):
    b, s, _ = q.shape
    s0_map = (lambda i, c: (i, 0, 0, 0)) if s0.shape[0] == b else (lambda i, c: (0, 0, 0, 0))
    blk = lambda w: pl.BlockSpec((1, cb, w), lambda i, c: (i, c, 0))
    st_block = (1, RET_HEADS, RET_DK, RET_DV)
    return pl.pallas_call(
        functools.partial(_ret_kernel, cb=cb, n_valid=n_valid),
        grid=(b, s // cb),
        in_specs=[blk(q.shape[2]), blk(k.shape[2]), blk(v.shape[2]), blk(g.shape[2]),
                  pl.BlockSpec(st_block, s0_map)],
        out_specs=[blk(v.shape[2]), pl.BlockSpec(st_block, lambda i, c: (i, 0, 0, 0))],
        out_shape=[jax.ShapeDtypeStruct(v.shape, BF16),
                   jax.ShapeDtypeStruct((b,) + st_block[1:], F32)],
        scratch_shapes=[pltpu.VMEM((RET_HEADS, cb, cb), F32)],
        compiler_params=_cparams(("arbitrary", "arbitrary")),
        name="retention",
    )(q, k, v, g, s0)


def _scores(qq, kk):
    return lax.dot_general(qq, kk, (((1,), (1,)), ((), ())), preferred_element_type=F32)


def _softmax_seed(scores, values):
    ms = [jnp.max(s, axis=-1, keepdims=True) for s in scores]
    ps = [jnp.exp2(s - m) for s, m in zip(scores, ms)]
    out = []
    for m, p, vv in zip(ms, ps, values):
        out += [m, jnp.sum(p, axis=-1, keepdims=True),
                jnp.dot(p.astype(BF16), vv, preferred_element_type=F32)]
    return out


def _softmax_step(carry, scores, values):
    m2s = [jnp.maximum(carry[3 * c], jnp.max(s, axis=-1, keepdims=True)) for c, s in enumerate(scores)]
    ps = [jnp.exp2(s - m2) for s, m2 in zip(scores, m2s)]
    out = []
    for c, (m2, p, vv) in enumerate(zip(m2s, ps, values)):
        m, l, acc = carry[3 * c:3 * c + 3]
        a = jnp.exp2(m - m2)
        out += [m2, a * l + jnp.sum(p, axis=-1, keepdims=True),
                a * acc + jnp.dot(p.astype(BF16), vv, preferred_element_type=F32)]
    return out


def _sub_norm(o1, o2, lam, gs, coef):
    o = o1 - lam * o2
    return o * lax.rsqrt(jnp.mean(o * o, axis=-1, keepdims=True) + EPS) * gs * coef


SUM_ROWS = 16
VT_ROWS = HEAD_W + SUM_ROWS


def _softmax_seed_t(scores_t, values_t):
    ms = [jnp.max(s, axis=0, keepdims=True) for s in scores_t]
    ps = [jnp.exp2(s - m) for s, m in zip(scores_t, ms)]
    out = []
    for m, p, vt in zip(ms, ps, values_t):
        out += [m, jnp.dot(vt, p.astype(BF16), preferred_element_type=F32)]
    return out


def _softmax_step_t(carry, scores_t, values_t):
    m2s = [jnp.maximum(carry[2 * c], jnp.max(s, axis=0, keepdims=True)) for c, s in enumerate(scores_t)]
    ps = [jnp.exp2(s - m2) for s, m2 in zip(scores_t, m2s)]
    out = []
    for c, (m2, p, vt) in enumerate(zip(m2s, ps, values_t)):
        m, acc = carry[2 * c:2 * c + 2]
        out += [m2, jnp.exp2(m - m2) * acc + jnp.dot(vt, p.astype(BF16), preferred_element_type=F32)]
    return out


def _attn_prompt_kernel(lam_ref, q_ref, km_ref, vm_ref, ke_ref, ve_ref, gst_ref, o_ref, kb_ref, vt_ref,
                        *, sq, tq, hp, n_ext, coef):
    heads = [slice(hh * HEAD_W, (hh + 1) * HEAD_W) for hh in range(hp)]
    vrows = [slice(hh * VT_ROWS, (hh + 1) * VT_ROWS) for hh in range(hp)]
    kb_ref[...] = km_ref[0].astype(BF16)
    ones = jnp.ones((SUM_ROWS, tq), BF16)
    for j in range(sq // tq):
        for hh, sl in enumerate(heads):
            vt_ref[j, hh * VT_ROWS:hh * VT_ROWS + HEAD_W, :] = vm_ref[0, j * tq:(j + 1) * tq, sl].T.astype(BF16)
            vt_ref[j, hh * VT_ROWS + HEAD_W:(hh + 1) * VT_ROWS, :] = ones
    ke = [ke_ref[0, :, sl].astype(BF16) for sl in heads]
    vet = [jnp.concatenate([ve_ref[0, :, sl].T.astype(BF16), jnp.ones((SUM_ROWS, ve_ref.shape[1]), BF16)], axis=0)
           for sl in heads]
    lam = lam_ref[0]
    lo = lax.broadcasted_iota(jnp.int32, (1, HEAD_W), 1) < DIFF_DH
    ext_ok = lax.broadcasted_iota(jnp.int32, (ke_ref.shape[1], 1), 0) < n_ext
    diag = (lax.broadcasted_iota(jnp.int32, (tq, tq), 0) // CHUNK
            <= lax.broadcasted_iota(jnp.int32, (tq, tq), 1) // CHUNK)

    def q_tile(qi):
        q0 = qi * tq
        qs = []
        for sl in heads:
            q = q_ref[0, pl.ds(q0, tq), sl]
            zero = jnp.zeros_like(q)
            qs += [jnp.where(lo, q, zero), jnp.where(lo, zero, q)]

        ext_s = [jnp.where(ext_ok, _scores(ke[c // 2], qq), NEG) for c, qq in enumerate(qs)]
        carry = _softmax_seed_t(ext_s, [vet[c // 2] for c in range(len(qs))])

        def step(j, carry, mask=None):
            ss = [_scores(kb_ref[j * tq:(j + 1) * tq, heads[c // 2]], qq) for c, qq in enumerate(qs)]
            if mask is not None:
                ss = [jnp.where(mask, s, NEG) for s in ss]
            return tuple(_softmax_step_t(carry, ss, [vt_ref[j, vrows[c // 2], :] for c in range(len(qs))]))

        carry = tuple(carry)
        for j in range(qi):
            carry = step(j, carry)
        carry = step(qi, carry, diag)
        for hh, sl in enumerate(heads):
            _, a1, _, a2 = carry[4 * hh:4 * hh + 4]
            o = a1[:HEAD_W] / a1[HEAD_W:HEAD_W + 1] - lam * (a2[:HEAD_W] / a2[HEAD_W:HEAD_W + 1])
            on = o * lax.rsqrt(jnp.mean(o * o, axis=0, keepdims=True) + EPS) * gst_ref[...] * coef
            o_ref[0, q0:q0 + tq, sl] = on.T.astype(BF16)

    for qi in range(sq // tq):
        pl.when(pl.program_id(0) >= 0)(functools.partial(q_tile, qi))


def _attention_prompt(q, k_main, v_main, k_ext, v_ext, n_ext, lam, g_subln, coef, tq, hp):
    b, sq, hw = q.shape
    w = hp * HEAD_W
    bh = pl.BlockSpec((1, sq, w), lambda i, h: (i, 0, h))
    ext = pl.BlockSpec((1, k_ext.shape[1], w), lambda i, h: (0, 0, h))
    gst = jnp.broadcast_to(g_subln.reshape(HEAD_W, 1), (HEAD_W, tq))
    return pl.pallas_call(
        functools.partial(_attn_prompt_kernel, sq=sq, tq=tq, hp=hp, n_ext=n_ext, coef=coef),
        grid=(b, hw // w),
        in_specs=[pl.BlockSpec(memory_space=pltpu.SMEM), bh, bh, bh, ext, ext,
                  pl.BlockSpec((HEAD_W, tq), lambda i, h: (0, 0))],
        out_specs=bh,
        out_shape=jax.ShapeDtypeStruct(q.shape, BF16),
        scratch_shapes=[pltpu.VMEM((sq, w), BF16), pltpu.VMEM((sq // tq, hp * VT_ROWS, tq), BF16)],
        compiler_params=_cparams(("arbitrary", "arbitrary")),
        name="attention_prompt",
    )(lam, q, k_main, v_main, k_ext, v_ext, gst)


def _attn_sample_kernel(lam_ref, q_ref, kc_ref, vc_ref, ke_ref, ve_ref, gs_ref, o_ref, m_ref, l_ref, acc_ref,
                        *, ds, nh, n_ext, coef):
    t = pl.program_id(1)
    tkv = kc_ref.shape[1] // nh
    lo = lax.broadcasted_iota(jnp.int32, (1, HEAD_W), 1) < DIFF_DH
    ext_ok = lax.broadcasted_iota(jnp.int32, (1, ke_ref.shape[1]), 1) < n_ext
    heads = [slice(hh * HEAD_W, (hh + 1) * HEAD_W) for hh in range(nh)]

    def stacked_q(sl):
        q = q_ref[0, :, sl]
        zero = jnp.zeros_like(q)
        return jnp.concatenate([jnp.where(lo, q, zero), jnp.where(lo, zero, q)], axis=0)

    qs = [stacked_q(sl) for sl in heads]

    def write_stats(stats):
        for hh in range(nh):
            m_ref[hh], l_ref[hh], acc_ref[hh] = stats[3 * hh:3 * hh + 3]

    @pl.when(t == 0)
    def _():
        ext_s = [jnp.where(ext_ok, _scores(qq, ke_ref[0, :, sl].astype(BF16)), NEG) for qq, sl in zip(qs, heads)]
        write_stats(_softmax_seed(ext_s, [ve_ref[0, :, sl].astype(BF16) for sl in heads]))

    carry = []
    for hh in range(nh):
        carry += [m_ref[hh], l_ref[hh], acc_ref[hh]]
    ss = [_scores(qq, kc_ref[0, pl.ds(hh, tkv, stride=nh), :].astype(BF16)) for hh, qq in enumerate(qs)]
    write_stats(_softmax_step(carry, ss, [vc_ref[0, pl.ds(hh, tkv, stride=nh), :].astype(BF16) for hh in range(nh)]))

    @pl.when(t == pl.num_programs(1) - 1)
    def _():
        lam = lam_ref[0]
        for hh, sl in enumerate(heads):
            o = acc_ref[hh] / l_ref[hh]
            o_ref[0, :, sl] = _sub_norm(o[:ds], o[ds:], lam, gs_ref[...], coef).astype(BF16)


def _attention_sample(q, k_cache, v_cache, k_ext, v_ext, n_ext, lam, g_subln, coef, tkv):
    b, ds, hw = q.shape
    _, pl_len, nh, _ = k_cache.shape
    per_b = lambda rows: pl.BlockSpec((1, rows, hw), lambda i, t: (i, 0, 0))
    cache = pl.BlockSpec((1, tkv * nh, HEAD_W), lambda i, t: (i, t, 0))
    return pl.pallas_call(
        functools.partial(_attn_sample_kernel, ds=ds, nh=nh, n_ext=n_ext, coef=coef),
        grid=(b, pl_len // tkv),
        in_specs=[pl.BlockSpec(memory_space=pltpu.SMEM), per_b(ds), cache, cache,
                  per_b(k_ext.shape[1]), per_b(k_ext.shape[1]),
                  pl.BlockSpec((1, HEAD_W), lambda i, t: (0, 0))],
        out_specs=per_b(ds),
        out_shape=jax.ShapeDtypeStruct(q.shape, BF16),
        scratch_shapes=[pltpu.VMEM((nh, 2 * ds, 1), F32), pltpu.VMEM((nh, 2 * ds, 1), F32),
                        pltpu.VMEM((nh, 2 * ds, HEAD_W), F32)],
        compiler_params=_cparams(("arbitrary", "arbitrary")),
        name="attention_sample",
    )(lam, q, k_cache.reshape(b, pl_len * nh, HEAD_W), v_cache.reshape(b, pl_len * nh, HEAD_W),
      k_ext, v_ext, g_subln)


def _mix_kernel(x_ref, yr_ref, od_ref, ga_ref, gb_ref, wr_ref, wd_ref, wo_ref, gf_ref, wrt_ref, br_ref,
                x1_ref, h_ref, lgt_ref):
    ya = jnp.dot(yr_ref[...], wr_ref[...], preferred_element_type=F32)
    yb = jnp.dot(od_ref[...], wd_ref[...], preferred_element_type=F32)
    z = jax.nn.sigmoid(ga_ref[...].astype(F32)) * ya + jax.nn.sigmoid(gb_ref[...].astype(F32)) * yb
    x1 = x_ref[...] + jnp.dot(z.astype(BF16), wo_ref[...], preferred_element_type=F32)
    x1_ref[...] = x1
    h = x1 * lax.rsqrt(jnp.mean(x1 * x1, axis=-1, keepdims=True) + EPS) * gf_ref[...]
    _tile_rows_store(h_ref, h)
    lgt_ref[...] = _scores(wrt_ref[...], h.astype(BF16)) + br_ref[...]


def _mix_tail_kernel(*refs):
    th_ref, tlgt_ref = refs[11:13]
    h_ref, lgt_ref = refs[14:16]
    last = pl.num_programs(0) - 1

    @pl.when(pl.program_id(0) < last)
    def _():
        _mix_kernel(*refs[:11], *refs[13:])

    @pl.when(pl.program_id(0) == last)
    def _():
        h_ref[...] = th_ref[...]
        lgt_ref[...] = tlgt_ref[...]


def _mix_ret_kernel(x_ref, rq_ref, rk_ref, rv_ref, rg_ref, s0_ref, od_ref, ga_ref, gb_ref,
                    wr_ref, wd_ref, wo_ref, gf_ref, wrt_ref, br_ref, th_ref, tlgt_ref,
                    x1_ref, h_ref, lgt_ref, st_ref, decay_ref, yr_ref, *, cb, tiles_per_seq):
    i = pl.program_id(0)
    last = pl.num_programs(0) - 1

    @pl.when(i == 0)
    def _():
        _fill_decay(decay_ref, cb)

    @pl.when(i < last)
    def _():
        @pl.when(i % tiles_per_seq == 0)
        def _():
            st_ref[...] = s0_ref[...]

        refs = {"q": rq_ref, "k": rk_ref, "v": rv_ref, "g": rg_ref}
        for sub in range(yr_ref.shape[0] // cb):
            rows = slice(sub * cb, (sub + 1) * cb)

            def store(cols, y, rows=rows):
                yr_ref[rows, cols] = y

            _ret_block(lambda name, cols, rows=rows: refs[name][rows, cols], store, st_ref, decay_ref, cb, cb)
        _mix_kernel(x_ref, yr_ref, od_ref, ga_ref, gb_ref, wr_ref, wd_ref, wo_ref, gf_ref, wrt_ref, br_ref,
                    x1_ref, h_ref, lgt_ref)

    @pl.when(i == last)
    def _():
        h_ref[...] = th_ref[...]
        lgt_ref[...] = tlgt_ref[...]


def _mix_ret(x, rq, rk, rv, rg, s0, od, ga, gb, wr, wd, wo, gf, wrt, br, tm, tail, seq, cb):
    t, d = x.shape
    ne = wrt.shape[0]
    nt = t // tm
    tiles_per_seq = seq // tm
    row = lambda w: pl.BlockSpec((tm, w), lambda i: (jnp.minimum(i, nt - 1), 0))
    full = lambda a: pl.BlockSpec(a.shape, lambda i: (0,) * a.ndim)
    st_block = (1, RET_HEADS, RET_DK, RET_DV)
    weights = (wr, wd, wo, gf, wrt, br) + tuple(tail)
    return pl.pallas_call(
        functools.partial(_mix_ret_kernel, cb=cb, tiles_per_seq=tiles_per_seq),
        grid=(nt + 1,),
        in_specs=[row(d), row(rq.shape[1]), row(rk.shape[1]), row(rv.shape[1]), row(rg.shape[1]), full(s0),
                  row(od.shape[1]), row(d), row(d)] + [full(a) for a in weights],
        out_specs=[row(d), pl.BlockSpec((tm * SUBLANES, LANES), lambda i: (i, 0)),
                   pl.BlockSpec((ne, tm), lambda i: (0, i)),
                   pl.BlockSpec(st_block, lambda i: (jnp.minimum(i, nt - 1) // tiles_per_seq, 0, 0, 0))],
        out_shape=[jax.ShapeDtypeStruct((t, d), F32),
                   jax.ShapeDtypeStruct(((t + tm) * SUBLANES, LANES), F32),
                   jax.ShapeDtypeStruct((ne, t + tm), F32),
                   jax.ShapeDtypeStruct((t // seq,) + st_block[1:], F32)],
        scratch_shapes=[pltpu.VMEM((RET_HEADS, cb, cb), F32), pltpu.VMEM((tm, rv.shape[1]), BF16)],
        compiler_params=_cparams(("arbitrary",)),
        name="mix_retention",
    )(x, rq, rk, rv, rg, s0, od, ga, gb, *weights)


def _mix(x, yr, od, ga, gb, wr, wd, wo, gf, wrt, br, tm, tail=None):
    t, d = x.shape
    ne = wrt.shape[0]
    nt = t // tm
    extra = 0 if tail is None else 1
    row = lambda w: pl.BlockSpec((tm, w), lambda i: (jnp.minimum(i, nt - 1), 0))
    full = lambda a: pl.BlockSpec(a.shape, lambda i: (0, 0))
    ops = (x, yr, od, ga, gb, wr, wd, wo, gf, wrt, br) + (() if tail is None else tuple(tail))
    return pl.pallas_call(
        _mix_kernel if tail is None else _mix_tail_kernel,
        grid=(nt + extra,),
        in_specs=[row(d), row(yr.shape[1]), row(od.shape[1]), row(d), row(d)] + [full(a) for a in ops[5:]],
        out_specs=[row(d), pl.BlockSpec((tm * SUBLANES, LANES), lambda i: (i, 0)),
                   pl.BlockSpec((ne, tm), lambda i: (0, i))],
        out_shape=[jax.ShapeDtypeStruct((t, d), F32),
                   jax.ShapeDtypeStruct(((t + extra * tm) * SUBLANES, LANES), F32),
                   jax.ShapeDtypeStruct((ne, t + extra * tm), F32)],
        compiler_params=_cparams(("arbitrary",)),
        name="mix",
    )(*ops)


def _route_kernel(lgt_ref, e_ref, gate_ref, rank_ref, cnt_ref):
    @pl.when(pl.program_id(0) == 0)
    def _():
        cnt_ref[...] = jnp.zeros_like(cnt_ref)

    v = lgt_ref[...]
    ne, tn = v.shape
    eid = lax.broadcasted_iota(jnp.int32, (ne, tn), 0)
    tops, sels = [], []
    for k in range(TOP_K):
        m = jnp.max(v, axis=0, keepdims=True)
        idx = jnp.min(jnp.where(v == m, eid, ne), axis=0, keepdims=True)
        sel = eid == idx
        e_ref[k:k + 1, :] = idx
        tops.append(m)
        sels.append(sel)
        v = jnp.where(sel, -jnp.inf, v)
    ex = [jnp.exp(m - tops[0]) for m in tops]
    den = ex[0] + ex[1] + ex[2] + ex[3]
    for k in range(TOP_K):
        gate_ref[k:k + 1, :] = ex[k] / den

    chosen = sels[0] | sels[1] | sels[2] | sels[3]
    before = (lax.broadcasted_iota(jnp.int32, (tn, tn), 0) < lax.broadcasted_iota(jnp.int32, (tn, tn), 1))
    prior = jnp.dot(chosen.astype(BF16), before.astype(BF16), preferred_element_type=F32) + cnt_ref[:, 0:1]
    for k in range(TOP_K):
        rank_ref[k:k + 1, :] = jnp.sum(jnp.where(sels[k], prior, 0.0), axis=0, keepdims=True).astype(jnp.int32)
    cnt_ref[...] = cnt_ref[...] + jnp.sum(chosen.astype(F32), axis=1, keepdims=True)


def _route_topk(lgt, tn):
    ne, t = lgt.shape
    kt = pl.BlockSpec((TOP_K, tn), lambda i: (0, i))
    return pl.pallas_call(
        _route_kernel,
        grid=(t // tn,),
        in_specs=[pl.BlockSpec((ne, tn), lambda i: (0, i))],
        out_specs=[kt, kt, kt, pl.BlockSpec((ne, LANES), lambda i: (0, 0))],
        out_shape=[jax.ShapeDtypeStruct((TOP_K, t), jnp.int32), jax.ShapeDtypeStruct((TOP_K, t), F32),
                   jax.ShapeDtypeStruct((TOP_K, t), jnp.int32), jax.ShapeDtypeStruct((ne, LANES), F32)],
        compiler_params=_cparams(("arbitrary",)),
        name="route",
    )(lgt)


GATHER_UNROLL = 8
MOE_GATHER_PRIORITIES = (1,)


def _gather_rows(src_hbm, idx_ref, n, buf_ref, slot, sem, priorities=(0, 1)):
    def body(g, _):
        for u in range(GATHER_UNROLL):
            r = g * GATHER_UNROLL + u
            src = pl.multiple_of(idx_ref[0, 0, r], SUBLANES)
            dst = pl.multiple_of(r * SUBLANES, SUBLANES)
            pltpu.make_async_copy(src_hbm.at[pl.ds(src, SUBLANES)], buf_ref.at[slot, pl.ds(dst, SUBLANES)],
                                  sem.at[slot]).start(priority=priorities[u % len(priorities)])
        return 0
    lax.fori_loop(0, n // GATHER_UNROLL, body, 0)


def _wait_rows(src_hbm, n, buf_ref, slot, sem):
    pltpu.make_async_copy(src_hbm.at[pl.ds(0, n * SUBLANES)], buf_ref.at[slot], sem.at[slot]).wait()


MOE_SLOTS = 3


def _moe_kernel(be_ref, nu_ref, tok_ref, tok1_ref, tok2_ref, h_hbm, wgu_ref, bgu_ref, wdn_ref, bdn_ref, out_ref,
                buf_ref, wgu_bf, wdn_bf, sem):
    i = pl.program_id(0)
    n_used = nu_ref[0]
    slot = i % MOE_SLOTS
    d_ff = wdn_bf.shape[0]

    @pl.when(i == 0)
    def _():
        _gather_rows(h_hbm, tok_ref, MOE_ROWS, buf_ref, 0, sem, MOE_GATHER_PRIORITIES)

        @pl.when(1 < n_used)
        def _():
            _gather_rows(h_hbm, tok1_ref, MOE_ROWS, buf_ref, 1, sem, MOE_GATHER_PRIORITIES)

    @pl.when(i + 2 < n_used)
    def _():
        _gather_rows(h_hbm, tok2_ref, MOE_ROWS, buf_ref, (i + 2) % MOE_SLOTS, sem, MOE_GATHER_PRIORITIES)

    @pl.when(i < n_used)
    def _():
        @pl.when((i == 0) | (be_ref[i] != be_ref[jnp.maximum(i - 1, 0)]))
        def _():
            wgu_bf[...] = wgu_ref[0].astype(BF16)
            wdn_bf[...] = wdn_ref[0].astype(BF16)

        _wait_rows(h_hbm, MOE_ROWS, buf_ref, slot, sem)

        xb = _tile_rows_load(buf_ref, slot, 0, MOE_ROWS).astype(BF16)
        gu = jnp.dot(xb, wgu_bf[...], preferred_element_type=F32) + bgu_ref[0]
        gate = jnp.minimum(gu[:, :d_ff], SWIGLU_LIMIT)
        up = jnp.clip(gu[:, d_ff:], -SWIGLU_LIMIT, SWIGLU_LIMIT)
        act = (up + 1.0) * gate * jax.nn.sigmoid(SWIGLU_ALPHA * gate)
        _tile_rows_store(out_ref, jnp.dot(act.astype(BF16), wdn_bf[...], preferred_element_type=F32) + bdn_ref[0])

    @pl.when(i >= n_used)
    def _():
        out_ref[...] = jnp.zeros_like(out_ref)


def _moe(h, tok, block_e, n_used, w_gu, b_gu, w_down, b_down):
    d = h.shape[1] * SUBLANES
    nb = tok.shape[0]
    ne, _, two_ff = w_gu.shape
    d_ff = two_ff // 2
    e_map = lambda i, be, nu: (be[i], 0, 0)
    grid_spec = pltpu.PrefetchScalarGridSpec(
        num_scalar_prefetch=2,
        grid=(nb,),
        in_specs=[
            pl.BlockSpec((1, 1, MOE_ROWS), lambda i, be, nu: (i, 0, 0), memory_space=pltpu.SMEM),
            pl.BlockSpec((1, 1, MOE_ROWS), lambda i, be, nu: (jnp.minimum(i + 1, nb - 1), 0, 0),
                         memory_space=pltpu.SMEM),
            pl.BlockSpec((1, 1, MOE_ROWS), lambda i, be, nu: (jnp.minimum(i + 2, nb - 1), 0, 0),
                         memory_space=pltpu.SMEM),
            pl.BlockSpec(memory_space=pl.ANY),
            pl.BlockSpec((1, d, two_ff), e_map),
            pl.BlockSpec((1, 1, two_ff), e_map),
            pl.BlockSpec((1, d_ff, d), e_map),
            pl.BlockSpec((1, 1, d), e_map),
        ],
        out_specs=pl.BlockSpec((MOE_ROWS * SUBLANES, LANES), lambda i, be, nu: (i, 0)),
        scratch_shapes=[pltpu.VMEM((MOE_SLOTS, MOE_ROWS * SUBLANES, LANES), F32),
                        pltpu.VMEM((d, two_ff), BF16),
                        pltpu.VMEM((d_ff, d), BF16),
                        pltpu.SemaphoreType.DMA((MOE_SLOTS,))],
    )
    return pl.pallas_call(
        _moe_kernel,
        grid_spec=grid_spec,
        out_shape=jax.ShapeDtypeStruct((nb * MOE_ROWS * SUBLANES, LANES), F32),
        compiler_params=_cparams(("arbitrary",)),
        name="moe",
    )(block_e, n_used, tok, tok, tok, h, w_gu, b_gu.reshape(ne, 1, two_ff), w_down, b_down.reshape(ne, 1, d))


def _combine_kernel(dst_ref, dstn_ref, x_ref, gate_ref, gf_ref, rows_hbm, y_ref, buf_ref, sem, *, tm, nsteps):
    i = pl.program_id(0)
    slot = i % 2
    n = tm * TOP_K

    @pl.when(i == 0)
    def _():
        _gather_rows(rows_hbm, dst_ref, n, buf_ref, 0, sem)

    @pl.when(i + 1 < nsteps)
    def _():
        _gather_rows(rows_hbm, dstn_ref, n, buf_ref, 1 - slot, sem)

    _wait_rows(rows_hbm, n, buf_ref, slot, sem)

    y = x_ref[...]
    gate = gate_ref[...]
    for k in range(TOP_K):
        y = y + gate[:, k:k + 1] * _tile_rows_load(buf_ref, slot, k * tm, tm)
    y_ref[...] = y * lax.rsqrt(jnp.mean(y * y, axis=-1, keepdims=True) + EPS) * gf_ref[...]


def _combine(x1, gates, dest, rows, g_final, tm):
    t, d = x1.shape
    nsteps = t // tm
    n = tm * TOP_K
    return pl.pallas_call(
        functools.partial(_combine_kernel, tm=tm, nsteps=nsteps),
        grid=(nsteps,),
        in_specs=[pl.BlockSpec((1, 1, n), lambda i: (i, 0, 0), memory_space=pltpu.SMEM),
                  pl.BlockSpec((1, 1, n), lambda i: (jnp.minimum(i + 1, nsteps - 1), 0, 0),
                               memory_space=pltpu.SMEM),
                  pl.BlockSpec((tm, d), lambda i: (i, 0)),
                  pl.BlockSpec((tm, TOP_K), lambda i: (i, 0)),
                  pl.BlockSpec((1, d), lambda i: (0, 0)),
                  pl.BlockSpec(memory_space=pl.ANY)],
        out_specs=pl.BlockSpec((tm, d), lambda i: (i, 0)),
        out_shape=jax.ShapeDtypeStruct((t, d), F32),
        scratch_shapes=[pltpu.VMEM((2, n * SUBLANES, LANES), F32), pltpu.SemaphoreType.DMA((2,))],
        compiler_params=_cparams(("arbitrary",)),
        name="combine",
    )(dest, dest, x1, gates, g_final, rows)


def _route(lgt, tn):
    t = lgt.shape[1]
    n_assign = t * TOP_K
    e4, gates, rank, cnt = _route_topk(lgt, tn)
    counts = cnt[:, 0].astype(jnp.int32)
    padded = (counts + MOE_ROWS - 1) // MOE_ROWS * MOE_ROWS
    pend = jnp.cumsum(padded)
    pstart = pend - padded
    start = jnp.cumsum(counts) - counts
    e_ids = jnp.arange(N_EXPERTS, dtype=jnp.int32)
    dest = rank + jnp.sum(jnp.where(e4[None] == e_ids[:, None, None], pstart[:, None, None], 0), axis=0)
    nb = n_assign // MOE_ROWS + N_EXPERTS
    n_used = (pend[-1] // MOE_ROWS).astype(jnp.int32).reshape(1)
    blk0 = jnp.arange(nb, dtype=jnp.int32) * MOE_ROWS
    block_e = jnp.minimum(jnp.sum((pend[None, :] <= blk0[:, None]).astype(jnp.int32), axis=1), N_EXPERTS - 1)
    of_block = lambda table: jnp.sum(jnp.where(block_e[:, None] == e_ids[None, :], table[None, :], 0), axis=1)
    order = jnp.argsort(e4.T.reshape(-1))
    within = (blk0 - of_block(pstart))[:, None] + jnp.arange(MOE_ROWS, dtype=jnp.int32)[None, :]
    src = jnp.clip(of_block(start)[:, None] + jnp.minimum(within, of_block(counts)[:, None] - 1), 0, n_assign - 1)
    tok = (order[src] // TOP_K).astype(jnp.int32)
    return gates.T, dest * SUBLANES, (tok * SUBLANES).reshape(nb, 1, MOE_ROWS), block_e, n_used


def kernel(x_prompt, x_sample, cache_k, cache_v, state_ret, meta_tokens, g_mix, w_in,
           lam_q1, lam_k1, lam_q2, lam_k2, g_subln, w_ret_out, w_diff_out, w_out,
           g_ffn, w_router, b_router, w_gu, b_gu, w_down, b_down, g_final):
    b, s, d = x_prompt.shape
    db, ds, _ = x_sample.shape
    depth, _, pl_len = cache_k.shape[:3]
    assert depth == 1, "single-layer step only"
    nm = meta_tokens.shape[0]
    hw = DIFF_HEADS * HEAD_W

    lam_init = 0.8 - 0.6 * math.exp(-0.3 * 0)
    lam = (jnp.exp(jnp.sum(lam_q1[0] * lam_k1[0])) - jnp.exp(jnp.sum(lam_q2[0] * lam_k2[0]))
           + lam_init).reshape(1).astype(F32)
    coef = 1.0 - lam_init

    w_in_bf = w_in[0].astype(BF16)
    g_mix2 = g_mix[0].reshape(1, d)

    tm = 512 if (b * s) % 512 == 0 else s
    tab_p = _tables(N_META + jnp.arange(s, dtype=jnp.int32))
    rq, rk, rv, rg, dq, dk, dv, ga, gb = _proj(x_prompt.reshape(b * s, d), g_mix2, w_in_bf, tab_p, tm, s // tm)

    pos_small = jnp.concatenate([jnp.tile(N_META + pl_len + jnp.arange(ds, dtype=jnp.int32), db),
                                 jnp.arange(nm, dtype=jnp.int32)])
    x_small = jnp.concatenate([x_sample.reshape(db * ds, d), meta_tokens.astype(F32)], axis=0)
    small = _proj(x_small, g_mix2, w_in_bf, _tables(pos_small), x_small.shape[0], 1)
    ns = db * ds
    srq, srk, srv, srg, sdq, sdk, sdv, sga, sgb = [a[:ns] for a in small]
    mrq, mrk, mrv, mrg, _, mdk, mdv, _, _ = [a[ns:] for a in small]

    cb_small = 128

    def pad_rows(a, nb_, n_):
        a = a.reshape(nb_, n_, a.shape[-1])
        return jnp.pad(a, ((0, 0), (0, cb_small - n_), (0, 0)))

    zero_state = jnp.zeros((1, RET_HEADS, RET_DK, RET_DV), F32)
    _, s_meta = _retention(pad_rows(mrq, 1, nm), pad_rows(mrk, 1, nm), pad_rows(mrv, 1, nm),
                           pad_rows(mrg, 1, nm), zero_state, cb_small, nm)
    cb = 256
    r3 = lambda a: a.reshape(b, s, a.shape[-1])
    syr, ret_s = _retention(pad_rows(srq, db, ds), pad_rows(srk, db, ds), pad_rows(srv, db, ds),
                            pad_rows(srg, db, ds), state_ret[0], cb_small, ds)
    syr = syr[:, :ds].reshape(ns, -1)

    g_sub = g_subln[0].reshape(1, HEAD_W)
    ext_rows = 128
    pad_ext = lambda a: jnp.pad(a, ((0, 0), (0, ext_rows - a.shape[1]), (0, 0)))
    od = _attention_prompt(r3(dq), r3(dk), r3(dv), pad_ext(mdk[None]), pad_ext(mdv[None]), nm,
                           lam, g_sub, coef, 256, 4)
    sdk3, sdv3 = sdk.reshape(db, ds, hw), sdv.reshape(db, ds, hw)
    ke_s = jnp.concatenate([jnp.broadcast_to(mdk[None], (db, nm, hw)), sdk3], axis=1)
    ve_s = jnp.concatenate([jnp.broadcast_to(mdv[None], (db, nm, hw)), sdv3], axis=1)
    sod = _attention_sample(sdq.reshape(db, ds, hw), cache_k[0], cache_v[0], pad_ext(ke_s), pad_ext(ve_s),
                            nm + ds, lam, g_sub, coef, min(2048, pl_len))

    wr, wd, wo = w_ret_out[0].astype(BF16), w_diff_out[0].astype(BF16), w_out[0].astype(BF16)
    gf = g_ffn[0].reshape(1, d)
    wrt = w_router[0].T.astype(BF16)
    br = b_router[0].astype(F32).reshape(N_EXPERTS, 1)
    sx1, sh, slgt = _mix(x_sample.reshape(ns, d), syr, sod.reshape(ns, hw), sga, sgb, wr, wd, wo, gf, wrt, br, ns)
    x1, h, lgt, ret_p = _mix_ret(x_prompt.reshape(b * s, d), rq, rk, rv, rg, s_meta, od.reshape(b * s, hw), ga, gb,
                                 wr, wd, wo, gf, wrt, br, ns, (sh, slgt), s, min(cb, ns))

    t = b * s
    gates, dest, tok, block_e, n_used = _route(lgt, ns)
    rows = _moe(h, tok, block_e, n_used, w_gu[0], b_gu[0], w_down[0], b_down[0])
    gfin = g_final.reshape(1, d)

    def combine(x1g, lo, n):
        tc = min(512, n)
        dg = dest[:, lo:lo + n].reshape(TOP_K, n // tc, tc).swapaxes(0, 1).reshape(n // tc, 1, TOP_K * tc)
        return _combine(x1g, gates[lo:lo + n], dg, rows, gfin, tc)

    y_p = combine(x1, 0, t)
    y_s = combine(sx1, t, ns)

    k_p = jnp.concatenate([jnp.broadcast_to(mdk[None], (b, nm, hw)), r3(dk)], axis=1)
    v_p = jnp.concatenate([jnp.broadcast_to(mdv[None], (b, nm, hw)), r3(dv)], axis=1)
    shp = lambda a: a.reshape(1, a.shape[0], a.shape[1], DIFF_HEADS, HEAD_W)
    return (y_p.reshape(b, s, d), y_s.reshape(db, ds, d), ret_p[None], shp(k_p), shp(v_p),
            ret_s[None], shp(sdk3), shp(sdv3))
```

```python
import functools
import math

import jax
import jax.numpy as jnp
from jax import lax
from jax.experimental import pallas as pl
from jax.experimental.pallas import tpu as pltpu

F32 = jnp.float32
BF16 = jnp.bfloat16

EPS = 1e-6
N_META = 16
CHUNK = 64
RET_HEADS = 4
RET_DK = 128
RET_DV = 256
RET_THETA = 10000.0
DIFF_HEADS = 8
DIFF_DH = 64
ROT_DIM = DIFF_DH // 4
ROPE_THETA = 500000.0
N_EXPERTS = 32
TOP_K = 4
SWIGLU_LIMIT = 7.0
SWIGLU_ALPHA = 1.702

LANES = 128
SUBLANES = 8
HEAD_W = 2 * DIFF_DH
MOE_ROWS = 512
VMEM_LIMIT = 56 * 1024 * 1024
NEG = -1e30
Q_SCALE = DIFF_DH ** -0.5 * math.log2(math.e)


def _cparams(sem):
    return pltpu.CompilerParams(dimension_semantics=sem, vmem_limit_bytes=VMEM_LIMIT)


def _tile_rows_store(ref, val):
    n = val.shape[0]
    for c in range(SUBLANES):
        ref[pl.ds(c, n, stride=SUBLANES), :] = val[:, c * LANES:(c + 1) * LANES]


def _tile_rows_load(ref, slot, first, n):
    return jnp.concatenate([ref[slot, pl.ds(first * SUBLANES + c, n, stride=SUBLANES), :]
                            for c in range(SUBLANES)], axis=1)


def _ret_tables(pos):
    angle = RET_THETA ** (-jnp.linspace(0.0, 1.0, RET_DK // 2, dtype=F32))
    angle = jnp.repeat(angle, 2)
    ang = pos.astype(F32)[:, None] * angle[None, :]
    cos, sin = jnp.cos(ang), jnp.sin(ang)
    even = (jnp.arange(RET_DK) % 2 == 0)[None, :]
    return cos, jnp.where(even, -sin, 0.0), jnp.where(even, 0.0, sin)


def _rope_tables(pos):
    half = ROT_DIM // 2
    inv = ROPE_THETA ** (-jnp.arange(half, dtype=F32) * (2.0 / ROT_DIM))
    ang = pos.astype(F32)[:, None] * inv[None, :]
    cos, sin = jnp.cos(ang), jnp.sin(ang)
    jj = jnp.arange(HEAD_W) % DIFF_DH
    first = (jj < half)[None, :]
    second = ((jj >= half) & (jj < ROT_DIM))[None, :]
    cos_l = jnp.take(cos, jj % half, axis=1)
    sin_l = jnp.take(sin, jj % half, axis=1)
    c = jnp.where(first | second, cos_l, 1.0)
    sa = jnp.where(first, -sin_l, 0.0)
    sb = jnp.where(second, sin_l, 0.0)
    return c, sa, sb


def _tables(pos):
    return jnp.concatenate(_ret_tables(pos) + _rope_tables(pos), axis=1)


def _proj_kernel(x_ref, g_ref, w_ref, tab_ref, rq_ref, rk_ref, rv_ref, rg_ref,
                 dq_ref, dk_ref, dv_ref, ga_ref, gb_ref):
    x = x_ref[...]
    ms = jnp.mean(x * x, axis=-1, keepdims=True)
    h = (x * lax.rsqrt(ms + EPS) * g_ref[...]).astype(BF16)

    def mm(c0, width):
        return jnp.dot(h, w_ref[:, c0:c0 + width], preferred_element_type=F32)

    def rot(p, t0, near, far):
        c = tab_ref[:, t0:t0 + LANES]
        sa = tab_ref[:, t0 + LANES:t0 + 2 * LANES]
        sb = tab_ref[:, t0 + 2 * LANES:t0 + 3 * LANES]
        return p * c + pltpu.roll(p, LANES - near, 1) * sa + pltpu.roll(p, far, 1) * sb

    kw = RET_HEADS * RET_DK
    vw = RET_HEADS * RET_DV
    dw = DIFF_HEADS * HEAD_W
    c0 = 0
    p = mm(c0, kw)
    for hh in range(RET_HEADS):
        sl = slice(hh * LANES, (hh + 1) * LANES)
        rq_ref[:, sl] = rot(p[:, sl], 0, 1, 1).astype(BF16)
    c0 += kw
    p = mm(c0, kw)
    for hh in range(RET_HEADS):
        sl = slice(hh * LANES, (hh + 1) * LANES)
        rk_ref[:, sl] = (rot(p[:, sl], 0, 1, 1) * (RET_DK ** -0.5)).astype(BF16)
    c0 += kw
    rv_ref[...] = mm(c0, vw).astype(BF16)
    c0 += vw
    rg_ref[...] = mm(c0, vw).astype(BF16)
    c0 += vw
    half = ROT_DIM // 2
    p = mm(c0, dw)
    for hh in range(DIFF_HEADS):
        sl = slice(hh * LANES, (hh + 1) * LANES)
        dq_ref[:, sl] = (rot(p[:, sl], 3 * LANES, half, half) * Q_SCALE).astype(BF16)
    c0 += dw
    p = mm(c0, dw)
    for hh in range(DIFF_HEADS):
        sl = slice(hh * LANES, (hh + 1) * LANES)
        dk_ref[:, sl] = rot(p[:, sl], 3 * LANES, half, half)
    c0 += dw
    dv_ref[...] = mm(c0, dw)
    c0 += dw
    d = x.shape[1]
    ga_ref[...] = mm(c0, d).astype(BF16)
    c0 += d
    gb_ref[...] = mm(c0, d).astype(BF16)


def _proj(x, g, w_bf, tab, tm, tab_blocks):
    t, d = x.shape
    kw, vw, dw = RET_HEADS * RET_DK, RET_HEADS * RET_DV, DIFF_HEADS * HEAD_W
    widths = (kw, kw, vw, vw, dw, dw, dw, d, d)
    dtypes = (BF16, BF16, BF16, BF16, BF16, F32, F32, BF16, BF16)
    row = lambda i: (i, 0)
    const = lambda i: (0, 0)
    return pl.pallas_call(
        _proj_kernel,
        grid=(t // tm,),
        in_specs=[pl.BlockSpec((tm, d), row),
                  pl.BlockSpec((1, d), const),
                  pl.BlockSpec(w_bf.shape, const),
                  pl.BlockSpec((tm, tab.shape[1]), lambda i: (i % tab_blocks, 0))],
        out_specs=[pl.BlockSpec((tm, w), row) for w in widths],
        out_shape=[jax.ShapeDtypeStruct((t, w), dt) for w, dt in zip(widths, dtypes)],
        compiler_params=_cparams(("arbitrary",)),
        name="proj",
    )(x, g, w_bf, tab)


def _ret_log_gammas():
    return tuple(math.log(1.0 - 2.0 ** (-5.0 - hh)) for hh in range(RET_HEADS))


def _fill_decay(decay_ref, cb):
    row = lax.broadcasted_iota(jnp.int32, (cb, cb), 0)
    col = lax.broadcasted_iota(jnp.int32, (cb, cb), 1)
    dist = (row - col).astype(F32)
    for hh, lg in enumerate(_ret_log_gammas()):
        decay_ref[hh] = jnp.where(dist >= 0, jnp.exp(lg * jnp.maximum(dist, 0.0)), 0.0)


def _ret_block(load, store, st_ref, decay_ref, cb, n_valid):
    idx = lax.broadcasted_iota(jnp.int32, (cb, 1), 0).astype(F32)
    for hh, lg in enumerate(_ret_log_gammas()):
        kcols = slice(hh * RET_DK, (hh + 1) * RET_DK)
        vcols = slice(hh * RET_DV, (hh + 1) * RET_DV)
        q, k, v = load("q", kcols), load("k", kcols), load("v", vcols)
        g = load("g", vcols).astype(F32)
        st = st_ref[0, hh]
        s = lax.dot_general(q, k, (((1,), (1,)), ((), ())), preferred_element_type=F32) * decay_ref[hh]
        o = jnp.dot(s.astype(BF16), v, preferred_element_type=F32)
        o = o + jnp.dot(q, st.astype(BF16), preferred_element_type=F32) * jnp.exp(lg * (idx + 1.0))
        kw = (k.astype(F32) * jnp.exp(lg * (n_valid - 1.0 - idx))).astype(BF16)
        st_ref[0, hh] = math.exp(lg * n_valid) * st + lax.dot_general(
            kw, v, (((0,), (0,)), ((), ())), preferred_element_type=F32)
        on = o * lax.rsqrt(jnp.mean(o * o, axis=-1, keepdims=True) + EPS)
        store(vcols, (g * jax.nn.sigmoid(g) * on).astype(BF16))


def _ret_kernel(q_ref, k_ref, v_ref, g_ref, s0_ref, y_ref, st_ref, decay_ref, *, cb, n_valid):
    @pl.when((pl.program_id(0) == 0) & (pl.program_id(1) == 0))
    def _():
        _fill_decay(decay_ref, cb)

    @pl.when(pl.program_id(1) == 0)
    def _():
        st_ref[...] = s0_ref[...]

    refs = {"q": q_ref, "k": k_ref, "v": v_ref, "g": g_ref}

    def store(cols, y):
        y_ref[0, :, cols] = y

    _ret_block(lambda name, cols: refs[name][0, :, cols], store, st_ref, decay_ref, cb, n_valid)


def _retention(q, k, v, g, s0, cb, n_valid):
    b, s, _ = q.shape
    s0_map = (lambda i, c: (i, 0, 0, 0)) if s0.shape[0] == b else (lambda i, c: (0, 0, 0, 0))
    blk = lambda w: pl.BlockSpec((1, cb, w), lambda i, c: (i, c, 0))
    st_block = (1, RET_HEADS, RET_DK, RET_DV)
    return pl.pallas_call(
        functools.partial(_ret_kernel, cb=cb, n_valid=n_valid),
        grid=(b, s // cb),
        in_specs=[blk(q.shape[2]), blk(k.shape[2]), blk(v.shape[2]), blk(g.shape[2]),
                  pl.BlockSpec(st_block, s0_map)],
        out_specs=[blk(v.shape[2]), pl.BlockSpec(st_block, lambda i, c: (i, 0, 0, 0))],
        out_shape=[jax.ShapeDtypeStruct(v.shape, BF16),
                   jax.ShapeDtypeStruct((b,) + st_block[1:], F32)],
        scratch_shapes=[pltpu.VMEM((RET_HEADS, cb, cb), F32)],
        compiler_params=_cparams(("arbitrary", "arbitrary")),
        name="retention",
    )(q, k, v, g, s0)


def _scores(qq, kk):
    return lax.dot_general(qq, kk, (((1,), (1,)), ((), ())), preferred_element_type=F32)


def _softmax_seed(scores, values):
    ms = [jnp.max(s, axis=-1, keepdims=True) for s in scores]
    ps = [jnp.exp2(s - m) for s, m in zip(scores, ms)]
    out = []
    for m, p, vv in zip(ms, ps, values):
        out += [m, jnp.sum(p, axis=-1, keepdims=True),
                jnp.dot(p.astype(BF16), vv, preferred_element_type=F32)]
    return out


def _softmax_step(carry, scores, values):
    m2s = [jnp.maximum(carry[3 * c], jnp.max(s, axis=-1, keepdims=True)) for c, s in enumerate(scores)]
    ps = [jnp.exp2(s - m2) for s, m2 in zip(scores, m2s)]
    out = []
    for c, (m2, p, vv) in enumerate(zip(m2s, ps, values)):
        m, l, acc = carry[3 * c:3 * c + 3]
        a = jnp.exp2(m - m2)
        out += [m2, a * l + jnp.sum(p, axis=-1, keepdims=True),
                a * acc + jnp.dot(p.astype(BF16), vv, preferred_element_type=F32)]
    return out


def _sub_norm(o1, o2, lam, gs, coef):
    o = o1 - lam * o2
    return o * lax.rsqrt(jnp.mean(o * o, axis=-1, keepdims=True) + EPS) * gs * coef


SUM_ROWS = 16
VT_ROWS = HEAD_W + SUM_ROWS


def _softmax_seed_t(scores_t, values_t):
    ms = [jnp.max(s, axis=0, keepdims=True) for s in scores_t]
    ps = [jnp.exp2(s - m) for s, m in zip(scores_t, ms)]
    out = []
    for m, p, vt in zip(ms, ps, values_t):
        out += [m, jnp.dot(vt, p.astype(BF16), preferred_element_type=F32)]
    return out


def _softmax_step_t(carry, scores_t, values_t):
    m2s = [jnp.maximum(carry[2 * c], jnp.max(s, axis=0, keepdims=True)) for c, s in enumerate(scores_t)]
    ps = [jnp.exp2(s - m2) for s, m2 in zip(scores_t, m2s)]
    out = []
    for c, (m2, p, vt) in enumerate(zip(m2s, ps, values_t)):
        m, acc = carry[2 * c:2 * c + 2]
        out += [m2, jnp.exp2(m - m2) * acc + jnp.dot(vt, p.astype(BF16), preferred_element_type=F32)]
    return out


def _attn_prompt_kernel(lam_ref, q_ref, km_ref, vm_ref, ke_ref, ve_ref, gst_ref, o_ref, kb_ref, vt_ref,
                        *, sq, tq, hp, n_ext, coef):
    heads = [slice(hh * HEAD_W, (hh + 1) * HEAD_W) for hh in range(hp)]
    vrows = [slice(hh * VT_ROWS, (hh + 1) * VT_ROWS) for hh in range(hp)]
    kb_ref[...] = km_ref[0].astype(BF16)
    ones = jnp.ones((SUM_ROWS, tq), BF16)
    for j in range(sq // tq):
        for hh, sl in enumerate(heads):
            vt_ref[j, hh * VT_ROWS:hh * VT_ROWS + HEAD_W, :] = vm_ref[0, j * tq:(j + 1) * tq, sl].T.astype(BF16)
            vt_ref[j, hh * VT_ROWS + HEAD_W:(hh + 1) * VT_ROWS, :] = ones
    ke = [ke_ref[0, :, sl].astype(BF16) for sl in heads]
    vet = [jnp.concatenate([ve_ref[0, :, sl].T.astype(BF16), jnp.ones((SUM_ROWS, ve_ref.shape[1]), BF16)], axis=0)
           for sl in heads]
    lam = lam_ref[0]
    lo = lax.broadcasted_iota(jnp.int32, (1, HEAD_W), 1) < DIFF_DH
    ext_ok = lax.broadcasted_iota(jnp.int32, (ke_ref.shape[1], 1), 0) < n_ext
    diag = (lax.broadcasted_iota(jnp.int32, (tq, tq), 0) // CHUNK
            <= lax.broadcasted_iota(jnp.int32, (tq, tq), 1) // CHUNK)

    def q_tile(qi):
        q0 = qi * tq
        qs = []
        for sl in heads:
            q = q_ref[0, pl.ds(q0, tq), sl]
            zero = jnp.zeros_like(q)
            qs += [jnp.where(lo, q, zero), jnp.where(lo, zero, q)]

        ext_s = [jnp.where(ext_ok, _scores(ke[c // 2], qq), NEG) for c, qq in enumerate(qs)]
        carry = _softmax_seed_t(ext_s, [vet[c // 2] for c in range(len(qs))])

        def step(j, carry, mask=None):
            ss = [_scores(kb_ref[j * tq:(j + 1) * tq, heads[c // 2]], qq) for c, qq in enumerate(qs)]
            if mask is not None:
                ss = [jnp.where(mask, s, NEG) for s in ss]
            return tuple(_softmax_step_t(carry, ss, [vt_ref[j, vrows[c // 2], :] for c in range(len(qs))]))

        carry = tuple(carry)
        for j in range(qi):
            carry = step(j, carry)
        carry = step(qi, carry, diag)
        for hh, sl in enumerate(heads):
            _, a1, _, a2 = carry[4 * hh:4 * hh + 4]
            o = a1[:HEAD_W] / a1[HEAD_W:HEAD_W + 1] - lam * (a2[:HEAD_W] / a2[HEAD_W:HEAD_W + 1])
            on = o * lax.rsqrt(jnp.mean(o * o, axis=0, keepdims=True) + EPS) * gst_ref[...] * coef
            o_ref[0, q0:q0 + tq, sl] = on.T.astype(BF16)

    for qi in range(sq // tq):
        q_tile(qi)


def _attention_prompt(q, k_main, v_main, k_ext, v_ext, n_ext, lam, g_subln, coef, tq, hp):
    b, sq, hw = q.shape
    w = hp * HEAD_W
    bh = pl.BlockSpec((1, sq, w), lambda i, h: (i, 0, h))
    ext = pl.BlockSpec((1, k_ext.shape[1], w), lambda i, h: (0, 0, h))
    gst = jnp.broadcast_to(g_subln.reshape(HEAD_W, 1), (HEAD_W, tq))
    return pl.pallas_call(
        functools.partial(_attn_prompt_kernel, sq=sq, tq=tq, hp=hp, n_ext=n_ext, coef=coef),
        grid=(b, hw // w),
        in_specs=[pl.BlockSpec(memory_space=pltpu.SMEM), bh, bh, bh, ext, ext,
                  pl.BlockSpec((HEAD_W, tq), lambda i, h: (0, 0))],
        out_specs=bh,
        out_shape=jax.ShapeDtypeStruct(q.shape, BF16),
        scratch_shapes=[pltpu.VMEM((sq, w), BF16), pltpu.VMEM((sq // tq, hp * VT_ROWS, tq), BF16)],
        compiler_params=_cparams(("arbitrary", "arbitrary")),
        name="attention_prompt",
    )(lam, q, k_main, v_main, k_ext, v_ext, gst)


def _attn_sample_kernel(lam_ref, q_ref, kc_ref, vc_ref, ke_ref, ve_ref, gs_ref, o_ref, m_ref, l_ref, acc_ref,
                        *, ds, nh, n_ext, coef):
    t = pl.program_id(1)
    tkv = kc_ref.shape[1] // nh
    lo = lax.broadcasted_iota(jnp.int32, (1, HEAD_W), 1) < DIFF_DH
    ext_ok = lax.broadcasted_iota(jnp.int32, (1, ke_ref.shape[1]), 1) < n_ext
    heads = [slice(hh * HEAD_W, (hh + 1) * HEAD_W) for hh in range(nh)]

    def stacked_q(sl):
        q = q_ref[0, :, sl]
        zero = jnp.zeros_like(q)
        return jnp.concatenate([jnp.where(lo, q, zero), jnp.where(lo, zero, q)], axis=0)

    qs = [stacked_q(sl) for sl in heads]

    def write_stats(stats):
        for hh in range(nh):
            m_ref[hh], l_ref[hh], acc_ref[hh] = stats[3 * hh:3 * hh + 3]

    @pl.when(t == 0)
    def _():
        ext_s = [jnp.where(ext_ok, _scores(qq, ke_ref[0, :, sl].astype(BF16)), NEG) for qq, sl in zip(qs, heads)]
        write_stats(_softmax_seed(ext_s, [ve_ref[0, :, sl].astype(BF16) for sl in heads]))

    carry = []
    for hh in range(nh):
        carry += [m_ref[hh], l_ref[hh], acc_ref[hh]]
    ss = [_scores(qq, kc_ref[0, pl.ds(hh, tkv, stride=nh), :].astype(BF16)) for hh, qq in enumerate(qs)]
    write_stats(_softmax_step(carry, ss, [vc_ref[0, pl.ds(hh, tkv, stride=nh), :].astype(BF16) for hh in range(nh)]))

    @pl.when(t == pl.num_programs(1) - 1)
    def _():
        lam = lam_ref[0]
        for hh, sl in enumerate(heads):
            o = acc_ref[hh] / l_ref[hh]
            o_ref[0, :, sl] = _sub_norm(o[:ds], o[ds:], lam, gs_ref[...], coef).astype(BF16)


def _attention_sample(q, k_cache, v_cache, k_ext, v_ext, n_ext, lam, g_subln, coef, tkv):
    b, ds, hw = q.shape
    _, pl_len, nh, _ = k_cache.shape
    per_b = lambda rows: pl.BlockSpec((1, rows, hw), lambda i, t: (i, 0, 0))
    cache = pl.BlockSpec((1, tkv * nh, HEAD_W), lambda i, t: (i, t, 0))
    return pl.pallas_call(
        functools.partial(_attn_sample_kernel, ds=ds, nh=nh, n_ext=n_ext, coef=coef),
        grid=(b, pl_len // tkv),
        in_specs=[pl.BlockSpec(memory_space=pltpu.SMEM), per_b(ds), cache, cache,
                  per_b(k_ext.shape[1]), per_b(k_ext.shape[1]),
                  pl.BlockSpec((1, HEAD_W), lambda i, t: (0, 0))],
        out_specs=per_b(ds),
        out_shape=jax.ShapeDtypeStruct(q.shape, BF16),
        scratch_shapes=[pltpu.VMEM((nh, 2 * ds, 1), F32), pltpu.VMEM((nh, 2 * ds, 1), F32),
                        pltpu.VMEM((nh, 2 * ds, HEAD_W), F32)],
        compiler_params=_cparams(("arbitrary", "arbitrary")),
        name="attention_sample",
    )(lam, q, k_cache.reshape(b, pl_len * nh, HEAD_W), v_cache.reshape(b, pl_len * nh, HEAD_W),
      k_ext, v_ext, g_subln)


def _mix_kernel(x_ref, yr_ref, od_ref, ga_ref, gb_ref, wr_ref, wd_ref, wo_ref, gf_ref, wrt_ref, br_ref,
                x1_ref, h_ref, lgt_ref):
    ya = jnp.dot(yr_ref[...], wr_ref[...], preferred_element_type=F32)
    yb = jnp.dot(od_ref[...], wd_ref[...], preferred_element_type=F32)
    z = jax.nn.sigmoid(ga_ref[...].astype(F32)) * ya + jax.nn.sigmoid(gb_ref[...].astype(F32)) * yb
    x1 = x_ref[...] + jnp.dot(z.astype(BF16), wo_ref[...], preferred_element_type=F32)
    x1_ref[...] = x1
    h = x1 * lax.rsqrt(jnp.mean(x1 * x1, axis=-1, keepdims=True) + EPS) * gf_ref[...]
    _tile_rows_store(h_ref, h)
    lgt_ref[...] = _scores(wrt_ref[...], h.astype(BF16)) + br_ref[...]


def _mix_tail_kernel(*refs):
    th_ref, tlgt_ref = refs[11:13]
    h_ref, lgt_ref = refs[14:16]
    last = pl.num_programs(0) - 1

    @pl.when(pl.program_id(0) < last)
    def _():
        _mix_kernel(*refs[:11], *refs[13:])

    @pl.when(pl.program_id(0) == last)
    def _():
        h_ref[...] = th_ref[...]
        lgt_ref[...] = tlgt_ref[...]


def _mix_ret_kernel(x_ref, rq_ref, rk_ref, rv_ref, rg_ref, s0_ref, od_ref, ga_ref, gb_ref,
                    wr_ref, wd_ref, wo_ref, gf_ref, wrt_ref, br_ref, th_ref, tlgt_ref,
                    x1_ref, h_ref, lgt_ref, st_ref, decay_ref, yr_ref, *, cb, tiles_per_seq):
    i = pl.program_id(0)
    last = pl.num_programs(0) - 1

    @pl.when(i == 0)
    def _():
        _fill_decay(decay_ref, cb)

    @pl.when(i < last)
    def _():
        @pl.when(i % tiles_per_seq == 0)
        def _():
            st_ref[...] = s0_ref[...]

        refs = {"q": rq_ref, "k": rk_ref, "v": rv_ref, "g": rg_ref}
        for sub in range(yr_ref.shape[0] // cb):
            rows = slice(sub * cb, (sub + 1) * cb)

            def store(cols, y, rows=rows):
                yr_ref[rows, cols] = y

            _ret_block(lambda name, cols, rows=rows: refs[name][rows, cols], store, st_ref, decay_ref, cb, cb)
        _mix_kernel(x_ref, yr_ref, od_ref, ga_ref, gb_ref, wr_ref, wd_ref, wo_ref, gf_ref, wrt_ref, br_ref,
                    x1_ref, h_ref, lgt_ref)

    @pl.when(i == last)
    def _():
        h_ref[...] = th_ref[...]
        lgt_ref[...] = tlgt_ref[...]


def _mix_ret(x, rq, rk, rv, rg, s0, od, ga, gb, wr, wd, wo, gf, wrt, br, tm, tail, seq, cb):
    t, d = x.shape
    ne = wrt.shape[0]
    nt = t // tm
    tiles_per_seq = seq // tm
    row = lambda w: pl.BlockSpec((tm, w), lambda i: (jnp.minimum(i, nt - 1), 0))
    full = lambda a: pl.BlockSpec(a.shape, lambda i: (0,) * a.ndim)
    st_block = (1, RET_HEADS, RET_DK, RET_DV)
    weights = (wr, wd, wo, gf, wrt, br) + tuple(tail)
    return pl.pallas_call(
        functools.partial(_mix_ret_kernel, cb=cb, tiles_per_seq=tiles_per_seq),
        grid=(nt + 1,),
        in_specs=[row(d), row(rq.shape[1]), row(rk.shape[1]), row(rv.shape[1]), row(rg.shape[1]), full(s0),
                  row(od.shape[1]), row(d), row(d)] + [full(a) for a in weights],
        out_specs=[row(d), pl.BlockSpec((tm * SUBLANES, LANES), lambda i: (i, 0)),
                   pl.BlockSpec((ne, tm), lambda i: (0, i)),
                   pl.BlockSpec(st_block, lambda i: (jnp.minimum(i, nt - 1) // tiles_per_seq, 0, 0, 0))],
        out_shape=[jax.ShapeDtypeStruct((t, d), F32),
                   jax.ShapeDtypeStruct(((t + tm) * SUBLANES, LANES), F32),
                   jax.ShapeDtypeStruct((ne, t + tm), F32),
                   jax.ShapeDtypeStruct((t // seq,) + st_block[1:], F32)],
        scratch_shapes=[pltpu.VMEM((RET_HEADS, cb, cb), F32), pltpu.VMEM((tm, rv.shape[1]), BF16)],
        compiler_params=_cparams(("arbitrary",)),
        name="mix_retention",
    )(x, rq, rk, rv, rg, s0, od, ga, gb, *weights)


def _mix(x, yr, od, ga, gb, wr, wd, wo, gf, wrt, br, tm, tail=None):
    t, d = x.shape
    ne = wrt.shape[0]
    nt = t // tm
    extra = 0 if tail is None else 1
    row = lambda w: pl.BlockSpec((tm, w), lambda i: (jnp.minimum(i, nt - 1), 0))
    full = lambda a: pl.BlockSpec(a.shape, lambda i: (0, 0))
    ops = (x, yr, od, ga, gb, wr, wd, wo, gf, wrt, br) + (() if tail is None else tuple(tail))
    return pl.pallas_call(
        _mix_kernel if tail is None else _mix_tail_kernel,
        grid=(nt + extra,),
        in_specs=[row(d), row(yr.shape[1]), row(od.shape[1]), row(d), row(d)] + [full(a) for a in ops[5:]],
        out_specs=[row(d), pl.BlockSpec((tm * SUBLANES, LANES), lambda i: (i, 0)),
                   pl.BlockSpec((ne, tm), lambda i: (0, i))],
        out_shape=[jax.ShapeDtypeStruct((t, d), F32),
                   jax.ShapeDtypeStruct(((t + extra * tm) * SUBLANES, LANES), F32),
                   jax.ShapeDtypeStruct((ne, t + extra * tm), F32)],
        compiler_params=_cparams(("arbitrary",)),
        name="mix",
    )(*ops)


def _route_kernel(lgt_ref, e_ref, gate_ref, rank_ref, cnt_ref):
    @pl.when(pl.program_id(0) == 0)
    def _():
        cnt_ref[...] = jnp.zeros_like(cnt_ref)

    v = lgt_ref[...]
    ne, tn = v.shape
    eid = lax.broadcasted_iota(jnp.int32, (ne, tn), 0)
    tops, sels = [], []
    for k in range(TOP_K):
        m = jnp.max(v, axis=0, keepdims=True)
        idx = jnp.min(jnp.where(v == m, eid, ne), axis=0, keepdims=True)
        sel = eid == idx
        e_ref[k:k + 1, :] = idx
        tops.append(m)
        sels.append(sel)
        v = jnp.where(sel, -jnp.inf, v)
    ex = [jnp.exp(m - tops[0]) for m in tops]
    den = ex[0] + ex[1] + ex[2] + ex[3]
    for k in range(TOP_K):
        gate_ref[k:k + 1, :] = ex[k] / den

    chosen = sels[0] | sels[1] | sels[2] | sels[3]
    before = (lax.broadcasted_iota(jnp.int32, (tn, tn), 0) < lax.broadcasted_iota(jnp.int32, (tn, tn), 1))
    prior = jnp.dot(chosen.astype(BF16), before.astype(BF16), preferred_element_type=F32) + cnt_ref[:, 0:1]
    for k in range(TOP_K):
        rank_ref[k:k + 1, :] = jnp.sum(jnp.where(sels[k], prior, 0.0), axis=0, keepdims=True).astype(jnp.int32)
    cnt_ref[...] = cnt_ref[...] + jnp.sum(chosen.astype(F32), axis=1, keepdims=True)


def _route_topk(lgt, tn):
    ne, t = lgt.shape
    kt = pl.BlockSpec((TOP_K, tn), lambda i: (0, i))
    return pl.pallas_call(
        _route_kernel,
        grid=(t // tn,),
        in_specs=[pl.BlockSpec((ne, tn), lambda i: (0, i))],
        out_specs=[kt, kt, kt, pl.BlockSpec((ne, LANES), lambda i: (0, 0))],
        out_shape=[jax.ShapeDtypeStruct((TOP_K, t), jnp.int32), jax.ShapeDtypeStruct((TOP_K, t), F32),
                   jax.ShapeDtypeStruct((TOP_K, t), jnp.int32), jax.ShapeDtypeStruct((ne, LANES), F32)],
        compiler_params=_cparams(("arbitrary",)),
        name="route",
    )(lgt)


GATHER_UNROLL = 8
MOE_GATHER_PRIORITIES = (1,)


def _gather_rows(src_hbm, idx_ref, n, buf_ref, slot, sem, priorities=(0, 1)):
    def body(g, _):
        for u in range(GATHER_UNROLL):
            r = g * GATHER_UNROLL + u
            src = pl.multiple_of(idx_ref[0, 0, r], SUBLANES)
            dst = pl.multiple_of(r * SUBLANES, SUBLANES)
            pltpu.make_async_copy(src_hbm.at[pl.ds(src, SUBLANES)], buf_ref.at[slot, pl.ds(dst, SUBLANES)],
                                  sem.at[slot]).start(priority=priorities[u % len(priorities)])
        return 0
    lax.fori_loop(0, n // GATHER_UNROLL, body, 0)


def _wait_rows(src_hbm, n, buf_ref, slot, sem):
    pltpu.make_async_copy(src_hbm.at[pl.ds(0, n * SUBLANES)], buf_ref.at[slot], sem.at[slot]).wait()


MOE_SLOTS = 3


def _moe_kernel(be_ref, nu_ref, tok_ref, tok1_ref, tok2_ref, h_hbm, wgu_ref, bgu_ref, wdn_ref, bdn_ref, out_ref,
                buf_ref, wgu_bf, wdn_bf, sem):
    i = pl.program_id(0)
    n_used = nu_ref[0]
    slot = i % MOE_SLOTS
    d_ff = wdn_bf.shape[0]

    @pl.when(i == 0)
    def _():
        _gather_rows(h_hbm, tok_ref, MOE_ROWS, buf_ref, 0, sem, MOE_GATHER_PRIORITIES)

        @pl.when(1 < n_used)
        def _():
            _gather_rows(h_hbm, tok1_ref, MOE_ROWS, buf_ref, 1, sem, MOE_GATHER_PRIORITIES)

    @pl.when(i + 2 < n_used)
    def _():
        _gather_rows(h_hbm, tok2_ref, MOE_ROWS, buf_ref, (i + 2) % MOE_SLOTS, sem, MOE_GATHER_PRIORITIES)

    @pl.when(i < n_used)
    def _():
        @pl.when((i == 0) | (be_ref[i] != be_ref[jnp.maximum(i - 1, 0)]))
        def _():
            wgu_bf[...] = wgu_ref[0].astype(BF16)
            wdn_bf[...] = wdn_ref[0].astype(BF16)

        _wait_rows(h_hbm, MOE_ROWS, buf_ref, slot, sem)

        xb = _tile_rows_load(buf_ref, slot, 0, MOE_ROWS).astype(BF16)
        gu = jnp.dot(xb, wgu_bf[...], preferred_element_type=F32) + bgu_ref[0]
        gate = jnp.minimum(gu[:, :d_ff], SWIGLU_LIMIT)
        up = jnp.clip(gu[:, d_ff:], -SWIGLU_LIMIT, SWIGLU_LIMIT)
        act = (up + 1.0) * gate * jax.nn.sigmoid(SWIGLU_ALPHA * gate)
        _tile_rows_store(out_ref, jnp.dot(act.astype(BF16), wdn_bf[...], preferred_element_type=F32) + bdn_ref[0])

    @pl.when(i >= n_used)
    def _():
        out_ref[...] = jnp.zeros_like(out_ref)


def _moe(h, tok, block_e, n_used, w_gu, b_gu, w_down, b_down):
    d = h.shape[1] * SUBLANES
    nb = tok.shape[0]
    ne, _, two_ff = w_gu.shape
    d_ff = two_ff // 2
    e_map = lambda i, be, nu: (be[i], 0, 0)
    grid_spec = pltpu.PrefetchScalarGridSpec(
        num_scalar_prefetch=2,
        grid=(nb,),
        in_specs=[
            pl.BlockSpec((1, 1, MOE_ROWS), lambda i, be, nu: (i, 0, 0), memory_space=pltpu.SMEM),
            pl.BlockSpec((1, 1, MOE_ROWS), lambda i, be, nu: (jnp.minimum(i + 1, nb - 1), 0, 0),
                         memory_space=pltpu.SMEM),
            pl.BlockSpec((1, 1, MOE_ROWS), lambda i, be, nu: (jnp.minimum(i + 2, nb - 1), 0, 0),
                         memory_space=pltpu.SMEM),
            pl.BlockSpec(memory_space=pl.ANY),
            pl.BlockSpec((1, d, two_ff), e_map),
            pl.BlockSpec((1, 1, two_ff), e_map),
            pl.BlockSpec((1, d_ff, d), e_map),
            pl.BlockSpec((1, 1, d), e_map),
        ],
        out_specs=pl.BlockSpec((MOE_ROWS * SUBLANES, LANES), lambda i, be, nu: (i, 0)),
        scratch_shapes=[pltpu.VMEM((MOE_SLOTS, MOE_ROWS * SUBLANES, LANES), F32),
                        pltpu.VMEM((d, two_ff), BF16),
                        pltpu.VMEM((d_ff, d), BF16),
                        pltpu.SemaphoreType.DMA((MOE_SLOTS,))],
    )
    return pl.pallas_call(
        _moe_kernel,
        grid_spec=grid_spec,
        out_shape=jax.ShapeDtypeStruct((nb * MOE_ROWS * SUBLANES, LANES), F32),
        compiler_params=_cparams(("arbitrary",)),
        name="moe",
    )(block_e, n_used, tok, tok, tok, h, w_gu, b_gu.reshape(ne, 1, two_ff), w_down, b_down.reshape(ne, 1, d))


def _combine_kernel(dst_ref, dstn_ref, x_ref, gate_ref, gf_ref, rows_hbm, y_ref, buf_ref, sem, *, tm, nsteps):
    i = pl.program_id(0)
    slot = i % 2
    n = tm * TOP_K

    @pl.when(i == 0)
    def _():
        _gather_rows(rows_hbm, dst_ref, n, buf_ref, 0, sem)

    @pl.when(i + 1 < nsteps)
    def _():
        _gather_rows(rows_hbm, dstn_ref, n, buf_ref, 1 - slot, sem)

    _wait_rows(rows_hbm, n, buf_ref, slot, sem)

    y = x_ref[...]
    gate = gate_ref[...]
    for k in range(TOP_K):
        y = y + gate[:, k:k + 1] * _tile_rows_load(buf_ref, slot, k * tm, tm)
    y_ref[...] = y * lax.rsqrt(jnp.mean(y * y, axis=-1, keepdims=True) + EPS) * gf_ref[...]


def _combine(x1, gates, dest, rows, g_final, tm):
    t, d = x1.shape
    nsteps = t // tm
    n = tm * TOP_K
    return pl.pallas_call(
        functools.partial(_combine_kernel, tm=tm, nsteps=nsteps),
        grid=(nsteps,),
        in_specs=[pl.BlockSpec((1, 1, n), lambda i: (i, 0, 0), memory_space=pltpu.SMEM),
                  pl.BlockSpec((1, 1, n), lambda i: (jnp.minimum(i + 1, nsteps - 1), 0, 0),
                               memory_space=pltpu.SMEM),
                  pl.BlockSpec((tm, d), lambda i: (i, 0)),
                  pl.BlockSpec((tm, TOP_K), lambda i: (i, 0)),
                  pl.BlockSpec((1, d), lambda i: (0, 0)),
                  pl.BlockSpec(memory_space=pl.ANY)],
        out_specs=pl.BlockSpec((tm, d), lambda i: (i, 0)),
        out_shape=jax.ShapeDtypeStruct((t, d), F32),
        scratch_shapes=[pltpu.VMEM((2, n * SUBLANES, LANES), F32), pltpu.SemaphoreType.DMA((2,))],
        compiler_params=_cparams(("arbitrary",)),
        name="combine",
    )(dest, dest, x1, gates, g_final, rows)


def _route(lgt, tn):
    t = lgt.shape[1]
    n_assign = t * TOP_K
    e4, gates, rank, cnt = _route_topk(lgt, tn)
    counts = cnt[:, 0].astype(jnp.int32)
    padded = (counts + MOE_ROWS - 1) // MOE_ROWS * MOE_ROWS
    pend = jnp.cumsum(padded)
    pstart = pend - padded
    start = jnp.cumsum(counts) - counts
    e_ids = jnp.arange(N_EXPERTS, dtype=jnp.int32)
    dest = rank + jnp.sum(jnp.where(e4[None] == e_ids[:, None, None], pstart[:, None, None], 0), axis=0)
    nb = n_assign // MOE_ROWS + N_EXPERTS
    n_used = (pend[-1] // MOE_ROWS).astype(jnp.int32).reshape(1)
    blk0 = jnp.arange(nb, dtype=jnp.int32) * MOE_ROWS
    block_e = jnp.minimum(jnp.sum((pend[None, :] <= blk0[:, None]).astype(jnp.int32), axis=1), N_EXPERTS - 1)
    of_block = lambda table: jnp.sum(jnp.where(block_e[:, None] == e_ids[None, :], table[None, :], 0), axis=1)
    bits = max(n_assign - 1, 1).bit_length()
    assert N_EXPERTS << bits < 2 ** 31
    keys = (e4.T.reshape(-1) << bits) + jnp.arange(n_assign, dtype=jnp.int32)
    order = jnp.sort(keys) & ((1 << bits) - 1)
    within = (blk0 - of_block(pstart))[:, None] + jnp.arange(MOE_ROWS, dtype=jnp.int32)[None, :]
    src = jnp.clip(of_block(start)[:, None] + jnp.minimum(within, of_block(counts)[:, None] - 1), 0, n_assign - 1)
    tok = (order[src] // TOP_K).astype(jnp.int32)
    return gates.T, dest * SUBLANES, (tok * SUBLANES).reshape(nb, 1, MOE_ROWS), block_e, n_used


def kernel(x_prompt, x_sample, cache_k, cache_v, state_ret, meta_tokens, g_mix, w_in,
           lam_q1, lam_k1, lam_q2, lam_k2, g_subln, w_ret_out, w_diff_out, w_out,
           g_ffn, w_router, b_router, w_gu, b_gu, w_down, b_down, g_final):
    b, s, d = x_prompt.shape
    db, ds, _ = x_sample.shape
    depth, _, pl_len = cache_k.shape[:3]
    assert depth == 1, "single-layer step only"
    nm = meta_tokens.shape[0]
    hw = DIFF_HEADS * HEAD_W

    lam_init = 0.8 - 0.6 * math.exp(-0.3 * 0)
    lam = (jnp.exp(jnp.sum(lam_q1[0] * lam_k1[0])) - jnp.exp(jnp.sum(lam_q2[0] * lam_k2[0]))
           + lam_init).reshape(1).astype(F32)
    coef = 1.0 - lam_init

    w_in_bf = w_in[0].astype(BF16)
    g_mix2 = g_mix[0].reshape(1, d)

    tm = 512 if (b * s) % 512 == 0 else s
    tab_p = _tables(N_META + jnp.arange(s, dtype=jnp.int32))
    rq, rk, rv, rg, dq, dk, dv, ga, gb = _proj(x_prompt.reshape(b * s, d), g_mix2, w_in_bf, tab_p, tm, s // tm)

    pos_small = jnp.concatenate([jnp.tile(N_META + pl_len + jnp.arange(ds, dtype=jnp.int32), db),
                                 jnp.arange(nm, dtype=jnp.int32)])
    x_small = jnp.concatenate([x_sample.reshape(db * ds, d), meta_tokens.astype(F32)], axis=0)
    small = _proj(x_small, g_mix2, w_in_bf, _tables(pos_small), x_small.shape[0], 1)
    ns = db * ds
    srq, srk, srv, srg, sdq, sdk, sdv, sga, sgb = [a[:ns] for a in small]
    mrq, mrk, mrv, mrg, _, mdk, mdv, _, _ = [a[ns:] for a in small]

    cb_small = 128

    def pad_rows(a, nb_, n_):
        a = a.reshape(nb_, n_, a.shape[-1])
        return jnp.pad(a, ((0, 0), (0, cb_small - n_), (0, 0)))

    zero_state = jnp.zeros((1, RET_HEADS, RET_DK, RET_DV), F32)
    _, s_meta = _retention(pad_rows(mrq, 1, nm), pad_rows(mrk, 1, nm), pad_rows(mrv, 1, nm),
                           pad_rows(mrg, 1, nm), zero_state, cb_small, nm)
    cb = 256
    r3 = lambda a: a.reshape(b, s, a.shape[-1])
    syr, ret_s = _retention(pad_rows(srq, db, ds), pad_rows(srk, db, ds), pad_rows(srv, db, ds),
                            pad_rows(srg, db, ds), state_ret[0], cb_small, ds)
    syr = syr[:, :ds].reshape(ns, -1)

    g_sub = g_subln[0].reshape(1, HEAD_W)
    ext_rows = 128
    pad_ext = lambda a: jnp.pad(a, ((0, 0), (0, ext_rows - a.shape[1]), (0, 0)))
    od = _attention_prompt(r3(dq), r3(dk), r3(dv), pad_ext(mdk[None]), pad_ext(mdv[None]), nm,
                           lam, g_sub, coef, 256, 4)
    sdk3, sdv3 = sdk.reshape(db, ds, hw), sdv.reshape(db, ds, hw)
    ke_s = jnp.concatenate([jnp.broadcast_to(mdk[None], (db, nm, hw)), sdk3], axis=1)
    ve_s = jnp.concatenate([jnp.broadcast_to(mdv[None], (db, nm, hw)), sdv3], axis=1)
    sod = _attention_sample(sdq.reshape(db, ds, hw), cache_k[0], cache_v[0], pad_ext(ke_s), pad_ext(ve_s),
                            nm + ds, lam, g_sub, coef, min(2048, pl_len))

    wr, wd, wo = w_ret_out[0].astype(BF16), w_diff_out[0].astype(BF16), w_out[0].astype(BF16)
    gf = g_ffn[0].reshape(1, d)
    wrt = w_router[0].T.astype(BF16)
    br = b_router[0].astype(F32).reshape(N_EXPERTS, 1)
    sx1, sh, slgt = _mix(x_sample.reshape(ns, d), syr, sod.reshape(ns, hw), sga, sgb, wr, wd, wo, gf, wrt, br, ns)
    x1, h, lgt, ret_p = _mix_ret(x_prompt.reshape(b * s, d), rq, rk, rv, rg, s_meta, od.reshape(b * s, hw), ga, gb,
                                 wr, wd, wo, gf, wrt, br, ns, (sh, slgt), s, min(cb, ns))

    t = b * s
    gates, dest, tok, block_e, n_used = _route(lgt, ns)
    rows = _moe(h, tok, block_e, n_used, w_gu[0], b_gu[0], w_down[0], b_down[0])
    gfin = g_final.reshape(1, d)

    def combine(x1g, lo, n):
        tc = min(512, n)
        dg = dest[:, lo:lo + n].reshape(TOP_K, n // tc, tc).swapaxes(0, 1).reshape(n // tc, 1, TOP_K * tc)
        return _combine(x1g, gates[lo:lo + n], dg, rows, gfin, tc)

    y_p = combine(x1, 0, t)
    y_s = combine(sx1, t, ns)

    k_p = jnp.concatenate([jnp.broadcast_to(mdk[None], (b, nm, hw)), r3(dk)], axis=1)
    v_p = jnp.concatenate([jnp.broadcast_to(mdv[None], (b, nm, hw)), r3(dv)], axis=1)
    shp = lambda a: a.reshape(1, a.shape[0], a.shape[1], DIFF_HEADS, HEAD_W)
    return (y_p.reshape(b, s, d), y_s.reshape(db, ds, d), ret_p[None], shp(k_p), shp(v_p),
            ret_s[None], shp(sdk3), shp(sdv3))
```

```python
import functools
import math

import jax
import jax.numpy as jnp
from jax import lax
from jax.experimental import pallas as pl
from jax.experimental.pallas import tpu as pltpu

F32 = jnp.float32
BF16 = jnp.bfloat16

EPS = 1e-6
N_META = 16
CHUNK = 64
RET_HEADS = 4
RET_DK = 128
RET_DV = 256
RET_THETA = 10000.0
DIFF_HEADS = 8
DIFF_DH = 64
ROT_DIM = DIFF_DH // 4
ROPE_THETA = 500000.0
N_EXPERTS = 32
TOP_K = 4
SWIGLU_LIMIT = 7.0
SWIGLU_ALPHA = 1.702

LANES = 128
SUBLANES = 8
HEAD_W = 2 * DIFF_DH
MOE_ROWS = 512
VMEM_LIMIT = 56 * 1024 * 1024
NEG = -1e30
Q_SCALE = DIFF_DH ** -0.5 * math.log2(math.e)


def _cparams(sem):
    return pltpu.CompilerParams(dimension_semantics=sem, vmem_limit_bytes=VMEM_LIMIT)


def _tile_rows_store(ref, val):
    n = val.shape[0]
    for c in range(SUBLANES):
        ref[pl.ds(c, n, stride=SUBLANES), :] = val[:, c * LANES:(c + 1) * LANES]


def _tile_rows_load(ref, slot, first, n):
    return jnp.concatenate([ref[slot, pl.ds(first * SUBLANES + c, n, stride=SUBLANES), :]
                            for c in range(SUBLANES)], axis=1)


def _ret_tables(pos):
    angle = RET_THETA ** (-jnp.linspace(0.0, 1.0, RET_DK // 2, dtype=F32))
    angle = jnp.repeat(angle, 2)
    ang = pos.astype(F32)[:, None] * angle[None, :]
    cos, sin = jnp.cos(ang), jnp.sin(ang)
    even = (jnp.arange(RET_DK) % 2 == 0)[None, :]
    return cos, jnp.where(even, -sin, 0.0), jnp.where(even, 0.0, sin)


def _rope_tables(pos):
    half = ROT_DIM // 2
    inv = ROPE_THETA ** (-jnp.arange(half, dtype=F32) * (2.0 / ROT_DIM))
    ang = pos.astype(F32)[:, None] * inv[None, :]
    cos, sin = jnp.cos(ang), jnp.sin(ang)
    jj = jnp.arange(HEAD_W) % DIFF_DH
    first = (jj < half)[None, :]
    second = ((jj >= half) & (jj < ROT_DIM))[None, :]
    cos_l = jnp.take(cos, jj % half, axis=1)
    sin_l = jnp.take(sin, jj % half, axis=1)
    c = jnp.where(first | second, cos_l, 1.0)
    sa = jnp.where(first, -sin_l, 0.0)
    sb = jnp.where(second, sin_l, 0.0)
    return c, sa, sb


def _tables(pos):
    return jnp.concatenate(_ret_tables(pos) + _rope_tables(pos), axis=1)


def _proj_kernel(x_ref, g_ref, w_ref, tab_ref, rq_ref, rk_ref, rv_ref, rg_ref,
                 dq_ref, dk_ref, dv_ref, ga_ref, gb_ref):
    x = x_ref[...]
    ms = jnp.mean(x * x, axis=-1, keepdims=True)
    h = (x * lax.rsqrt(ms + EPS) * g_ref[...]).astype(BF16)

    def mm(c0, width):
        return jnp.dot(h, w_ref[:, c0:c0 + width], preferred_element_type=F32)

    def rot(p, t0, near, far):
        c = tab_ref[:, t0:t0 + LANES]
        sa = tab_ref[:, t0 + LANES:t0 + 2 * LANES]
        sb = tab_ref[:, t0 + 2 * LANES:t0 + 3 * LANES]
        return p * c + pltpu.roll(p, LANES - near, 1) * sa + pltpu.roll(p, far, 1) * sb

    kw = RET_HEADS * RET_DK
    vw = RET_HEADS * RET_DV
    dw = DIFF_HEADS * HEAD_W
    c0 = 0
    p = mm(c0, kw)
    for hh in range(RET_HEADS):
        sl = slice(hh * LANES, (hh + 1) * LANES)
        rq_ref[:, sl] = rot(p[:, sl], 0, 1, 1).astype(BF16)
    c0 += kw
    p = mm(c0, kw)
    for hh in range(RET_HEADS):
        sl = slice(hh * LANES, (hh + 1) * LANES)
        rk_ref[:, sl] = (rot(p[:, sl], 0, 1, 1) * (RET_DK ** -0.5)).astype(BF16)
    c0 += kw
    rv_ref[...] = mm(c0, vw).astype(BF16)
    c0 += vw
    rg_ref[...] = mm(c0, vw).astype(BF16)
    c0 += vw
    half = ROT_DIM // 2
    p = mm(c0, dw)
    for hh in range(DIFF_HEADS):
        sl = slice(hh * LANES, (hh + 1) * LANES)
        dq_ref[:, sl] = (rot(p[:, sl], 3 * LANES, half, half) * Q_SCALE).astype(BF16)
    c0 += dw
    p = mm(c0, dw)
    for hh in range(DIFF_HEADS):
        sl = slice(hh * LANES, (hh + 1) * LANES)
        dk_ref[:, sl] = rot(p[:, sl], 3 * LANES, half, half)
    c0 += dw
    dv_ref[...] = mm(c0, dw)
    c0 += dw
    d = x.shape[1]
    ga_ref[...] = mm(c0, d).astype(BF16)
    c0 += d
    gb_ref[...] = mm(c0, d).astype(BF16)


def _proj(x, g, w_bf, tab, tm, tab_blocks):
    t, d = x.shape
    kw, vw, dw = RET_HEADS * RET_DK, RET_HEADS * RET_DV, DIFF_HEADS * HEAD_W
    widths = (kw, kw, vw, vw, dw, dw, dw, d, d)
    dtypes = (BF16, BF16, BF16, BF16, BF16, F32, F32, BF16, BF16)
    row = lambda i: (i, 0)
    const = lambda i: (0, 0)
    return pl.pallas_call(
        _proj_kernel,
        grid=(t // tm,),
        in_specs=[pl.BlockSpec((tm, d), row),
                  pl.BlockSpec((1, d), const),
                  pl.BlockSpec(w_bf.shape, const),
                  pl.BlockSpec((tm, tab.shape[1]), lambda i: (i % tab_blocks, 0))],
        out_specs=[pl.BlockSpec((tm, w), row) for w in widths],
        out_shape=[jax.ShapeDtypeStruct((t, w), dt) for w, dt in zip(widths, dtypes)],
        compiler_params=_cparams(("arbitrary",)),
        name="proj",
    )(x, g, w_bf, tab)


def _ret_log_gammas():
    return tuple(math.log(1.0 - 2.0 ** (-5.0 - hh)) for hh in range(RET_HEADS))


def _fill_decay(decay_ref, cb):
    row = lax.broadcasted_iota(jnp.int32, (cb, cb), 0)
    col = lax.broadcasted_iota(jnp.int32, (cb, cb), 1)
    dist = (row - col).astype(F32)
    for hh, lg in enumerate(_ret_log_gammas()):
        decay_ref[hh] = jnp.where(dist >= 0, jnp.exp(lg * jnp.maximum(dist, 0.0)), 0.0)


def _ret_block(load, store, st_ref, decay_ref, cb, n_valid):
    idx = lax.broadcasted_iota(jnp.int32, (cb, 1), 0).astype(F32)
    for hh, lg in enumerate(_ret_log_gammas()):
        kcols = slice(hh * RET_DK, (hh + 1) * RET_DK)
        vcols = slice(hh * RET_DV, (hh + 1) * RET_DV)
        q, k, v = load("q", kcols), load("k", kcols), load("v", vcols)
        g = load("g", vcols).astype(F32)
        st = st_ref[0, hh]
        s = lax.dot_general(q, k, (((1,), (1,)), ((), ())), preferred_element_type=F32) * decay_ref[hh]
        o = jnp.dot(s.astype(BF16), v, preferred_element_type=F32)
        o = o + jnp.dot(q, st.astype(BF16), preferred_element_type=F32) * jnp.exp(lg * (idx + 1.0))
        kw = (k.astype(F32) * jnp.exp(lg * (n_valid - 1.0 - idx))).astype(BF16)
        st_ref[0, hh] = math.exp(lg * n_valid) * st + lax.dot_general(
            kw, v, (((0,), (0,)), ((), ())), preferred_element_type=F32)
        on = o * lax.rsqrt(jnp.mean(o * o, axis=-1, keepdims=True) + EPS)
        store(vcols, (g * jax.nn.sigmoid(g) * on).astype(BF16))


def _ret_kernel(q_ref, k_ref, v_ref, g_ref, s0_ref, y_ref, st_ref, decay_ref, *, cb, n_valid):
    @pl.when((pl.program_id(0) == 0) & (pl.program_id(1) == 0))
    def _():
        _fill_decay(decay_ref, cb)

    @pl.when(pl.program_id(1) == 0)
    def _():
        st_ref[...] = s0_ref[...]

    refs = {"q": q_ref, "k": k_ref, "v": v_ref, "g": g_ref}

    def store(cols, y):
        y_ref[0, :, cols] = y

    _ret_block(lambda name, cols: refs[name][0, :, cols], store, st_ref, decay_ref, cb, n_valid)


def _retention(q, k, v, g, s0, cb, n_valid):
    b, s, _ = q.shape
    s0_map = (lambda i, c: (i, 0, 0, 0)) if s0.shape[0] == b else (lambda i, c: (0, 0, 0, 0))
    blk = lambda w: pl.BlockSpec((1, cb, w), lambda i, c: (i, c, 0))
    st_block = (1, RET_HEADS, RET_DK, RET_DV)
    return pl.pallas_call(
        functools.partial(_ret_kernel, cb=cb, n_valid=n_valid),
        grid=(b, s // cb),
        in_specs=[blk(q.shape[2]), blk(k.shape[2]), blk(v.shape[2]), blk(g.shape[2]),
                  pl.BlockSpec(st_block, s0_map)],
        out_specs=[blk(v.shape[2]), pl.BlockSpec(st_block, lambda i, c: (i, 0, 0, 0))],
        out_shape=[jax.ShapeDtypeStruct(v.shape, BF16),
                   jax.ShapeDtypeStruct((b,) + st_block[1:], F32)],
        scratch_shapes=[pltpu.VMEM((RET_HEADS, cb, cb), F32)],
        compiler_params=_cparams(("arbitrary", "arbitrary")),
        name="retention",
    )(q, k, v, g, s0)


def _scores(qq, kk):
    return lax.dot_general(qq, kk, (((1,), (1,)), ((), ())), preferred_element_type=F32)


def _softmax_seed(scores, values):
    ms = [jnp.max(s, axis=-1, keepdims=True) for s in scores]
    ps = [jnp.exp2(s - m) for s, m in zip(scores, ms)]
    out = []
    for m, p, vv in zip(ms, ps, values):
        out += [m, jnp.sum(p, axis=-1, keepdims=True),
                jnp.dot(p.astype(BF16), vv, preferred_element_type=F32)]
    return out


def _softmax_step(carry, scores, values):
    m2s = [jnp.maximum(carry[3 * c], jnp.max(s, axis=-1, keepdims=True)) for c, s in enumerate(scores)]
    ps = [jnp.exp2(s - m2) for s, m2 in zip(scores, m2s)]
    out = []
    for c, (m2, p, vv) in enumerate(zip(m2s, ps, values)):
        m, l, acc = carry[3 * c:3 * c + 3]
        a = jnp.exp2(m - m2)
        out += [m2, a * l + jnp.sum(p, axis=-1, keepdims=True),
                a * acc + jnp.dot(p.astype(BF16), vv, preferred_element_type=F32)]
    return out


def _sub_norm(o1, o2, lam, gs, coef):
    o = o1 - lam * o2
    return o * lax.rsqrt(jnp.mean(o * o, axis=-1, keepdims=True) + EPS) * gs * coef


SUM_ROWS = 16
VT_ROWS = HEAD_W + SUM_ROWS


def _softmax_seed_t(scores_t, values_t):
    ms = [jnp.max(s, axis=0, keepdims=True) for s in scores_t]
    ps = [jnp.exp2(s - m) for s, m in zip(scores_t, ms)]
    out = []
    for m, p, vt in zip(ms, ps, values_t):
        out += [m, jnp.dot(vt, p.astype(BF16), preferred_element_type=F32)]
    return out


def _softmax_step_t(carry, scores_t, values_t):
    m2s = [jnp.maximum(carry[2 * c], jnp.max(s, axis=0, keepdims=True)) for c, s in enumerate(scores_t)]
    ps = [jnp.exp2(s - m2) for s, m2 in zip(scores_t, m2s)]
    out = []
    for c, (m2, p, vt) in enumerate(zip(m2s, ps, values_t)):
        m, acc = carry[2 * c:2 * c + 2]
        out += [m2, jnp.exp2(m - m2) * acc + jnp.dot(vt, p.astype(BF16), preferred_element_type=F32)]
    return out


def _attn_prompt_kernel(lam_ref, q_ref, km_ref, vm_ref, ke_ref, ve_ref, gst_ref, o_ref, kb_ref, vt_ref,
                        *, sq, tq, hp, n_ext, coef):
    heads = [slice(hh * HEAD_W, (hh + 1) * HEAD_W) for hh in range(hp)]
    vrows = [slice(hh * VT_ROWS, (hh + 1) * VT_ROWS) for hh in range(hp)]
    kb_ref[...] = km_ref[0].astype(BF16)
    ones = jnp.ones((SUM_ROWS, tq), BF16)
    for j in range(sq // tq):
        for hh, sl in enumerate(heads):
            vt_ref[j, hh * VT_ROWS:hh * VT_ROWS + HEAD_W, :] = vm_ref[0, j * tq:(j + 1) * tq, sl].T.astype(BF16)
            vt_ref[j, hh * VT_ROWS + HEAD_W:(hh + 1) * VT_ROWS, :] = ones
    ke = [ke_ref[0, :, sl].astype(BF16) for sl in heads]
    vet = [jnp.concatenate([ve_ref[0, :, sl].T.astype(BF16), jnp.ones((SUM_ROWS, ve_ref.shape[1]), BF16)], axis=0)
           for sl in heads]
    lam = lam_ref[0]
    lo = lax.broadcasted_iota(jnp.int32, (1, HEAD_W), 1) < DIFF_DH
    ext_ok = lax.broadcasted_iota(jnp.int32, (ke_ref.shape[1], 1), 0) < n_ext
    diag = (lax.broadcasted_iota(jnp.int32, (tq, tq), 0) // CHUNK
            <= lax.broadcasted_iota(jnp.int32, (tq, tq), 1) // CHUNK)

    def q_tile(qi):
        q0 = qi * tq
        qs = []
        for sl in heads:
            q = q_ref[0, pl.ds(q0, tq), sl]
            zero = jnp.zeros_like(q)
            qs += [jnp.where(lo, q, zero), jnp.where(lo, zero, q)]

        ext_s = [jnp.where(ext_ok, _scores(ke[c // 2], qq), NEG) for c, qq in enumerate(qs)]
        carry = _softmax_seed_t(ext_s, [vet[c // 2] for c in range(len(qs))])

        def step(j, carry, mask=None):
            ss = [_scores(kb_ref[j * tq:(j + 1) * tq, heads[c // 2]], qq) for c, qq in enumerate(qs)]
            if mask is not None:
                ss = [jnp.where(mask, s, NEG) for s in ss]
            return tuple(_softmax_step_t(carry, ss, [vt_ref[j, vrows[c // 2], :] for c in range(len(qs))]))

        carry = tuple(carry)
        for j in range(qi):
            carry = step(j, carry)
        carry = step(qi, carry, diag)
        for hh, sl in enumerate(heads):
            _, a1, _, a2 = carry[4 * hh:4 * hh + 4]
            o = a1[:HEAD_W] / a1[HEAD_W:HEAD_W + 1] - lam * (a2[:HEAD_W] / a2[HEAD_W:HEAD_W + 1])
            on = o * lax.rsqrt(jnp.mean(o * o, axis=0, keepdims=True) + EPS) * gst_ref[...] * coef
            o_ref[0, q0:q0 + tq, sl] = on.T.astype(BF16)

    for qi in range(sq // tq):
        q_tile(qi)


def _attention_prompt(q, k_main, v_main, k_ext, v_ext, n_ext, lam, g_subln, coef, tq, hp):
    b, sq, hw = q.shape
    w = hp * HEAD_W
    bh = pl.BlockSpec((1, sq, w), lambda i, h: (i, 0, h))
    ext = pl.BlockSpec((1, k_ext.shape[1], w), lambda i, h: (0, 0, h))
    gst = jnp.broadcast_to(g_subln.reshape(HEAD_W, 1), (HEAD_W, tq))
    return pl.pallas_call(
        functools.partial(_attn_prompt_kernel, sq=sq, tq=tq, hp=hp, n_ext=n_ext, coef=coef),
        grid=(b, hw // w),
        in_specs=[pl.BlockSpec(memory_space=pltpu.SMEM), bh, bh, bh, ext, ext,
                  pl.BlockSpec((HEAD_W, tq), lambda i, h: (0, 0))],
        out_specs=bh,
        out_shape=jax.ShapeDtypeStruct(q.shape, BF16),
        scratch_shapes=[pltpu.VMEM((sq, w), BF16), pltpu.VMEM((sq // tq, hp * VT_ROWS, tq), BF16)],
        compiler_params=_cparams(("arbitrary", "arbitrary")),
        name="attention_prompt",
    )(lam, q, k_main, v_main, k_ext, v_ext, gst)


def _attn_sample_kernel(lam_ref, q_ref, kc_ref, vc_ref, ke_ref, ve_ref, gs_ref, o_ref, m_ref, l_ref, acc_ref,
                        *, ds, nh, n_ext, coef):
    t = pl.program_id(1)
    tkv = kc_ref.shape[1] // nh
    lo = lax.broadcasted_iota(jnp.int32, (1, HEAD_W), 1) < DIFF_DH
    ext_ok = lax.broadcasted_iota(jnp.int32, (1, ke_ref.shape[1]), 1) < n_ext
    heads = [slice(hh * HEAD_W, (hh + 1) * HEAD_W) for hh in range(nh)]

    def stacked_q(sl):
        q = q_ref[0, :, sl]
        zero = jnp.zeros_like(q)
        return jnp.concatenate([jnp.where(lo, q, zero), jnp.where(lo, zero, q)], axis=0)

    qs = [stacked_q(sl) for sl in heads]

    def write_stats(stats):
        for hh in range(nh):
            m_ref[hh], l_ref[hh], acc_ref[hh] = stats[3 * hh:3 * hh + 3]

    @pl.when(t == 0)
    def _():
        ext_s = [jnp.where(ext_ok, _scores(qq, ke_ref[0, :, sl].astype(BF16)), NEG) for qq, sl in zip(qs, heads)]
        write_stats(_softmax_seed(ext_s, [ve_ref[0, :, sl].astype(BF16) for sl in heads]))

    carry = []
    for hh in range(nh):
        carry += [m_ref[hh], l_ref[hh], acc_ref[hh]]
    ss = [_scores(qq, kc_ref[0, pl.ds(hh, tkv, stride=nh), :].astype(BF16)) for hh, qq in enumerate(qs)]
    write_stats(_softmax_step(carry, ss, [vc_ref[0, pl.ds(hh, tkv, stride=nh), :].astype(BF16) for hh in range(nh)]))

    @pl.when(t == pl.num_programs(1) - 1)
    def _():
        lam = lam_ref[0]
        for hh, sl in enumerate(heads):
            o = acc_ref[hh] / l_ref[hh]
            o_ref[0, :, sl] = _sub_norm(o[:ds], o[ds:], lam, gs_ref[...], coef).astype(BF16)


def _attention_sample(q, k_cache, v_cache, k_ext, v_ext, n_ext, lam, g_subln, coef, tkv):
    b, ds, hw = q.shape
    _, pl_len, nh, _ = k_cache.shape
    per_b = lambda rows: pl.BlockSpec((1, rows, hw), lambda i, t: (i, 0, 0))
    cache = pl.BlockSpec((1, tkv * nh, HEAD_W), lambda i, t: (i, t, 0))
    return pl.pallas_call(
        functools.partial(_attn_sample_kernel, ds=ds, nh=nh, n_ext=n_ext, coef=coef),
        grid=(b, pl_len // tkv),
        in_specs=[pl.BlockSpec(memory_space=pltpu.SMEM), per_b(ds), cache, cache,
                  per_b(k_ext.shape[1]), per_b(k_ext.shape[1]),
                  pl.BlockSpec((1, HEAD_W), lambda i, t: (0, 0))],
        out_specs=per_b(ds),
        out_shape=jax.ShapeDtypeStruct(q.shape, BF16),
        scratch_shapes=[pltpu.VMEM((nh, 2 * ds, 1), F32), pltpu.VMEM((nh, 2 * ds, 1), F32),
                        pltpu.VMEM((nh, 2 * ds, HEAD_W), F32)],
        compiler_params=_cparams(("arbitrary", "arbitrary")),
        name="attention_sample",
    )(lam, q, k_cache.reshape(b, pl_len * nh, HEAD_W), v_cache.reshape(b, pl_len * nh, HEAD_W),
      k_ext, v_ext, g_subln)


def _mix_kernel(x_ref, yr_ref, od_ref, ga_ref, gb_ref, wr_ref, wd_ref, wo_ref, gf_ref, wrt_ref, br_ref,
                x1_ref, h_ref, lgt_ref):
    ya = jnp.dot(yr_ref[...], wr_ref[...], preferred_element_type=F32)
    yb = jnp.dot(od_ref[...], wd_ref[...], preferred_element_type=F32)
    z = jax.nn.sigmoid(ga_ref[...].astype(F32)) * ya + jax.nn.sigmoid(gb_ref[...].astype(F32)) * yb
    x1 = x_ref[...] + jnp.dot(z.astype(BF16), wo_ref[...], preferred_element_type=F32)
    x1_ref[...] = x1
    h = x1 * lax.rsqrt(jnp.mean(x1 * x1, axis=-1, keepdims=True) + EPS) * gf_ref[...]
    _tile_rows_store(h_ref, h)
    lgt_ref[...] = _scores(wrt_ref[...], h.astype(BF16)) + br_ref[...]


def _mix_tail_kernel(*refs):
    th_ref, tlgt_ref = refs[11:13]
    h_ref, lgt_ref = refs[14:16]
    last = pl.num_programs(0) - 1

    @pl.when(pl.program_id(0) < last)
    def _():
        _mix_kernel(*refs[:11], *refs[13:])

    @pl.when(pl.program_id(0) == last)
    def _():
        h_ref[...] = th_ref[...]
        lgt_ref[...] = tlgt_ref[...]


def _mix_ret_kernel(x_ref, rq_ref, rk_ref, rv_ref, rg_ref, s0_ref, od_ref, ga_ref, gb_ref,
                    wr_ref, wd_ref, wo_ref, gf_ref, wrt_ref, br_ref, th_ref, tlgt_ref,
                    x1_ref, h_ref, lgt_ref, st_ref, decay_ref, yr_ref, *, cb, tiles_per_seq):
    i = pl.program_id(0)
    last = pl.num_programs(0) - 1

    @pl.when(i == 0)
    def _():
        _fill_decay(decay_ref, cb)

    @pl.when(i < last)
    def _():
        @pl.when(i % tiles_per_seq == 0)
        def _():
            st_ref[...] = s0_ref[...]

        refs = {"q": rq_ref, "k": rk_ref, "v": rv_ref, "g": rg_ref}
        for sub in range(yr_ref.shape[0] // cb):
            rows = slice(sub * cb, (sub + 1) * cb)

            def store(cols, y, rows=rows):
                yr_ref[rows, cols] = y

            _ret_block(lambda name, cols, rows=rows: refs[name][rows, cols], store, st_ref, decay_ref, cb, cb)
        _mix_kernel(x_ref, yr_ref, od_ref, ga_ref, gb_ref, wr_ref, wd_ref, wo_ref, gf_ref, wrt_ref, br_ref,
                    x1_ref, h_ref, lgt_ref)

    @pl.when(i == last)
    def _():
        h_ref[...] = th_ref[...]
        lgt_ref[...] = tlgt_ref[...]


def _mix_ret(x, rq, rk, rv, rg, s0, od, ga, gb, wr, wd, wo, gf, wrt, br, tm, tail, seq, cb):
    t, d = x.shape
    ne = wrt.shape[0]
    nt = t // tm
    tiles_per_seq = seq // tm
    row = lambda w: pl.BlockSpec((tm, w), lambda i: (jnp.minimum(i, nt - 1), 0))
    full = lambda a: pl.BlockSpec(a.shape, lambda i: (0,) * a.ndim)
    st_block = (1, RET_HEADS, RET_DK, RET_DV)
    weights = (wr, wd, wo, gf, wrt, br) + tuple(tail)
    return pl.pallas_call(
        functools.partial(_mix_ret_kernel, cb=cb, tiles_per_seq=tiles_per_seq),
        grid=(nt + 1,),
        in_specs=[row(d), row(rq.shape[1]), row(rk.shape[1]), row(rv.shape[1]), row(rg.shape[1]), full(s0),
                  row(od.shape[1]), row(d), row(d)] + [full(a) for a in weights],
        out_specs=[row(d), pl.BlockSpec((tm * SUBLANES, LANES), lambda i: (i, 0)),
                   pl.BlockSpec((ne, tm), lambda i: (0, i)),
                   pl.BlockSpec(st_block, lambda i: (jnp.minimum(i, nt - 1) // tiles_per_seq, 0, 0, 0))],
        out_shape=[jax.ShapeDtypeStruct((t, d), F32),
                   jax.ShapeDtypeStruct(((t + tm) * SUBLANES, LANES), F32),
                   jax.ShapeDtypeStruct((ne, t + tm), F32),
                   jax.ShapeDtypeStruct((t // seq,) + st_block[1:], F32)],
        scratch_shapes=[pltpu.VMEM((RET_HEADS, cb, cb), F32), pltpu.VMEM((tm, rv.shape[1]), BF16)],
        compiler_params=_cparams(("arbitrary",)),
        name="mix_retention",
    )(x, rq, rk, rv, rg, s0, od, ga, gb, *weights)


def _mix(x, yr, od, ga, gb, wr, wd, wo, gf, wrt, br, tm, tail=None):
    t, d = x.shape
    ne = wrt.shape[0]
    nt = t // tm
    extra = 0 if tail is None else 1
    row = lambda w: pl.BlockSpec((tm, w), lambda i: (jnp.minimum(i, nt - 1), 0))
    full = lambda a: pl.BlockSpec(a.shape, lambda i: (0, 0))
    ops = (x, yr, od, ga, gb, wr, wd, wo, gf, wrt, br) + (() if tail is None else tuple(tail))
    return pl.pallas_call(
        _mix_kernel if tail is None else _mix_tail_kernel,
        grid=(nt + extra,),
        in_specs=[row(d), row(yr.shape[1]), row(od.shape[1]), row(d), row(d)] + [full(a) for a in ops[5:]],
        out_specs=[row(d), pl.BlockSpec((tm * SUBLANES, LANES), lambda i: (i, 0)),
                   pl.BlockSpec((ne, tm), lambda i: (0, i))],
        out_shape=[jax.ShapeDtypeStruct((t, d), F32),
                   jax.ShapeDtypeStruct(((t + extra * tm) * SUBLANES, LANES), F32),
                   jax.ShapeDtypeStruct((ne, t + extra * tm), F32)],
        compiler_params=_cparams(("arbitrary",)),
        name="mix",
    )(*ops)


def _route_kernel(lgt_ref, e_ref, gate_ref, rank_ref, cnt_ref):
    @pl.when(pl.program_id(0) == 0)
    def _():
        cnt_ref[...] = jnp.zeros_like(cnt_ref)

    v = lgt_ref[...]
    ne, tn = v.shape
    eid = lax.broadcasted_iota(jnp.int32, (ne, tn), 0)
    tops, sels = [], []
    for k in range(TOP_K):
        m = jnp.max(v, axis=0, keepdims=True)
        idx = jnp.min(jnp.where(v == m, eid, ne), axis=0, keepdims=True)
        sel = eid == idx
        e_ref[k:k + 1, :] = idx
        tops.append(m)
        sels.append(sel)
        v = jnp.where(sel, -jnp.inf, v)
    ex = [jnp.exp(m - tops[0]) for m in tops]
    den = ex[0] + ex[1] + ex[2] + ex[3]
    for k in range(TOP_K):
        gate_ref[k:k + 1, :] = ex[k] / den

    chosen = sels[0] | sels[1] | sels[2] | sels[3]
    before = (lax.broadcasted_iota(jnp.int32, (tn, tn), 0) < lax.broadcasted_iota(jnp.int32, (tn, tn), 1))
    prior = jnp.dot(chosen.astype(BF16), before.astype(BF16), preferred_element_type=F32) + cnt_ref[:, 0:1]
    for k in range(TOP_K):
        rank_ref[k:k + 1, :] = jnp.sum(jnp.where(sels[k], prior, 0.0), axis=0, keepdims=True).astype(jnp.int32)
    cnt_ref[...] = cnt_ref[...] + jnp.sum(chosen.astype(F32), axis=1, keepdims=True)


def _route_topk(lgt, tn):
    ne, t = lgt.shape
    kt = pl.BlockSpec((TOP_K, tn), lambda i: (0, i))
    return pl.pallas_call(
        _route_kernel,
        grid=(t // tn,),
        in_specs=[pl.BlockSpec((ne, tn), lambda i: (0, i))],
        out_specs=[kt, kt, kt, pl.BlockSpec((ne, LANES), lambda i: (0, 0))],
        out_shape=[jax.ShapeDtypeStruct((TOP_K, t), jnp.int32), jax.ShapeDtypeStruct((TOP_K, t), F32),
                   jax.ShapeDtypeStruct((TOP_K, t), jnp.int32), jax.ShapeDtypeStruct((ne, LANES), F32)],
        compiler_params=_cparams(("arbitrary",)),
        name="route",
    )(lgt)


GATHER_UNROLL = 16
MOE_GATHER_PRIORITIES = (1,)


def _gather_rows(src_hbm, idx_ref, n, buf_ref, slot, sem, priorities=(0, 1)):
    def body(g, _):
        for u in range(GATHER_UNROLL):
            r = g * GATHER_UNROLL + u
            src = pl.multiple_of(idx_ref[0, 0, r], SUBLANES)
            dst = pl.multiple_of(r * SUBLANES, SUBLANES)
            pltpu.make_async_copy(src_hbm.at[pl.ds(src, SUBLANES)], buf_ref.at[slot, pl.ds(dst, SUBLANES)],
                                  sem.at[slot]).start(priority=priorities[u % len(priorities)])
        return 0
    lax.fori_loop(0, n // GATHER_UNROLL, body, 0)


def _wait_rows(src_hbm, n, buf_ref, slot, sem):
    pltpu.make_async_copy(src_hbm.at[pl.ds(0, n * SUBLANES)], buf_ref.at[slot], sem.at[slot]).wait()


MOE_SLOTS = 3


def _moe_kernel(be_ref, nu_ref, tok_ref, tok1_ref, tok2_ref, h_hbm, wgu_ref, bgu_ref, wdn_ref, bdn_ref, out_ref,
                buf_ref, wgu_bf, wdn_bf, sem):
    i = pl.program_id(0)
    n_used = nu_ref[0]
    slot = i % MOE_SLOTS
    d_ff = wdn_bf.shape[0]

    @pl.when(i == 0)
    def _():
        _gather_rows(h_hbm, tok_ref, MOE_ROWS, buf_ref, 0, sem, MOE_GATHER_PRIORITIES)

        @pl.when(1 < n_used)
        def _():
            _gather_rows(h_hbm, tok1_ref, MOE_ROWS, buf_ref, 1, sem, MOE_GATHER_PRIORITIES)

    @pl.when(i + 2 < n_used)
    def _():
        _gather_rows(h_hbm, tok2_ref, MOE_ROWS, buf_ref, (i + 2) % MOE_SLOTS, sem, MOE_GATHER_PRIORITIES)

    @pl.when(i < n_used)
    def _():
        @pl.when((i == 0) | (be_ref[i] != be_ref[jnp.maximum(i - 1, 0)]))
        def _():
            wgu_bf[...] = wgu_ref[0].astype(BF16)
            wdn_bf[...] = wdn_ref[0].astype(BF16)

        _wait_rows(h_hbm, MOE_ROWS, buf_ref, slot, sem)

        xb = _tile_rows_load(buf_ref, slot, 0, MOE_ROWS).astype(BF16)
        gu = jnp.dot(xb, wgu_bf[...], preferred_element_type=F32) + bgu_ref[0]
        gate = jnp.minimum(gu[:, :d_ff], SWIGLU_LIMIT)
        up = jnp.clip(gu[:, d_ff:], -SWIGLU_LIMIT, SWIGLU_LIMIT)
        act = (up + 1.0) * gate * jax.nn.sigmoid(SWIGLU_ALPHA * gate)
        _tile_rows_store(out_ref, jnp.dot(act.astype(BF16), wdn_bf[...], preferred_element_type=F32) + bdn_ref[0])

    @pl.when(i >= n_used)
    def _():
        out_ref[...] = jnp.zeros_like(out_ref)


def _moe(h, tok, block_e, n_used, w_gu, b_gu, w_down, b_down):
    d = h.shape[1] * SUBLANES
    nb = tok.shape[0]
    ne, _, two_ff = w_gu.shape
    d_ff = two_ff // 2
    e_map = lambda i, be, nu: (be[i], 0, 0)
    grid_spec = pltpu.PrefetchScalarGridSpec(
        num_scalar_prefetch=2,
        grid=(nb,),
        in_specs=[
            pl.BlockSpec((1, 1, MOE_ROWS), lambda i, be, nu: (i, 0, 0), memory_space=pltpu.SMEM),
            pl.BlockSpec((1, 1, MOE_ROWS), lambda i, be, nu: (jnp.minimum(i + 1, nb - 1), 0, 0),
                         memory_space=pltpu.SMEM),
            pl.BlockSpec((1, 1, MOE_ROWS), lambda i, be, nu: (jnp.minimum(i + 2, nb - 1), 0, 0),
                         memory_space=pltpu.SMEM),
            pl.BlockSpec(memory_space=pl.ANY),
            pl.BlockSpec((1, d, two_ff), e_map),
            pl.BlockSpec((1, 1, two_ff), e_map),
            pl.BlockSpec((1, d_ff, d), e_map),
            pl.BlockSpec((1, 1, d), e_map),
        ],
        out_specs=pl.BlockSpec((MOE_ROWS * SUBLANES, LANES), lambda i, be, nu: (i, 0)),
        scratch_shapes=[pltpu.VMEM((MOE_SLOTS, MOE_ROWS * SUBLANES, LANES), F32),
                        pltpu.VMEM((d, two_ff), BF16),
                        pltpu.VMEM((d_ff, d), BF16),
                        pltpu.SemaphoreType.DMA((MOE_SLOTS,))],
    )
    return pl.pallas_call(
        _moe_kernel,
        grid_spec=grid_spec,
        out_shape=jax.ShapeDtypeStruct((nb * MOE_ROWS * SUBLANES, LANES), F32),
        compiler_params=_cparams(("arbitrary",)),
        name="moe",
    )(block_e, n_used, tok, tok, tok, h, w_gu, b_gu.reshape(ne, 1, two_ff), w_down, b_down.reshape(ne, 1, d))


def _combine_kernel(dst_ref, dstn_ref, x_ref, gate_ref, gf_ref, rows_hbm, y_ref, buf_ref, sem, *, tm, nsteps):
    i = pl.program_id(0)
    slot = i % 2
    n = tm * TOP_K

    @pl.when(i == 0)
    def _():
        _gather_rows(rows_hbm, dst_ref, n, buf_ref, 0, sem)

    @pl.when(i + 1 < nsteps)
    def _():
        _gather_rows(rows_hbm, dstn_ref, n, buf_ref, 1 - slot, sem)

    _wait_rows(rows_hbm, n, buf_ref, slot, sem)

    y = x_ref[...]
    gate = gate_ref[...]
    for k in range(TOP_K):
        y = y + gate[:, k:k + 1] * _tile_rows_load(buf_ref, slot, k * tm, tm)
    y_ref[...] = y * lax.rsqrt(jnp.mean(y * y, axis=-1, keepdims=True) + EPS) * gf_ref[...]


def _combine(x1, gates, dest, rows, g_final, tm):
    t, d = x1.shape
    nsteps = t // tm
    n = tm * TOP_K
    return pl.pallas_call(
        functools.partial(_combine_kernel, tm=tm, nsteps=nsteps),
        grid=(nsteps,),
        in_specs=[pl.BlockSpec((1, 1, n), lambda i: (i, 0, 0), memory_space=pltpu.SMEM),
                  pl.BlockSpec((1, 1, n), lambda i: (jnp.minimum(i + 1, nsteps - 1), 0, 0),
                               memory_space=pltpu.SMEM),
                  pl.BlockSpec((tm, d), lambda i: (i, 0)),
                  pl.BlockSpec((tm, TOP_K), lambda i: (i, 0)),
                  pl.BlockSpec((1, d), lambda i: (0, 0)),
                  pl.BlockSpec(memory_space=pl.ANY)],
        out_specs=pl.BlockSpec((tm, d), lambda i: (i, 0)),
        out_shape=jax.ShapeDtypeStruct((t, d), F32),
        scratch_shapes=[pltpu.VMEM((2, n * SUBLANES, LANES), F32), pltpu.SemaphoreType.DMA((2,))],
        compiler_params=_cparams(("arbitrary",)),
        name="combine",
    )(dest, dest, x1, gates, g_final, rows)


def _route(lgt, tn):
    t = lgt.shape[1]
    n_assign = t * TOP_K
    e4, gates, rank, cnt = _route_topk(lgt, tn)
    counts = cnt[:, 0].astype(jnp.int32)
    padded = (counts + MOE_ROWS - 1) // MOE_ROWS * MOE_ROWS
    pend = jnp.cumsum(padded)
    pstart = pend - padded
    start = jnp.cumsum(counts) - counts
    e_ids = jnp.arange(N_EXPERTS, dtype=jnp.int32)
    dest = rank + jnp.sum(jnp.where(e4[None] == e_ids[:, None, None], pstart[:, None, None], 0), axis=0)
    nb = n_assign // MOE_ROWS + N_EXPERTS
    n_used = (pend[-1] // MOE_ROWS).astype(jnp.int32).reshape(1)
    blk0 = jnp.arange(nb, dtype=jnp.int32) * MOE_ROWS
    block_e = jnp.minimum(jnp.sum((pend[None, :] <= blk0[:, None]).astype(jnp.int32), axis=1), N_EXPERTS - 1)
    of_block = lambda table: jnp.sum(jnp.where(block_e[:, None] == e_ids[None, :], table[None, :], 0), axis=1)
    bits = max(n_assign - 1, 1).bit_length()
    assert N_EXPERTS << bits < 2 ** 31
    keys = (e4.T.reshape(-1) << bits) + jnp.arange(n_assign, dtype=jnp.int32)
    order = jnp.sort(keys) & ((1 << bits) - 1)
    within = (blk0 - of_block(pstart))[:, None] + jnp.arange(MOE_ROWS, dtype=jnp.int32)[None, :]
    src = jnp.clip(of_block(start)[:, None] + jnp.minimum(within, of_block(counts)[:, None] - 1), 0, n_assign - 1)
    tok = (order[src] // TOP_K).astype(jnp.int32)
    return gates.T, dest * SUBLANES, (tok * SUBLANES).reshape(nb, 1, MOE_ROWS), block_e, n_used


def kernel(x_prompt, x_sample, cache_k, cache_v, state_ret, meta_tokens, g_mix, w_in,
           lam_q1, lam_k1, lam_q2, lam_k2, g_subln, w_ret_out, w_diff_out, w_out,
           g_ffn, w_router, b_router, w_gu, b_gu, w_down, b_down, g_final):
    b, s, d = x_prompt.shape
    db, ds, _ = x_sample.shape
    depth, _, pl_len = cache_k.shape[:3]
    assert depth == 1, "single-layer step only"
    nm = meta_tokens.shape[0]
    hw = DIFF_HEADS * HEAD_W

    lam_init = 0.8 - 0.6 * math.exp(-0.3 * 0)
    lam = (jnp.exp(jnp.sum(lam_q1[0] * lam_k1[0])) - jnp.exp(jnp.sum(lam_q2[0] * lam_k2[0]))
           + lam_init).reshape(1).astype(F32)
    coef = 1.0 - lam_init

    w_in_bf = w_in[0].astype(BF16)
    g_mix2 = g_mix[0].reshape(1, d)

    tm = 512 if (b * s) % 512 == 0 else s
    tab_p = _tables(N_META + jnp.arange(s, dtype=jnp.int32))
    rq, rk, rv, rg, dq, dk, dv, ga, gb = _proj(x_prompt.reshape(b * s, d), g_mix2, w_in_bf, tab_p, tm, s // tm)

    pos_small = jnp.concatenate([jnp.tile(N_META + pl_len + jnp.arange(ds, dtype=jnp.int32), db),
                                 jnp.arange(nm, dtype=jnp.int32)])
    x_small = jnp.concatenate([x_sample.reshape(db * ds, d), meta_tokens.astype(F32)], axis=0)
    small = _proj(x_small, g_mix2, w_in_bf, _tables(pos_small), x_small.shape[0], 1)
    ns = db * ds
    srq, srk, srv, srg, sdq, sdk, sdv, sga, sgb = [a[:ns] for a in small]
    mrq, mrk, mrv, mrg, _, mdk, mdv, _, _ = [a[ns:] for a in small]

    cb_small = 128

    def pad_rows(a, nb_, n_):
        a = a.reshape(nb_, n_, a.shape[-1])
        return jnp.pad(a, ((0, 0), (0, cb_small - n_), (0, 0)))

    zero_state = jnp.zeros((1, RET_HEADS, RET_DK, RET_DV), F32)
    _, s_meta = _retention(pad_rows(mrq, 1, nm), pad_rows(mrk, 1, nm), pad_rows(mrv, 1, nm),
                           pad_rows(mrg, 1, nm), zero_state, cb_small, nm)
    cb = 256
    r3 = lambda a: a.reshape(b, s, a.shape[-1])
    syr, ret_s = _retention(pad_rows(srq, db, ds), pad_rows(srk, db, ds), pad_rows(srv, db, ds),
                            pad_rows(srg, db, ds), state_ret[0], cb_small, ds)
    syr = syr[:, :ds].reshape(ns, -1)

    g_sub = g_subln[0].reshape(1, HEAD_W)
    ext_rows = 128
    pad_ext = lambda a: jnp.pad(a, ((0, 0), (0, ext_rows - a.shape[1]), (0, 0)))
    od = _attention_prompt(r3(dq), r3(dk), r3(dv), pad_ext(mdk[None]), pad_ext(mdv[None]), nm,
                           lam, g_sub, coef, 256, 4)
    sdk3, sdv3 = sdk.reshape(db, ds, hw), sdv.reshape(db, ds, hw)
    ke_s = jnp.concatenate([jnp.broadcast_to(mdk[None], (db, nm, hw)), sdk3], axis=1)
    ve_s = jnp.concatenate([jnp.broadcast_to(mdv[None], (db, nm, hw)), sdv3], axis=1)
    sod = _attention_sample(sdq.reshape(db, ds, hw), cache_k[0], cache_v[0], pad_ext(ke_s), pad_ext(ve_s),
                            nm + ds, lam, g_sub, coef, min(2048, pl_len))

    wr, wd, wo = w_ret_out[0].astype(BF16), w_diff_out[0].astype(BF16), w_out[0].astype(BF16)
    gf = g_ffn[0].reshape(1, d)
    wrt = w_router[0].T.astype(BF16)
    br = b_router[0].astype(F32).reshape(N_EXPERTS, 1)
    sx1, sh, slgt = _mix(x_sample.reshape(ns, d), syr, sod.reshape(ns, hw), sga, sgb, wr, wd, wo, gf, wrt, br, ns)
    x1, h, lgt, ret_p = _mix_ret(x_prompt.reshape(b * s, d), rq, rk, rv, rg, s_meta, od.reshape(b * s, hw), ga, gb,
                                 wr, wd, wo, gf, wrt, br, ns, (sh, slgt), s, min(cb, ns))

    t = b * s
    gates, dest, tok, block_e, n_used = _route(lgt, ns)
    rows = _moe(h, tok, block_e, n_used, w_gu[0], b_gu[0], w_down[0], b_down[0])
    gfin = g_final.reshape(1, d)

    def combine(x1g, lo, n):
        tc = min(512, n)
        dg = dest[:, lo:lo + n].reshape(TOP_K, n // tc, tc).swapaxes(0, 1).reshape(n // tc, 1, TOP_K * tc)
        return _combine(x1g, gates[lo:lo + n], dg, rows, gfin, tc)

    y_p = combine(x1, 0, t)
    y_s = combine(sx1, t, ns)

    k_p = jnp.concatenate([jnp.broadcast_to(mdk[None], (b, nm, hw)), r3(dk)], axis=1)
    v_p = jnp.concatenate([jnp.broadcast_to(mdv[None], (b, nm, hw)), r3(dv)], axis=1)
    shp = lambda a: a.reshape(1, a.shape[0], a.shape[1], DIFF_HEADS, HEAD_W)
    return (y_p.reshape(b, s, d), y_s.reshape(db, ds, d), ret_p[None], shp(k_p), shp(v_p),
            ret_s[None], shp(sdk3), shp(sdv3))
```
